```python
import math, functools
import jax, jax.numpy as jnp
from jax import lax
import numpy as np

D_MODEL = 1024
BATCH = 1
SEQ = 16384
DEPTH = 2
DEC_BATCH = 128
DEC_SEQ = 4
PAST_LEN = 16384
PAGE_SIZE = 128

D_PLE = 256
NORM_EPS = 1e-6
N_BRANCH = 3
A_WIDTH = D_MODEL // 2
A_GROUPS = 4
A_GROUP_DIM = A_WIDTH // A_GROUPS
CHUNK = 128
B_INNER = D_MODEL
B_HEADDIM = 64
B_HEADS = B_INNER // B_HEADDIM
B_GROUPS = 4
B_STATE = 128
B_CONV = 4
B_CONV_DIM = B_INNER + 2 * B_GROUPS * B_STATE
SSD_CHUNK = 128
C_HEADS = 8
C_NOPE = 64
C_ROPE = 32
C_V = 64
C_KV_LORA = 256
C_Q_LORA = 384
ROPE_BASE = 10000.0
ATTN_BLOCK = 128
C_SCALE = (C_NOPE + C_ROPE) ** -0.5
D_FF = 4 * D_MODEL
IN_SPLITS = (N_BRANCH * D_MODEL, A_WIDTH, A_WIDTH, B_INNER, B_CONV_DIM, B_HEADS, C_Q_LORA, C_KV_LORA, C_ROPE)
D_IN = N_BRANCH * D_MODEL + 2 * A_WIDTH + B_INNER + B_CONV_DIM + B_HEADS + C_Q_LORA + C_KV_LORA + C_ROPE

kernel_name = 'hybrid_gmlp_ssd_mla_decode_step'


def _rmsnorm(x, w):
    xf = x.astype(jnp.float32)
    y = xf * lax.rsqrt(jnp.mean(xf * xf, axis=-1, keepdims=True) + NORM_EPS)
    return (y * w.astype(jnp.float32)).astype(x.dtype)


def _layernorm(x, w, b):
    xf = x.astype(jnp.float32)
    mu = jnp.mean(xf, axis=-1, keepdims=True)
    var = jnp.mean(jnp.square(xf - mu), axis=-1, keepdims=True)
    y = (xf - mu) * lax.rsqrt(var + NORM_EPS) * w.astype(jnp.float32) + b.astype(jnp.float32)
    return y.astype(x.dtype)


def _rope(x, pos):
    half = C_ROPE // 2
    inv = jnp.power(ROPE_BASE, -jnp.arange(half, dtype=jnp.float32) * (2.0 / C_ROPE))
    ang = pos.astype(jnp.float32)[:, None] * inv[None, :]
    shape = (1, pos.shape[0]) + (1,) * (x.ndim - 3) + (half,)
    cos = jnp.cos(ang).reshape(shape)
    sin = jnp.sin(ang).reshape(shape)
    xf = x.astype(jnp.float32)
    x1, x2 = xf[..., :half], xf[..., half:]
    return jnp.concatenate([x1 * cos - x2 * sin, x1 * sin + x2 * cos], axis=-1).astype(x.dtype)


def _spatial_gate(u, v, w_s, b_s, L):
    b, S, _ = v.shape
    vg = v.reshape(b, S // L, L, A_GROUPS, A_GROUP_DIM)
    w = jnp.tril(w_s[:, :L, :L])
    s = jnp.einsum('gts,bnsgc->bntgc', w, vg) + b_s[:, :L].T[None, None, :, :, None]
    return u * s.reshape(b, S, A_WIDTH)


def _causal_conv(xbc, prev, w, bias):
    S = xbc.shape[1]
    xp = jnp.concatenate([prev.astype(xbc.dtype), xbc], axis=1)
    acc = xp[:, 0:S] * w[0]
    for k in range(1, B_CONV):
        acc = acc + xp[:, k:k + S] * w[k]
    return jax.nn.silu(acc + bias), xp[:, -(B_CONV - 1):]


def _ssd(x, dt, a, bm, cm, h0, L):
    b, S, H, P = x.shape
    nc = S // L
    rep = H // B_GROUPS
    f32 = jnp.float32
    xf = x.astype(f32).reshape(b, nc, L, H, P)
    bh = jnp.repeat(bm.astype(f32), rep, axis=2).reshape(b, nc, L, H, B_STATE)
    ch = jnp.repeat(cm.astype(f32), rep, axis=2).reshape(b, nc, L, H, B_STATE)
    dtc = dt.reshape(b, nc, L, H)
    xdt = xf * dtc[..., None]
    cs = jnp.cumsum(dtc * a, axis=2)
    causal = jnp.tril(jnp.ones((L, L), dtype=bool))
    seg = cs[:, :, :, None, :] - cs[:, :, None, :, :]
    decay = jnp.exp(jnp.where(causal[None, None, :, :, None], seg, -jnp.inf))
    scores = jnp.einsum('bcthn,bcshn->bctsh', ch, bh) * decay
    y_diag = jnp.einsum('bctsh,bcshp->bcthp', scores, xdt)
    decay_end = jnp.exp(cs[:, :, -1:, :] - cs)
    st = jnp.einsum('bclhn,bclhp->bchpn', bh * decay_end[..., None], xdt)
    chunk_decay = jnp.exp(cs[:, :, -1, :])

    def step(h, inp):
        dec, s = inp
        return dec[:, :, None, None] * h + s, h

    h_last, h_prev = lax.scan(step, h0.astype(f32),
                              (chunk_decay.transpose(1, 0, 2), st.transpose(1, 0, 2, 3, 4)))
    h_prev = h_prev.transpose(1, 0, 2, 3, 4)
    y_off = jnp.einsum('bclhn,bchpn->bclhp', ch * jnp.exp(cs)[..., None], h_prev)
    y = (y_diag + y_off).reshape(b, S, H, P).astype(x.dtype)
    return y, h_last.astype(h0.dtype)


def _prompt_attention(q_nope, q_rope, c_kv, k_rope, w_uk, w_uv):
    b, S, H, _ = q_nope.shape
    k_nope = jnp.einsum('bsc,chd->bshd', c_kv, w_uk)
    v = jnp.einsum('bsc,chd->bshd', c_kv, w_uv)
    nb = S // ATTN_BLOCK
    qn = q_nope.reshape(b, nb, ATTN_BLOCK, H, C_NOPE).transpose(1, 0, 2, 3, 4)
    qr = q_rope.reshape(b, nb, ATTN_BLOCK, H, C_ROPE).transpose(1, 0, 2, 3, 4)
    kpos = jnp.arange(S)

    def blk(args):
        i, qn_b, qr_b = args
        qpos = i * ATTN_BLOCK + jnp.arange(ATTN_BLOCK)
        s = (jnp.einsum('bthd,bshd->bhts', qn_b, k_nope)
             + jnp.einsum('bthr,bsr->bhts', qr_b, k_rope)).astype(jnp.float32) * C_SCALE
        s = jnp.where(kpos[None, :] <= qpos[:, None], s, -jnp.inf)
        p = jax.nn.softmax(s, axis=-1).astype(v.dtype)
        return jnp.einsum('bhts,bshd->bthd', p, v)

    o = lax.map(blk, (jnp.arange(nb), qn, qr))
    return o.transpose(1, 0, 2, 3, 4).reshape(b, S, H, C_V)


def _sample_attention(q_nope, q_rope, c_kv, k_rope, w_uk, w_uv, past_kv, past_kr):
    T = q_nope.shape[1]
    P = past_kv.shape[1]
    q_lat = jnp.einsum('bthd,chd->bthc', q_nope, w_uk)
    s_past = jnp.einsum('bthc,bsc->bhts', q_lat, past_kv) + jnp.einsum('bthr,bsr->bhts', q_rope, past_kr)
    s_new = jnp.einsum('bthc,bsc->bhts', q_lat, c_kv) + jnp.einsum('bthr,bsr->bhts', q_rope, k_rope)
    s_new = jnp.where(jnp.tril(jnp.ones((T, T), dtype=bool)), s_new.astype(jnp.float32), -jnp.inf)
    s = jnp.concatenate([s_past.astype(jnp.float32), s_new], axis=-1) * C_SCALE
    p = jax.nn.softmax(s, axis=-1).astype(c_kv.dtype)
    o_lat = jnp.einsum('bhts,bsc->bthc', p[..., :P], past_kv) + jnp.einsum('bhts,bsc->bthc', p[..., P:], c_kv)
    return jnp.einsum('bthc,chd->bthd', o_lat, w_uv)


def _layer(x, p, pos, conv_prev, h0, attn_fn, chunk_len,
           ln_mix, w_in, sgu_ln_w, sgu_ln_b, w_s, b_s, conv_w, conv_b, dt_bias, a_log, d_skip, b_norm,
           q_norm, w_uq, kv_norm, w_ukv, w_pa, w_pb, w_pc, w_o, ln_ffn, w_up, w_down,
           ln_ple, w_ple_gate, w_ple):
    b, S, _ = x.shape
    h = _rmsnorm(x, ln_mix)
    proj = h @ w_in
    idx = [int(i) for i in np.cumsum(IN_SPLITS)[:-1]]
    gates, a_u, a_v, z, xbc, dt_raw, c_q, c_kv, k_r = jnp.split(proj, idx, axis=-1)

    a_u = jax.nn.gelu(a_u)
    a_v = _layernorm(jax.nn.gelu(a_v), sgu_ln_w, sgu_ln_b)
    y_a = _spatial_gate(a_u, a_v, w_s, b_s, chunk_len)

    xbc, conv_new = _causal_conv(xbc, conv_prev, conv_w, conv_b)
    xs, bm, cm = jnp.split(xbc, [B_INNER, B_INNER + B_GROUPS * B_STATE], axis=-1)
    dt = jax.nn.softplus(dt_raw.astype(jnp.float32) + dt_bias.astype(jnp.float32))
    a = -jnp.exp(a_log.astype(jnp.float32))
    xs4 = xs.reshape(b, S, B_HEADS, B_HEADDIM)
    ssd_len = SSD_CHUNK if S % SSD_CHUNK == 0 else S
    y_b, h_new = _ssd(xs4, dt, a, bm.reshape(b, S, B_GROUPS, B_STATE),
                      cm.reshape(b, S, B_GROUPS, B_STATE), h0, ssd_len)
    y_b = (y_b + d_skip[:, None].astype(y_b.dtype) * xs4).reshape(b, S, B_INNER)
    y_b = _rmsnorm(y_b * jax.nn.silu(z), b_norm)

    q = (_rmsnorm(c_q, q_norm) @ w_uq).reshape(b, S, C_HEADS, C_NOPE + C_ROPE)
    q_nope, q_rope = q[..., :C_NOPE], _rope(q[..., C_NOPE:], pos)
    c_kv = _rmsnorm(c_kv, kv_norm)
    k_r = _rope(k_r, pos)
    w_ukv3 = w_ukv.reshape(C_KV_LORA, C_HEADS, C_NOPE + C_V)
    w_uk, w_uv = w_ukv3[..., :C_NOPE], w_ukv3[..., C_NOPE:]
    y_c = attn_fn(q_nope, q_rope, c_kv, k_r, w_uk, w_uv).reshape(b, S, C_HEADS * C_V)

    g = jax.nn.sigmoid(gates.astype(jnp.float32)).astype(x.dtype).reshape(b, S, N_BRANCH, D_MODEL)
    merged = g[:, :, 0] * (y_a @ w_pa) + g[:, :, 1] * (y_b @ w_pb) + g[:, :, 2] * (y_c @ w_pc)
    x = x + merged @ w_o

    x = x + jnp.square(jax.nn.relu(_rmsnorm(x, ln_ffn) @ w_up)) @ w_down

    pg = jax.nn.sigmoid((_rmsnorm(x, ln_ple) @ w_ple_gate).astype(jnp.float32)).astype(x.dtype)
    x = x + pg * (p.astype(x.dtype) @ w_ple)
    return x, (c_kv, k_r, h_new, conv_new, a_v[:, -chunk_len:])


def setup_inputs(seed: int = 0) -> dict:
    key = jax.random.key(seed)
    ks = iter(jax.random.split(key, 48))
    f32 = jnp.float32

    def nrm(shape, scale):
        return jax.random.normal(next(ks), shape, f32) * scale

    def gain(shape):
        return 1.0 + 0.01 * jax.random.normal(next(ks), shape, f32)

    n_pages = PAST_LEN // PAGE_SIZE
    n_used = DEC_BATCH * n_pages
    n_pool = n_used + max(1, n_used // 4)
    page_table = jax.random.permutation(next(ks), n_pool)[:n_used].reshape(DEC_BATCH, n_pages).astype(jnp.int32)
    dt0 = jnp.exp(jax.random.uniform(next(ks), (DEPTH, B_HEADS), f32, math.log(1e-3), math.log(1e-1)))
    dt_bias = dt0 + jnp.log(-jnp.expm1(-dt0))
    a_log = jnp.log(jax.random.uniform(next(ks), (DEPTH, B_HEADS), f32, 1.0, 16.0))
    return {
        'x_prompt': nrm((BATCH, SEQ, D_MODEL), 1.0),
        'x_sample': nrm((DEC_BATCH, DEC_SEQ, D_MODEL), 1.0),
        'cache_kv_latent': nrm((DEPTH, n_pool, PAGE_SIZE, C_KV_LORA), 1.0),
        'cache_k_rope': nrm((DEPTH, n_pool, PAGE_SIZE, C_ROPE), 1.0),
        'state_ssm': nrm((DEPTH, DEC_BATCH, B_HEADS, B_HEADDIM, B_STATE), 0.1),
        'state_conv': nrm((DEPTH, DEC_BATCH, B_CONV - 1, B_CONV_DIM), 1.0),
        'page_table': page_table,
        'p_prompt': nrm((DEPTH, BATCH, SEQ, D_PLE), 1.0),
        'p_sample': nrm((DEPTH, DEC_BATCH, DEC_SEQ, D_PLE), 1.0),
        'ln_mix': gain((DEPTH, D_MODEL)),
        'w_in': nrm((DEPTH, D_MODEL, D_IN), D_MODEL ** -0.5),
        'sgu_ln_w': gain((DEPTH, A_WIDTH)),
        'sgu_ln_b': nrm((DEPTH, A_WIDTH), 0.01),
        'w_s': nrm((DEPTH, A_GROUPS, CHUNK, CHUNK), CHUNK ** -0.5),
        'b_s': gain((DEPTH, A_GROUPS, CHUNK)),
        'conv_w': nrm((DEPTH, B_CONV, B_CONV_DIM), B_CONV ** -0.5),
        'conv_b': nrm((DEPTH, B_CONV_DIM), 0.01),
        'dt_bias': dt_bias,
        'a_log': a_log,
        'd_skip': gain((DEPTH, B_HEADS)),
        'b_norm': gain((DEPTH, B_INNER)),
        'q_norm': gain((DEPTH, C_Q_LORA)),
        'w_uq': nrm((DEPTH, C_Q_LORA, C_HEADS * (C_NOPE + C_ROPE)), C_Q_LORA ** -0.5),
        'kv_norm': gain((DEPTH, C_KV_LORA)),
        'w_ukv': nrm((DEPTH, C_KV_LORA, C_HEADS * (C_NOPE + C_V)), C_KV_LORA ** -0.5),
        'w_pa': nrm((DEPTH, A_WIDTH, D_MODEL), A_WIDTH ** -0.5),
        'w_pb': nrm((DEPTH, B_INNER, D_MODEL), B_INNER ** -0.5),
        'w_pc': nrm((DEPTH, C_HEADS * C_V, D_MODEL), (C_HEADS * C_V) ** -0.5),
        'w_o': nrm((DEPTH, D_MODEL, D_MODEL), D_MODEL ** -0.5),
        'ln_ffn': gain((DEPTH, D_MODEL)),
        'w_up': nrm((DEPTH, D_MODEL, D_FF), D_MODEL ** -0.5),
        'w_down': nrm((DEPTH, D_FF, D_MODEL), D_FF ** -0.5),
        'ln_ple': gain((DEPTH, D_MODEL)),
        'w_ple_gate': nrm((DEPTH, D_MODEL, D_MODEL), D_MODEL ** -0.5),
        'w_ple': nrm((DEPTH, D_PLE, D_MODEL), D_PLE ** -0.5),
        'ln_final': gain((D_MODEL,)),
    }


def reference(x_prompt, x_sample, cache_kv_latent, cache_k_rope, state_ssm, state_conv, page_table,
              p_prompt, p_sample, ln_mix, w_in, sgu_ln_w, sgu_ln_b, w_s, b_s, conv_w, conv_b, dt_bias,
              a_log, d_skip, b_norm, q_norm, w_uq, kv_norm, w_ukv, w_pa, w_pb, w_pc, w_o, ln_ffn,
              w_up, w_down, ln_ple, w_ple_gate, w_ple, ln_final):
    n_dec, t_dec = x_sample.shape[0], x_sample.shape[1]
    past_len = page_table.shape[1] * PAGE_SIZE
    pos_p = jnp.arange(x_prompt.shape[1])
    pos_s = past_len + jnp.arange(t_dec)
    conv0 = jnp.zeros((x_prompt.shape[0], B_CONV - 1, B_CONV_DIM), x_prompt.dtype)
    ssm0 = jnp.zeros((x_prompt.shape[0], B_HEADS, B_HEADDIM, B_STATE), x_prompt.dtype)
    layer_w = (ln_mix, w_in, sgu_ln_w, sgu_ln_b, w_s, b_s, conv_w, conv_b, dt_bias, a_log, d_skip, b_norm,
               q_norm, w_uq, kv_norm, w_ukv, w_pa, w_pb, w_pc, w_o, ln_ffn, w_up, w_down,
               ln_ple, w_ple_gate, w_ple)
    xp, xs = x_prompt, x_sample
    outs_p, outs_s = [], []
    for i in range(DEPTH):
        lw = [w[i] for w in layer_w]
        xp, st_p = _layer(xp, p_prompt[i], pos_p, conv0, ssm0, _prompt_attention, CHUNK, *lw)
        past_kv = cache_kv_latent[i][page_table].reshape(n_dec, past_len, C_KV_LORA)
        past_kr = cache_k_rope[i][page_table].reshape(n_dec, past_len, C_ROPE)
        attn_s = functools.partial(_sample_attention, past_kv=past_kv, past_kr=past_kr)
        xs, st_s = _layer(xs, p_sample[i], pos_s, state_conv[i], state_ssm[i], attn_s, t_dec, *lw)
        outs_p.append(st_p)
        outs_s.append(st_s)
    y_prompt = _rmsnorm(xp, ln_final)
    y_sample = _rmsnorm(xs, ln_final)
    kv_p, kr_p, ssm_p, conv_p, v_p = [jnp.stack(t) for t in zip(*outs_p)]
    kv_s, kr_s, ssm_s, conv_s, v_s = [jnp.stack(t) for t in zip(*outs_s)]
    return (y_prompt, y_sample, kv_p, kr_p, ssm_p, conv_p, v_p, kv_s, kr_s, ssm_s, conv_s, v_s)
```

```python
import functools
import math

import numpy as np
import jax
import jax.numpy as jnp
from jax import lax
from jax.experimental import pallas as pl
from jax.experimental.pallas import tpu as pltpu

F32 = jnp.float32
BF16 = jnp.bfloat16

NORM_EPS = 1e-6
D_MODEL = 1024
N_BRANCH = 3
A_WIDTH = 512
A_GROUPS = 4
CHUNK = 128
B_INNER = 1024
B_HEADDIM = 64
B_HEADS = 16
B_GROUPS = 4
B_STATE = 128
B_CONV = 4
B_CONV_DIM = B_INNER + 2 * B_GROUPS * B_STATE
C_HEADS = 8
C_NOPE = 64
C_ROPE = 32
C_V = 64
C_KV_LORA = 256
C_Q_LORA = 384
ROPE_BASE = 10000.0
C_SCALE = (C_NOPE + C_ROPE) ** -0.5
D_FF = 4 * D_MODEL
PAGE_SIZE = 128
HEAD_PAD = 128

COL_CQ = 0
COL_DT = 384
COL_CKV = 512
COL_KR = 768
COL_KRS = 896
COL_GATE = 1024
COL_XBC = 4096
COL_Z = 6144
COL_AU = 7168
COL_AV = 7680
D_IN_PAD = 8192

VMEM_LIMIT = 56 * 1024 * 1024
HIGHEST = lax.Precision.HIGHEST


def _cparams(sem):
    return pltpu.CompilerParams(dimension_semantics=sem, vmem_limit_bytes=VMEM_LIMIT)


def _rms(x, g):
    ms = jnp.mean(x * x, axis=-1, keepdims=True)
    return x * lax.rsqrt(ms + NORM_EPS) * g


def _sigmoid(x):
    return 1.0 / (1.0 + jnp.exp(-x))


def _silu(x):
    return x * _sigmoid(x)


def _gelu(x):
    c = math.sqrt(2.0 / math.pi)
    return 0.5 * x * (1.0 + jnp.tanh(c * (x + 0.044715 * (x * x * x))))


def _softplus(x):
    return jnp.maximum(x, 0.0) + jnp.log(1.0 + jnp.exp(-jnp.abs(x)))


def _dot(a, b):
    return jnp.dot(a, b, preferred_element_type=F32)


def _dot_nt(a, b):
    return lax.dot_general(a, b, (((1,), (1,)), ((), ())), preferred_element_type=F32)


def _dot_tn(a, b):
    return lax.dot_general(a, b, (((0,), (0,)), ((), ())), preferred_element_type=F32)


def _in_proj_kernel(x_ref, g_ref, w_ref, o_ref, h_scr):
    @pl.when(pl.program_id(1) == 0)
    def _():
        h_scr[...] = _rms(x_ref[...], g_ref[...]).astype(BF16)

    o_ref[...] = _dot(h_scr[...], w_ref[...])


def _in_proj(x, g, w, tm, tn):
    n_tok = x.shape[0]
    n_out = w.shape[1]
    return pl.pallas_call(
        _in_proj_kernel,
        grid=(n_tok // tm, n_out // tn),
        in_specs=[
            pl.BlockSpec((tm, D_MODEL), lambda i, j: (i, 0)),
            pl.BlockSpec((1, D_MODEL), lambda i, j: (0, 0)),
            pl.BlockSpec((D_MODEL, tn), lambda i, j: (0, j)),
        ],
        out_specs=pl.BlockSpec((tm, tn), lambda i, j: (i, j)),
        out_shape=jax.ShapeDtypeStruct((n_tok, n_out), F32),
        scratch_shapes=[pltpu.VMEM((tm, D_MODEL), BF16)],
        compiler_params=_cparams(("parallel", "arbitrary")),
        name="in_proj",
    )(x, g, w)


def _gate_a_kernel(u_ref, v_ref, lnw_ref, lnb_ref, ws_ref, bs_ref, ya_ref, av_ref, *, n_chunks):
    row = lax.broadcasted_iota(jnp.int32, (CHUNK, CHUNK), 0)
    col = lax.broadcasted_iota(jnp.int32, (CHUNK, CHUNK), 1)
    causal = col <= row
    gd = A_WIDTH // A_GROUPS
    for c in range(n_chunks):
        rows = pl.ds(c * CHUNK, CHUNK)
        v = _gelu(v_ref[rows, :])
        mu = jnp.mean(v, axis=-1, keepdims=True)
        vc = v - mu
        var = jnp.mean(vc * vc, axis=-1, keepdims=True)
        av = vc * lax.rsqrt(var + NORM_EPS) * lnw_ref[...] + lnb_ref[...]
        av_ref[rows, :] = av
        u = _gelu(u_ref[rows, :])
        for g in range(A_GROUPS):
            w = jnp.where(causal, ws_ref[g], 0.0).astype(BF16)
            s = _dot(w, av[:, g * gd:(g + 1) * gd].astype(BF16)) + bs_ref[g]
            ya_ref[rows, g * gd:(g + 1) * gd] = (u[:, g * gd:(g + 1) * gd] * s).astype(BF16)


def _gate_a(proj, lnw, lnb, ws, bs, tm):
    n_tok = proj.shape[0]
    kern = functools.partial(_gate_a_kernel, n_chunks=tm // CHUNK)
    return pl.pallas_call(
        kern,
        grid=(n_tok // tm,),
        in_specs=[
            pl.BlockSpec((tm, A_WIDTH), lambda i: (i, COL_AU // A_WIDTH)),
            pl.BlockSpec((tm, A_WIDTH), lambda i: (i, COL_AV // A_WIDTH)),
            pl.BlockSpec((1, A_WIDTH), lambda i: (0, 0)),
            pl.BlockSpec((1, A_WIDTH), lambda i: (0, 0)),
            pl.BlockSpec((A_GROUPS, CHUNK, CHUNK), lambda i: (0, 0, 0)),
            pl.BlockSpec((A_GROUPS, CHUNK, 1), lambda i: (0, 0, 0)),
        ],
        out_specs=[
            pl.BlockSpec((tm, A_WIDTH), lambda i: (i, 0)),
            pl.BlockSpec((tm, A_WIDTH), lambda i: (i, 0)),
        ],
        out_shape=[
            jax.ShapeDtypeStruct((n_tok, A_WIDTH), BF16),
            jax.ShapeDtypeStruct((n_tok, A_WIDTH), F32),
        ],
        compiler_params=_cparams(("parallel",)),
        name="gate_a",
    )(proj, proj, lnw, lnb, ws, bs)


def _ssd_prompt_kernel(xbc_ref, z_ref, dt_ref, cw_ref, cb_ref, dtb_ref, alog_ref, dsk_ref, bn_ref,
                       yb_ref, st_ref, tail_scr, h_scr, y_scr):
    L = CHUNK
    i = pl.program_id(0)

    @pl.when(i == 0)
    def _():
        tail_scr[...] = jnp.zeros_like(tail_scr)
        h_scr[...] = jnp.zeros_like(h_scr)

    x = xbc_ref[...]
    tail = tail_scr[...]
    sub8 = lax.broadcasted_iota(jnp.int32, (8, B_CONV_DIM), 0)
    acc = x * cw_ref[B_CONV - 1:B_CONV, :]
    head = x[0:8, :] * cw_ref[B_CONV - 1:B_CONV, :]
    for sh in range(1, B_CONV):
        wk = cw_ref[B_CONV - 1 - sh:B_CONV - sh, :]
        xr = pltpu.roll(x, sh, 0)
        acc = acc + xr * wk
        first = jnp.where(sub8 < sh, pltpu.roll(tail, sh, 0), xr[0:8, :])
        head = head + first * wk
    tail_scr[...] = x[L - 8:L, :]
    y_scr[...] = _silu(acc + cb_ref[...])
    y_scr[0:8, :] = _silu(head + cb_ref[...])

    lane = lax.broadcasted_iota(jnp.int32, (1, 128), 1)
    a_row = jnp.where(lane < B_HEADS, -jnp.exp(alog_ref[...]), 0.0)
    dt = _softplus(dt_ref[...] + dtb_ref[...])
    da = dt * a_row
    row = lax.broadcasted_iota(jnp.int32, (L, L), 0)
    col = lax.broadcasted_iota(jnp.int32, (L, L), 1)
    causal = col <= row
    tri = causal.astype(F32)
    cs = jnp.dot(tri, da, preferred_element_type=F32, precision=HIGHEST)
    cs_t = lax.dot_general(da, (row <= col).astype(F32), (((0,), (0,)), ((), ())),
                           preferred_element_type=F32, precision=HIGHEST)
    cs_last = cs[L - 1:L, :]
    exp_cs = jnp.exp(cs)
    dend = jnp.exp(cs_last - cs)
    dec_chunk = jnp.exp(cs_last)

    rep = B_HEADS // B_GROUPS
    for g in range(B_GROUPS):
        bg = y_scr[:, B_INNER + g * B_STATE:B_INNER + (g + 1) * B_STATE]
        cg = y_scr[:, B_INNER + (B_GROUPS + g) * B_STATE:B_INNER + (B_GROUPS + g + 1) * B_STATE]
        scores = _dot_nt(cg.astype(BF16), bg.astype(BF16))
        for hh in range(rep):
            h = g * rep + hh
            colv = cs[:, h:h + 1]
            decay = jnp.exp(jnp.where(causal, colv - cs_t[h:h + 1, :], -jnp.inf))
            xs_h = y_scr[:, h * B_HEADDIM:(h + 1) * B_HEADDIM]
            xdt = (xs_h * dt[:, h:h + 1]).astype(BF16)
            yd = _dot((scores * decay).astype(BF16), xdt)
            hprev = h_scr[h]
            yo = _dot((cg * exp_cs[:, h:h + 1]).astype(BF16), hprev.astype(BF16))
            y_scr[:, h * B_HEADDIM:(h + 1) * B_HEADDIM] = yd + yo + dsk_ref[:, h * B_HEADDIM:(h + 1) * B_HEADDIM] * xs_h
            st = _dot_tn((bg * dend[:, h:h + 1]).astype(BF16), xdt)
            h_scr[h] = dec_chunk[:, h:h + 1] * hprev + st

    y = y_scr[:, 0:B_INNER] * _silu(z_ref[...])
    yb_ref[...] = _rms(y, bn_ref[...]).astype(BF16)
    st_ref[...] = h_scr[...]


def _ssd_prompt(proj, cw, cb, dtb128, alog128, dsk_e, bnorm):
    S = proj.shape[0]
    return pl.pallas_call(
        _ssd_prompt_kernel,
        grid=(S // CHUNK,),
        in_specs=[
            pl.BlockSpec((CHUNK, B_CONV_DIM), lambda i: (i, COL_XBC // B_CONV_DIM)),
            pl.BlockSpec((CHUNK, B_INNER), lambda i: (i, COL_Z // B_INNER)),
            pl.BlockSpec((CHUNK, 128), lambda i: (i, COL_DT // 128)),
            pl.BlockSpec((B_CONV, B_CONV_DIM), lambda i: (0, 0)),
            pl.BlockSpec((1, B_CONV_DIM), lambda i: (0, 0)),
            pl.BlockSpec((1, 128), lambda i: (0, 0)),
            pl.BlockSpec((1, 128), lambda i: (0, 0)),
            pl.BlockSpec((1, B_INNER), lambda i: (0, 0)),
            pl.BlockSpec((1, B_INNER), lambda i: (0, 0)),
        ],
        out_specs=[
            pl.BlockSpec((CHUNK, B_INNER), lambda i: (i, 0)),
            pl.BlockSpec((B_HEADS, B_STATE, B_HEADDIM), lambda i: (0, 0, 0)),
        ],
        out_shape=[
            jax.ShapeDtypeStruct((S, B_INNER), BF16),
            jax.ShapeDtypeStruct((B_HEADS, B_STATE, B_HEADDIM), F32),
        ],
        scratch_shapes=[
            pltpu.VMEM((8, B_CONV_DIM), F32),
            pltpu.VMEM((B_HEADS, B_STATE, B_HEADDIM), F32),
            pltpu.VMEM((CHUNK, B_CONV_DIM), F32),
        ],
        compiler_params=_cparams(("arbitrary",)),
        name="ssd_prompt",
    )(proj, proj, proj, cw, cb, dtb128, alog128, dsk_e, bnorm)


def _ssd_sample_kernel(xbc_ref, cprev_ref, z_ref, dt_ref, h0_ref, cw_ref, cb_ref, dtb_ref, alog_ref,
                       dsk_ref, bn_ref, e_ref, bm_ref, s_ref,
                       yb_ref, hn_ref, xp_scr, r16_scr, prod_scr, ex_scr, a_scr, b_scr, *, T):
    H, P, N, G = B_HEADS, B_HEADDIM, B_STATE, B_GROUPS
    rep = H // G
    pairs = [(t, s) for t in range(T) for s in range(t + 1)]
    xp_scr[0:B_CONV - 1, :] = cprev_ref[0]
    xp_scr[B_CONV - 1:B_CONV - 1 + T, :] = xbc_ref[0]
    acc = xp_scr[0:T, :] * cw_ref[0:1, :]
    for k in range(1, B_CONV):
        acc = acc + xp_scr[k:k + T, :] * cw_ref[k:k + 1, :]
    xc = _silu(acc + cb_ref[...])
    xs = xc[:, 0:B_INNER]
    bm = xc[:, B_INNER:B_INNER + G * N]
    cm = xc[:, B_INNER + G * N:]

    lane = lax.broadcasted_iota(jnp.int32, (1, 128), 1)
    a_row = jnp.where(lane < H, -jnp.exp(alog_ref[...]), 0.0)
    dt = _softplus(dt_ref[0] + dtb_ref[...])
    da = dt * a_row
    cs_rows = [da[0:1, :]]
    for t in range(1, T):
        cs_rows.append(cs_rows[-1] + da[t:t + 1, :])
    cs_last = cs_rows[-1]

    prod_scr[...] = jnp.zeros_like(prod_scr)
    for idx, (t, s) in enumerate(pairs):
        prod_scr[idx:idx + 1, :] = cm[t:t + 1, :] * bm[s:s + 1, :]
    gh = jnp.dot(prod_scr[...], s_ref[...], preferred_element_type=F32, precision=HIGHEST)
    ex_scr[...] = jnp.zeros_like(ex_scr)
    ex_scr[0:T, :] = dt
    for t in range(T):
        ex_scr[T + t:T + t + 1, :] = jnp.exp(cs_rows[t])
    for idx, (t, s) in enumerate(pairs):
        ex_scr[2 * T + idx:2 * T + idx + 1, :] = gh[idx:idx + 1, :] * jnp.exp(cs_rows[t] - cs_rows[s])
    ex = jnp.dot(ex_scr[...], e_ref[...], preferred_element_type=F32, precision=HIGHEST)
    xdt = xs * ex[0:T, :]

    for g in range(G):
        r16_scr[g * T:(g + 1) * T, :] = cm[:, g * N:(g + 1) * N]
    h2d = h0_ref[0].reshape(H * P, N)
    r = _dot_nt(r16_scr[...].astype(BF16), h2d.astype(BF16))
    gw = rep * P
    y_rows = []
    for t in range(T):
        yo = jnp.concatenate([r[g * T + t:g * T + t + 1, g * gw:(g + 1) * gw] for g in range(G)], axis=1)
        y_rows.append(yo * ex[T + t:T + t + 1, :])
    for idx, (t, s) in enumerate(pairs):
        y_rows[t] = y_rows[t] + ex[2 * T + idx:2 * T + idx + 1, :] * xdt[s:s + 1, :]
    for t in range(T):
        y = y_rows[t] + dsk_ref[...] * xs[t:t + 1, :]
        y = y * _silu(z_ref[0, t:t + 1, :])
        yb_ref[0, t:t + 1, :] = _rms(y, bn_ref[...]).astype(BF16)

    eye = (lax.broadcasted_iota(jnp.int32, (H, 128), 0) == lax.broadcasted_iota(jnp.int32, (H, 128), 1)).astype(F32)

    def to_col(v):
        return jnp.sum(jnp.broadcast_to(v, (H, 128)) * eye, axis=1, keepdims=True)

    a_scr[...] = jnp.zeros_like(a_scr)
    b_scr[...] = jnp.zeros_like(b_scr)
    for t in range(T):
        dcol = to_col(jnp.exp(cs_last - cs_rows[t]))
        a_scr[t * H:(t + 1) * H, :] = jnp.broadcast_to(xdt[t:t + 1, :], (H, H * P)) * bm_ref[...]
        for g in range(G):
            b_scr[t * H + g * rep:t * H + (g + 1) * rep, :] = (
                jnp.broadcast_to(bm[t:t + 1, g * N:(g + 1) * N], (rep, N)) * dcol[g * rep:(g + 1) * rep, :])
    st = _dot_tn(a_scr[...].astype(BF16), b_scr[...].astype(BF16))
    dfull = jnp.broadcast_to(to_col(jnp.exp(cs_last)), (H, N))
    for h in range(H):
        hn_ref[0, h] = dfull[h:h + 1, :] * h0_ref[0, h] + st[h * P:(h + 1) * P, :]


def _ssd_sample(xbc, cprev, z, dtb, h0, cw, cb, dtb128, alog128, dsk_e, bnorm, e_mat, blockmask, s_mat):
    B, T, _ = xbc.shape
    n_pair = -(-(T * (T + 1) // 2) // 8) * 8
    assert B_CONV - 1 + T <= 8 and T * B_HEADS <= 128
    kern = functools.partial(_ssd_sample_kernel, T=T)
    c2 = lambda b: (0, 0)
    return pl.pallas_call(
        kern,
        grid=(B,),
        in_specs=[
            pl.BlockSpec((1, T, B_CONV_DIM), lambda b: (b, 0, 0)),
            pl.BlockSpec((1, B_CONV - 1, B_CONV_DIM), lambda b: (b, 0, 0)),
            pl.BlockSpec((1, T, B_INNER), lambda b: (b, 0, 0)),
            pl.BlockSpec((1, T, 128), lambda b: (b, 0, 0)),
            pl.BlockSpec((1, B_HEADS, B_HEADDIM, B_STATE), lambda b: (b, 0, 0, 0)),
            pl.BlockSpec((B_CONV, B_CONV_DIM), c2),
            pl.BlockSpec((1, B_CONV_DIM), c2),
            pl.BlockSpec((1, 128), c2),
            pl.BlockSpec((1, 128), c2),
            pl.BlockSpec((1, B_INNER), c2),
            pl.BlockSpec((1, B_INNER), c2),
            pl.BlockSpec((128, B_INNER), c2),
            pl.BlockSpec((B_HEADS, B_INNER), c2),
            pl.BlockSpec((B_GROUPS * B_STATE, 128), c2),
        ],
        out_specs=[
            pl.BlockSpec((1, T, B_INNER), lambda b: (b, 0, 0)),
            pl.BlockSpec((1, B_HEADS, B_HEADDIM, B_STATE), lambda b: (b, 0, 0, 0)),
        ],
        out_shape=[
            jax.ShapeDtypeStruct((B, T, B_INNER), BF16),
            jax.ShapeDtypeStruct((B, B_HEADS, B_HEADDIM, B_STATE), F32),
        ],
        scratch_shapes=[
            pltpu.VMEM((8, B_CONV_DIM), F32),
            pltpu.VMEM((B_GROUPS * T, B_STATE), F32),
            pltpu.VMEM((n_pair, B_GROUPS * B_STATE), F32),
            pltpu.VMEM((2 * T + n_pair, 128), F32),
            pltpu.VMEM((128, B_INNER), F32),
            pltpu.VMEM((128, B_STATE), F32),
        ],
        compiler_params=_cparams(("parallel",)),
        name="ssd_sample",
    )(xbc, cprev, z, dtb, h0, cw, cb, dtb128, alog128, dsk_e, bnorm, e_mat, blockmask, s_mat)


def _c_prep_kernel(cq_ref, ckv_ref, kr_ref, krs_ref, cos_ref, sin_ref, qn_ref, kvn_ref,
                   wq1_ref, wq2_ref, wk_ref, wv_ref,
                   q_ref, k_ref, v_ref, ckvn_ref, krope_ref):
    cos = cos_ref[...]
    sin = sin_ref[...]
    cos8 = jnp.concatenate([cos] * C_HEADS, axis=1)
    sin8 = jnp.concatenate([sin] * C_HEADS, axis=1)
    cqn = _rms(cq_ref[...], qn_ref[...]).astype(BF16)
    q = _dot(cqn, wq1_ref[...]) * cos8 + _dot(cqn, wq2_ref[...]) * sin8
    q_ref[...] = (q * C_SCALE).astype(BF16)
    ckvn = _rms(ckv_ref[...], kvn_ref[...])
    ckvn_ref[...] = ckvn
    k128 = kr_ref[...] * cos + krs_ref[...] * sin
    krope_ref[...] = k128[:, C_NOPE:C_NOPE + C_ROPE]
    cb = ckvn.astype(BF16)
    k_ref[...] = (_dot(cb, wk_ref[...]) + jnp.concatenate([k128] * C_HEADS, axis=1)).astype(BF16)
    v_ref[...] = _dot(cb, wv_ref[...]).astype(BF16)


def _c_prep(proj, cos, sin, qn, kvn, wq1, wq2, wk, wv, tm):
    n_tok = proj.shape[0]
    c2 = lambda i: (0, 0)
    hq = C_HEADS * HEAD_PAD
    return pl.pallas_call(
        _c_prep_kernel,
        grid=(n_tok // tm,),
        in_specs=[
            pl.BlockSpec((tm, C_Q_LORA), lambda i: (i, COL_CQ // C_Q_LORA)),
            pl.BlockSpec((tm, C_KV_LORA), lambda i: (i, COL_CKV // C_KV_LORA)),
            pl.BlockSpec((tm, 128), lambda i: (i, COL_KR // 128)),
            pl.BlockSpec((tm, 128), lambda i: (i, COL_KRS // 128)),
            pl.BlockSpec((tm, 128), lambda i: (i, 0)),
            pl.BlockSpec((tm, 128), lambda i: (i, 0)),
            pl.BlockSpec((1, C_Q_LORA), c2),
            pl.BlockSpec((1, C_KV_LORA), c2),
            pl.BlockSpec((C_Q_LORA, hq), c2),
            pl.BlockSpec((C_Q_LORA, hq), c2),
            pl.BlockSpec((C_KV_LORA, hq), c2),
            pl.BlockSpec((C_KV_LORA, C_HEADS * C_V), c2),
        ],
        out_specs=[
            pl.BlockSpec((tm, hq), lambda i: (i, 0)),
            pl.BlockSpec((tm, hq), lambda i: (i, 0)),
            pl.BlockSpec((tm, C_HEADS * C_V), lambda i: (i, 0)),
            pl.BlockSpec((tm, C_KV_LORA), lambda i: (i, 0)),
            pl.BlockSpec((tm, C_ROPE), lambda i: (i, 0)),
        ],
        out_shape=[
            jax.ShapeDtypeStruct((n_tok, hq), BF16),
            jax.ShapeDtypeStruct((n_tok, hq), BF16),
            jax.ShapeDtypeStruct((n_tok, C_HEADS * C_V), BF16),
            jax.ShapeDtypeStruct((n_tok, C_KV_LORA), F32),
            jax.ShapeDtypeStruct((n_tok, C_ROPE), F32),
        ],
        compiler_params=_cparams(("parallel",)),
        name="c_prep",
    )(proj, proj, proj, proj, cos, sin, qn, kvn, wq1, wq2, wk, wv)


def _flash_kernel(qi_ref, kj_ref, q_ref, k_ref, v_ref, o_ref, m_scr, l_scr, acc_scr, *, tq, tk):
    s_idx = pl.program_id(1)
    qi = qi_ref[s_idx]
    kj = kj_ref[s_idx]

    @pl.when(kj == 0)
    def _():
        m_scr[...] = jnp.full_like(m_scr, -jnp.inf)
        l_scr[...] = jnp.zeros_like(l_scr)
        acc_scr[...] = jnp.zeros_like(acc_scr)

    rowp = qi * tq + lax.broadcasted_iota(jnp.int32, (tq, tk), 0)
    colp = kj * tk + lax.broadcasted_iota(jnp.int32, (tq, tk), 1)
    visible = colp <= rowp
    lane = lax.broadcasted_iota(jnp.int32, (1, 2 * C_V), 1)
    v = v_ref[...]
    pv = []
    alphas = []
    for hh in range(2):
        qh = q_ref[:, hh * HEAD_PAD:(hh + 1) * HEAD_PAD]
        kh = k_ref[:, hh * HEAD_PAD:(hh + 1) * HEAD_PAD]
        s = jnp.where(visible, _dot_nt(qh, kh), -jnp.inf)
        m_prev = m_scr[hh]
        m_new = jnp.maximum(m_prev, jnp.max(s, axis=-1, keepdims=True))
        alpha = jnp.exp(m_prev - m_new)
        p = jnp.exp(s - m_new)
        l_scr[hh] = alpha * l_scr[hh] + jnp.sum(p, axis=-1, keepdims=True)
        m_scr[hh] = m_new
        in_head = (lane >= hh * C_V) & (lane < (hh + 1) * C_V)
        vh = jnp.where(in_head, v, jnp.zeros_like(v))
        pv.append(_dot(p.astype(BF16), vh))
        alphas.append(alpha)
    alpha_full = jnp.where(lane < C_V, alphas[0], alphas[1])
    acc_scr[...] = acc_scr[...] * alpha_full + pv[0] + pv[1]

    @pl.when(kj == qi)
    def _():
        l_full = jnp.where(lane < C_V, l_scr[0], l_scr[1])
        o_ref[...] = (acc_scr[...] / l_full).astype(BF16)


def _flash(q, k, v, tq):
    S = q.shape[0]
    tk = tq
    nq = S // tq
    qi = np.concatenate([np.full(i + 1, i, np.int32) for i in range(nq)])
    kj = np.concatenate([np.arange(i + 1, dtype=np.int32) for i in range(nq)])
    kern = functools.partial(_flash_kernel, tq=tq, tk=tk)
    grid_spec = pltpu.PrefetchScalarGridSpec(
        num_scalar_prefetch=2,
        grid=(C_HEADS // 2, int(qi.shape[0])),
        in_specs=[
            pl.BlockSpec((tq, 2 * HEAD_PAD), lambda p, s, qi, kj: (qi[s], p)),
            pl.BlockSpec((tk, 2 * HEAD_PAD), lambda p, s, qi, kj: (kj[s], p)),
            pl.BlockSpec((tk, 2 * C_V), lambda p, s, qi, kj: (kj[s], p)),
        ],
        out_specs=pl.BlockSpec((tq, 2 * C_V), lambda p, s, qi, kj: (qi[s], p)),
        scratch_shapes=[
            pltpu.VMEM((2, tq, 1), F32),
            pltpu.VMEM((2, tq, 1), F32),
            pltpu.VMEM((tq, 2 * C_V), F32),
        ],
    )
    return pl.pallas_call(
        kern,
        grid_spec=grid_spec,
        out_shape=jax.ShapeDtypeStruct((S, C_HEADS * C_V), BF16),
        compiler_params=_cparams(("parallel", "arbitrary")),
        name="flash_prompt",
    )(jnp.asarray(qi), jnp.asarray(kj), q, k, v)


def _q_lat_kernel(q_ref, wt_ref, o_ref):
    o_ref[0] = _dot(q_ref[...], wt_ref[0]).astype(BF16)


def _q_lat(q, wukt):
    n = q.shape[0]
    return pl.pallas_call(
        _q_lat_kernel,
        grid=(C_HEADS,),
        in_specs=[
            pl.BlockSpec((n, HEAD_PAD), lambda h: (0, h)),
            pl.BlockSpec((1, HEAD_PAD, C_KV_LORA), lambda h: (h, 0, 0)),
        ],
        out_specs=pl.BlockSpec((1, n, C_KV_LORA), lambda h: (h, 0, 0)),
        out_shape=jax.ShapeDtypeStruct((C_HEADS, n, C_KV_LORA), BF16),
        compiler_params=_cparams(("parallel",)),
        name="q_lat",
    )(q, wukt)


def _attn_sample_kernel(pt_ref, qlat_ref, q128_ref, ckv_ref, krn_ref, kv_hbm, kr_hbm, o_ref,
                        kvbuf, krbuf, sem, newkv_scr, newkr_scr, *, T, n_pages, cp):
    b = pl.program_id(0)
    nb = pl.num_programs(0)
    n_chunks = n_pages // cp
    R = T * C_HEADS

    def copies(bb, c, slot):
        out = []
        for i in range(cp):
            page = pt_ref[bb, c * cp + i]
            out.append(pltpu.make_async_copy(kv_hbm.at[page], kvbuf.at[slot, i], sem.at[0, slot]))
            out.append(pltpu.make_async_copy(kr_hbm.at[page], krbuf.at[slot, i], sem.at[1, slot]))
        return out

    def start(bb, c, slot):
        for cpy in copies(bb, c, slot):
            cpy.start()

    def wait(bb, c, slot):
        for cpy in copies(bb, c, slot):
            cpy.wait()

    @pl.when(b == 0)
    def _():
        start(0, 0, 0)

    qlat = qlat_ref[0]
    qr = q128_ref[0][:, C_NOPE:C_NOPE + C_ROPE]

    def chunk(c, carry):
        m_prev, l_prev, acc = carry
        slot = (b * n_chunks + c) % 2
        nxt = 1 - slot

        @pl.when(c + 1 < n_chunks)
        def _():
            start(b, c + 1, nxt)

        @pl.when((c + 1 == n_chunks) & (b + 1 < nb))
        def _():
            start(b + 1, 0, nxt)

        wait(b, c, slot)
        kv = kvbuf[slot].reshape(cp * PAGE_SIZE, C_KV_LORA).astype(BF16)
        kr = krbuf[slot].reshape(cp * PAGE_SIZE, C_ROPE).astype(BF16)
        s = _dot_nt(qlat, kv) + _dot_nt(qr, kr)
        m_new = jnp.maximum(m_prev, jnp.max(s, axis=-1, keepdims=True))
        alpha = jnp.exp(m_prev - m_new)
        p = jnp.exp(s - m_new)
        l_new = alpha * l_prev + jnp.sum(p, axis=-1, keepdims=True)
        acc = acc * alpha + _dot(p.astype(BF16), kv)
        return m_new, l_new, acc

    init = (jnp.full((R, 1), -jnp.inf, F32), jnp.zeros((R, 1), F32), jnp.zeros((R, C_KV_LORA), F32))
    m_prev, l_prev, acc = lax.fori_loop(0, n_chunks, chunk, init)

    newkv_scr[...] = jnp.zeros_like(newkv_scr)
    newkr_scr[...] = jnp.zeros_like(newkr_scr)
    newkv_scr[0:T, :] = ckv_ref[0]
    newkr_scr[0:T, :] = krn_ref[0]
    kvn = newkv_scr[...].astype(BF16)
    krn = newkr_scr[...].astype(BF16)
    s = _dot_nt(qlat, kvn) + _dot_nt(qr, krn)
    t_row = lax.broadcasted_iota(jnp.int32, (R, 128), 0) % T
    key = lax.broadcasted_iota(jnp.int32, (R, 128), 1)
    s = jnp.where(key <= t_row, s, -jnp.inf)
    m_new = jnp.maximum(m_prev, jnp.max(s, axis=-1, keepdims=True))
    alpha = jnp.exp(m_prev - m_new)
    p = jnp.exp(s - m_new)
    l_new = alpha * l_prev + jnp.sum(p, axis=-1, keepdims=True)
    acc = acc * alpha + _dot(p.astype(BF16), kvn)
    o_ref[0] = (acc / l_new).astype(BF16)


def _attn_sample(page_table, qlat, q128, ckv_new, kr_new, cache_kv, cache_kr, cp):
    B, R, _ = qlat.shape
    T = ckv_new.shape[1]
    n_pages = page_table.shape[1]
    kern = functools.partial(_attn_sample_kernel, T=T, n_pages=n_pages, cp=cp)
    grid_spec = pltpu.PrefetchScalarGridSpec(
        num_scalar_prefetch=1,
        grid=(B,),
        in_specs=[
            pl.BlockSpec((1, R, C_KV_LORA), lambda b, pt: (b, 0, 0)),
            pl.BlockSpec((1, R, HEAD_PAD), lambda b, pt: (b, 0, 0)),
            pl.BlockSpec((1, T, C_KV_LORA), lambda b, pt: (b, 0, 0)),
            pl.BlockSpec((1, T, C_ROPE), lambda b, pt: (b, 0, 0)),
            pl.BlockSpec(memory_space=pl.ANY),
            pl.BlockSpec(memory_space=pl.ANY),
        ],
        out_specs=pl.BlockSpec((1, R, C_KV_LORA), lambda b, pt: (b, 0, 0)),
        scratch_shapes=[
            pltpu.VMEM((2, cp, PAGE_SIZE, C_KV_LORA), F32),
            pltpu.VMEM((2, cp, PAGE_SIZE, C_ROPE), F32),
            pltpu.SemaphoreType.DMA((2, 2)),
            pltpu.VMEM((128, C_KV_LORA), F32),
            pltpu.VMEM((128, C_ROPE), F32),
        ],
    )
    return pl.pallas_call(
        kern,
        grid_spec=grid_spec,
        out_shape=jax.ShapeDtypeStruct((B, R, C_KV_LORA), BF16),
        compiler_params=_cparams(("arbitrary",)),
        name="attn_sample",
    )(page_table, qlat, q128, ckv_new, kr_new, cache_kv, cache_kr)


def _uv_proj_kernel(o_ref, w_ref, y_ref):
    y_ref[0] = _dot(o_ref[0], w_ref[0]).astype(BF16)


def _uv_proj(olat, wuv):
    H, n, _ = olat.shape
    return pl.pallas_call(
        _uv_proj_kernel,
        grid=(H,),
        in_specs=[
            pl.BlockSpec((1, n, C_KV_LORA), lambda h: (h, 0, 0)),
            pl.BlockSpec((1, C_KV_LORA, C_V), lambda h: (h, 0, 0)),
        ],
        out_specs=pl.BlockSpec((1, n, C_V), lambda h: (h, 0, 0)),
        out_shape=jax.ShapeDtypeStruct((H, n, C_V), BF16),
        compiler_params=_cparams(("parallel",)),
        name="uv_proj",
    )(olat, wuv)


def _merge_kernel(x_ref, g0_ref, g1_ref, g2_ref, ya_ref, yb_ref, yc_ref, wpa_ref, wpb_ref, wpc_ref, wo_ref, o_ref):
    m = _sigmoid(g0_ref[...]) * _dot(ya_ref[...], wpa_ref[...])
    m = m + _sigmoid(g1_ref[...]) * _dot(yb_ref[...], wpb_ref[...])
    m = m + _sigmoid(g2_ref[...]) * _dot(yc_ref[...], wpc_ref[...])
    o_ref[...] = x_ref[...] + _dot(m.astype(BF16), wo_ref[...])


def _merge(x, proj, ya, yb, yc, wpa, wpb, wpc, wo, tm):
    n_tok = x.shape[0]
    c2 = lambda i: (0, 0)
    g0 = COL_GATE // D_MODEL
    return pl.pallas_call(
        _merge_kernel,
        grid=(n_tok // tm,),
        in_specs=[
            pl.BlockSpec((tm, D_MODEL), lambda i: (i, 0)),
            pl.BlockSpec((tm, D_MODEL), lambda i: (i, g0)),
            pl.BlockSpec((tm, D_MODEL), lambda i: (i, g0 + 1)),
            pl.BlockSpec((tm, D_MODEL), lambda i: (i, g0 + 2)),
            pl.BlockSpec((tm, A_WIDTH), lambda i: (i, 0)),
            pl.BlockSpec((tm, B_INNER), lambda i: (i, 0)),
            pl.BlockSpec((tm, C_HEADS * C_V), lambda i: (i, 0)),
            pl.BlockSpec((A_WIDTH, D_MODEL), c2),
            pl.BlockSpec((B_INNER, D_MODEL), c2),
            pl.BlockSpec((C_HEADS * C_V, D_MODEL), c2),
            pl.BlockSpec((D_MODEL, D_MODEL), c2),
        ],
        out_specs=pl.BlockSpec((tm, D_MODEL), lambda i: (i, 0)),
        out_shape=jax.ShapeDtypeStruct((n_tok, D_MODEL), F32),
        compiler_params=_cparams(("parallel",)),
        name="merge",
    )(x, proj, proj, proj, ya, yb, yc, wpa, wpb, wpc, wo)


def _ffn_kernel(x_ref, g_ref, wu_ref, wd_ref, o_ref, h_scr, acc_scr):
    j = pl.program_id(1)

    @pl.when(j == 0)
    def _():
        x = x_ref[...]
        h_scr[...] = _rms(x, g_ref[...]).astype(BF16)
        acc_scr[...] = x

    u = jnp.maximum(_dot(h_scr[...], wu_ref[...]), 0.0)
    acc_scr[...] += _dot((u * u).astype(BF16), wd_ref[...])

    @pl.when(j == pl.num_programs(1) - 1)
    def _():
        o_ref[...] = acc_scr[...]


def _ffn(x, g, wu, wd, tm, tf):
    n_tok = x.shape[0]
    return pl.pallas_call(
        _ffn_kernel,
        grid=(n_tok // tm, D_FF // tf),
        in_specs=[
            pl.BlockSpec((tm, D_MODEL), lambda i, j: (i, 0)),
            pl.BlockSpec((1, D_MODEL), lambda i, j: (0, 0)),
            pl.BlockSpec((D_MODEL, tf), lambda i, j: (0, j)),
            pl.BlockSpec((tf, D_MODEL), lambda i, j: (j, 0)),
        ],
        out_specs=pl.BlockSpec((tm, D_MODEL), lambda i, j: (i, 0)),
        out_shape=jax.ShapeDtypeStruct((n_tok, D_MODEL), F32),
        scratch_shapes=[pltpu.VMEM((tm, D_MODEL), BF16), pltpu.VMEM((tm, D_MODEL), F32)],
        compiler_params=_cparams(("parallel", "arbitrary")),
        name="ffn",
    )(x, g, wu, wd)


def _ple_kernel(x_ref, g_ref, wg_ref, p_ref, wp_ref, gf_ref, o_ref, *, final):
    x = x_ref[...]
    pg = _sigmoid(_dot(_rms(x, g_ref[...]).astype(BF16), wg_ref[...]))
    y = x + pg * _dot(p_ref[...].astype(BF16), wp_ref[...])
    if final:
        y = _rms(y, gf_ref[...])
    o_ref[...] = y


def _ple(x, g, wg, p, wp, gf, tm, final):
    n_tok = x.shape[0]
    d_ple = p.shape[1]
    c2 = lambda i: (0, 0)
    return pl.pallas_call(
        functools.partial(_ple_kernel, final=final),
        grid=(n_tok // tm,),
        in_specs=[
            pl.BlockSpec((tm, D_MODEL), lambda i: (i, 0)),
            pl.BlockSpec((1, D_MODEL), c2),
            pl.BlockSpec((D_MODEL, D_MODEL), c2),
            pl.BlockSpec((tm, d_ple), lambda i: (i, 0)),
            pl.BlockSpec((d_ple, D_MODEL), c2),
            pl.BlockSpec((1, D_MODEL), c2),
        ],
        out_specs=pl.BlockSpec((tm, D_MODEL), lambda i: (i, 0)),
        out_shape=jax.ShapeDtypeStruct((n_tok, D_MODEL), F32),
        compiler_params=_cparams(("parallel",)),
        name="ple",
    )(x, g, wg, p, wp, gf)


def _prep_layer_weights(w_in, w_uq, w_ukv):
    sizes = (N_BRANCH * D_MODEL, A_WIDTH, A_WIDTH, B_INNER, B_CONV_DIM, B_HEADS, C_Q_LORA, C_KV_LORA, C_ROPE)
    idx = [int(v) for v in np.cumsum(sizes)[:-1]]
    gates, a_u, a_v, z, xbc, dt, c_q, c_kv, k_r = jnp.split(w_in, idx, axis=1)
    half = C_ROPE // 2
    zc = lambda n: jnp.zeros((D_MODEL, n), w_in.dtype)
    k_rs = jnp.concatenate([k_r[:, half:], k_r[:, :half]], axis=1)
    w_in_r = jnp.concatenate([
        c_q, dt, zc(128 - B_HEADS), c_kv,
        zc(C_NOPE), k_r, zc(HEAD_PAD - C_NOPE - C_ROPE),
        zc(C_NOPE), k_rs, zc(HEAD_PAD - C_NOPE - C_ROPE),
        gates, xbc, z, a_u, a_v], axis=1).astype(BF16)

    uq = w_uq.reshape(C_Q_LORA, C_HEADS, C_NOPE + C_ROPE)
    uq_n, uq_r = uq[..., :C_NOPE], uq[..., C_NOPE:]
    uq_rs = jnp.concatenate([uq_r[..., half:], uq_r[..., :half]], axis=-1)
    zq = lambda n: jnp.zeros((C_Q_LORA, C_HEADS, n), w_uq.dtype)
    wq1 = jnp.concatenate([uq_n, uq_r, zq(HEAD_PAD - C_NOPE - C_ROPE)], axis=-1).reshape(C_Q_LORA, -1).astype(BF16)
    wq2 = jnp.concatenate([zq(C_NOPE), uq_rs, zq(HEAD_PAD - C_NOPE - C_ROPE)], axis=-1).reshape(C_Q_LORA, -1).astype(BF16)

    ukv = w_ukv.reshape(C_KV_LORA, C_HEADS, C_NOPE + C_V)
    uk, uv = ukv[..., :C_NOPE], ukv[..., C_NOPE:]
    wk = jnp.concatenate([uk, jnp.zeros((C_KV_LORA, C_HEADS, HEAD_PAD - C_NOPE), w_ukv.dtype)], axis=-1)
    wk = wk.reshape(C_KV_LORA, -1).astype(BF16)
    wv = uv.reshape(C_KV_LORA, -1).astype(BF16)
    wukt = jnp.transpose(wk.reshape(C_KV_LORA, C_HEADS, HEAD_PAD), (1, 2, 0))
    wuv_h = jnp.transpose(uv, (1, 0, 2)).astype(BF16)
    return w_in_r, wq1, wq2, wk, wv, wukt, wuv_h


def _rope_tables(pos):
    half = C_ROPE // 2
    inv = jnp.power(ROPE_BASE, -jnp.arange(half, dtype=F32) * (2.0 / C_ROPE))
    ang = pos.astype(F32)[:, None] * inv[None, :]
    cos, sin = jnp.cos(ang), jnp.sin(ang)
    n = pos.shape[0]
    pad = jnp.zeros((n, HEAD_PAD - C_NOPE - C_ROPE), F32)
    cos_t = jnp.concatenate([jnp.ones((n, C_NOPE), F32), cos, cos, pad], axis=1)
    sin_t = jnp.concatenate([jnp.zeros((n, C_NOPE), F32), -sin, sin, pad], axis=1)
    return cos_t, sin_t


def _pad128(v):
    return jnp.concatenate([v, jnp.zeros((128 - v.shape[0],), v.dtype)])[None, :]


def _token_tile(n, pref):
    t = pref
    while n % t:
        t //= 2
    return t


def kernel(x_prompt, x_sample, cache_kv_latent, cache_k_rope, state_ssm, state_conv, page_table, p_prompt, p_sample, ln_mix, w_in, sgu_ln_w, sgu_ln_b, w_s, b_s, conv_w, conv_b, dt_bias, a_log, d_skip, b_norm, q_norm, w_uq, kv_norm, w_ukv, w_pa, w_pb, w_pc, w_o, ln_ffn, w_up, w_down, ln_ple, w_ple_gate, w_ple, ln_final):
    depth = w_in.shape[0]
    _, S, _ = x_prompt.shape
    B, T, _ = x_sample.shape
    n_pages = page_table.shape[1]
    past_len = n_pages * PAGE_SIZE
    ns = B * T
    assert x_prompt.shape[0] == 1 and S % CHUNK == 0 and ns % CHUNK == 0 and CHUNK % T == 0

    xp = x_prompt.reshape(S, D_MODEL)
    xs = x_sample.reshape(ns, D_MODEL)
    cos_p, sin_p = _rope_tables(jnp.arange(S))
    cos_s, sin_s = _rope_tables(past_len + (jnp.arange(ns) % T))

    hp = np.arange(B_INNER) // B_HEADDIM
    e_mat = jnp.asarray((np.arange(128)[:, None] == hp[None, :]).astype(np.float32))
    blockmask = jnp.asarray((np.arange(B_HEADS)[:, None] == hp[None, :]).astype(np.float32))
    gn = np.arange(B_GROUPS * B_STATE) // B_STATE
    hg = np.where(np.arange(128) < B_HEADS, np.arange(128) // (B_HEADS // B_GROUPS), -1)
    s_mat = jnp.asarray((gn[:, None] == hg[None, :]).astype(np.float32))

    tm_p = _token_tile(S, 1024)
    tm_s = _token_tile(ns, 512)
    tq = _token_tile(S, 512)
    cp = _token_tile(n_pages, 16)
    tile_rep = CHUNK // T
    eye_rep = jnp.eye(tile_rep, dtype=F32)

    outs_p, outs_s = [], []
    for i in range(depth):
        w_in_r, wq1, wq2, wk, wv, wukt, wuv_h = _prep_layer_weights(w_in[i], w_uq[i], w_ukv[i])
        g_mix = ln_mix[i][None, :]
        lnw, lnb = sgu_ln_w[i][None, :], sgu_ln_b[i][None, :]
        ws_p = w_s[i][:, :CHUNK, :CHUNK]
        bs_p = b_s[i][:, :CHUNK, None]
        ws_t = jnp.tril(w_s[i][:, :T, :T])
        ws_s = jnp.einsum('ab,gts->gatbs', eye_rep, ws_t).reshape(A_GROUPS, CHUNK, CHUNK)
        bs_s = jnp.tile(b_s[i][:, :T], (1, tile_rep))[:, :, None]
        cw, cb = conv_w[i], conv_b[i][None, :]
        dtb128, alog128 = _pad128(dt_bias[i]), _pad128(a_log[i])
        dsk_e = jnp.repeat(d_skip[i], B_HEADDIM)[None, :]
        bn = b_norm[i][None, :]
        qn, kvn = q_norm[i][None, :], kv_norm[i][None, :]
        wpa, wpb, wpc, wo = (w.astype(BF16) for w in (w_pa[i], w_pb[i], w_pc[i], w_o[i]))
        wu, wd = w_up[i].astype(BF16), w_down[i].astype(BF16)
        wg, wp = w_ple_gate[i].astype(BF16), w_ple[i].astype(BF16)
        g_ffn, g_ple, g_fin = ln_ffn[i][None, :], ln_ple[i][None, :], ln_final[None, :]
        final = i == depth - 1

        proj = _in_proj(xp, g_mix, w_in_r, tm_p, 1024)
        ya, av = _gate_a(proj, lnw, lnb, ws_p, bs_p, _token_tile(S, 512))
        yb, ssm_t = _ssd_prompt(proj, cw, cb, dtb128, alog128, dsk_e, bn)
        q, k, v, ckvn, krope = _c_prep(proj, cos_p, sin_p, qn, kvn, wq1, wq2, wk, wv, _token_tile(S, 512))
        yc = _flash(q, k, v, tq)
        xp = _merge(xp, proj, ya, yb, yc, wpa, wpb, wpc, wo, _token_tile(S, 512))
        xp = _ffn(xp, g_ffn, wu, wd, tm_p, 1024)
        xp = _ple(xp, g_ple, wg, p_prompt[i].reshape(S, -1), wp, g_fin, _token_tile(S, 512), final)
        outs_p.append((ckvn.reshape(1, S, C_KV_LORA), krope.reshape(1, S, C_ROPE),
                       jnp.swapaxes(ssm_t, 1, 2)[None],
                       proj[S - (B_CONV - 1):, COL_XBC:COL_XBC + B_CONV_DIM][None],
                       av[S - CHUNK:][None]))

        proj = _in_proj(xs, g_mix, w_in_r, tm_s, 1024)
        ya, av = _gate_a(proj, lnw, lnb, ws_s, bs_s, tm_s)
        xbc_s = proj[:, COL_XBC:COL_XBC + B_CONV_DIM].reshape(B, T, B_CONV_DIM)
        z_s = proj[:, COL_Z:COL_Z + B_INNER].reshape(B, T, B_INNER)
        dt_s = proj[:, COL_DT:COL_DT + 128].reshape(B, T, 128)
        yb, ssm_new = _ssd_sample(xbc_s, state_conv[i], z_s, dt_s, state_ssm[i], cw, cb, dtb128, alog128,
                                  dsk_e, bn, e_mat, blockmask, s_mat)
        q, _, _, ckvn, krope = _c_prep(proj, cos_s, sin_s, qn, kvn, wq1, wq2, wk, wv, tm_s)
        qlat = _q_lat(q, wukt)
        qlat = jnp.transpose(qlat.reshape(C_HEADS, B, T, C_KV_LORA), (1, 0, 2, 3)).reshape(B, C_HEADS * T, C_KV_LORA)
        q128 = jnp.transpose(q.reshape(B, T, C_HEADS, HEAD_PAD), (0, 2, 1, 3)).reshape(B, C_HEADS * T, HEAD_PAD)
        olat = _attn_sample(page_table, qlat, q128, ckvn.reshape(B, T, C_KV_LORA), krope.reshape(B, T, C_ROPE),
                            cache_kv_latent[i], cache_k_rope[i], cp)
        olat = jnp.transpose(olat.reshape(B, C_HEADS, T, C_KV_LORA), (1, 0, 2, 3)).reshape(C_HEADS, ns, C_KV_LORA)
        yc = jnp.transpose(_uv_proj(olat, wuv_h), (1, 0, 2)).reshape(ns, C_HEADS * C_V)
        xs = _merge(xs, proj, ya, yb.reshape(ns, B_INNER), yc, wpa, wpb, wpc, wo, tm_s)
        xs = _ffn(xs, g_ffn, wu, wd, tm_s, 1024)
        xs = _ple(xs, g_ple, wg, p_sample[i].reshape(ns, -1), wp, g_fin, tm_s, final)
        outs_s.append((ckvn.reshape(B, T, C_KV_LORA), krope.reshape(B, T, C_ROPE), ssm_new,
                       xbc_s[:, T - (B_CONV - 1):], av.reshape(B, T, A_WIDTH)))

    kv_p, kr_p, ssm_p, conv_p, v_p = [jnp.stack(t) for t in zip(*outs_p)]
    kv_s, kr_s, ssm_s, conv_s, v_s = [jnp.stack(t) for t in zip(*outs_s)]
    return (xp.reshape(1, S, D_MODEL), xs.reshape(B, T, D_MODEL), kv_p, kr_p, ssm_p, conv_p, v_p,
            kv_s, kr_s, ssm_s, conv_s, v_s)
```

```python
import functools
import math

import numpy as np
import jax
import jax.numpy as jnp
from jax import lax
from jax.experimental import pallas as pl
from jax.experimental.pallas import tpu as pltpu

F32 = jnp.float32
BF16 = jnp.bfloat16

NORM_EPS = 1e-6
D_MODEL = 1024
N_BRANCH = 3
A_WIDTH = 512
A_GROUPS = 4
CHUNK = 128
B_INNER = 1024
B_HEADDIM = 64
B_HEADS = 16
B_GROUPS = 4
B_STATE = 128
B_CONV = 4
B_CONV_DIM = B_INNER + 2 * B_GROUPS * B_STATE
C_HEADS = 8
C_NOPE = 64
C_ROPE = 32
C_V = 64
C_KV_LORA = 256
C_Q_LORA = 384
ROPE_BASE = 10000.0
C_SCALE = (C_NOPE + C_ROPE) ** -0.5
LOG2E = math.log2(math.e)
FLASH_HEADS_PER_STEP = 8
ATTN_PAGES_PER_CHUNK = 32
ATTN_STREAMS = 4
D_FF = 4 * D_MODEL
PAGE_SIZE = 128
HEAD_PAD = 128

COL_CQ = 0
COL_DT = 384
COL_CKV = 512
COL_KR = 768
COL_KRS = 896
COL_GATE = 1024
COL_XBC = 4096
COL_Z = 6144
COL_AU = 7168
COL_AV = 7680
D_IN_PAD = 8192

VMEM_LIMIT = 56 * 1024 * 1024
HIGHEST = lax.Precision.HIGHEST


def _cparams(sem):
    return pltpu.CompilerParams(dimension_semantics=sem, vmem_limit_bytes=VMEM_LIMIT)


def _rms(x, g):
    ms = jnp.mean(x * x, axis=-1, keepdims=True)
    return x * lax.rsqrt(ms + NORM_EPS) * g


def _sigmoid(x):
    return 1.0 / (1.0 + jnp.exp(-x))


def _silu(x):
    return x * _sigmoid(x)


def _gelu(x):
    c = math.sqrt(2.0 / math.pi)
    return 0.5 * x * (1.0 + jnp.tanh(c * (x + 0.044715 * (x * x * x))))


def _softplus(x):
    return jnp.maximum(x, 0.0) + jnp.log(1.0 + jnp.exp(-jnp.abs(x)))


def _dot(a, b):
    return jnp.dot(a, b, preferred_element_type=F32)


def _dot_nt(a, b):
    return lax.dot_general(a, b, (((1,), (1,)), ((), ())), preferred_element_type=F32)


def _dot_tn(a, b):
    return lax.dot_general(a, b, (((0,), (0,)), ((), ())), preferred_element_type=F32)


def _in_proj_kernel(x_ref, g_ref, w_ref, o_ref, h_scr):
    @pl.when(pl.program_id(1) == 0)
    def _():
        h_scr[...] = _rms(x_ref[...], g_ref[...]).astype(BF16)

    o_ref[...] = _dot(h_scr[...], w_ref[...])


def _in_proj(x, g, w, tm, tn):
    n_tok = x.shape[0]
    n_out = w.shape[1]
    return pl.pallas_call(
        _in_proj_kernel,
        grid=(n_tok // tm, n_out // tn),
        in_specs=[
            pl.BlockSpec((tm, D_MODEL), lambda i, j: (i, 0)),
            pl.BlockSpec((1, D_MODEL), lambda i, j: (0, 0)),
            pl.BlockSpec((D_MODEL, tn), lambda i, j: (0, j)),
        ],
        out_specs=pl.BlockSpec((tm, tn), lambda i, j: (i, j)),
        out_shape=jax.ShapeDtypeStruct((n_tok, n_out), F32),
        scratch_shapes=[pltpu.VMEM((tm, D_MODEL), BF16)],
        compiler_params=_cparams(("parallel", "arbitrary")),
        name="in_proj",
    )(x, g, w)


def _gate_a_kernel(u_ref, v_ref, lnw_ref, lnb_ref, ws_ref, bs_ref, ya_ref, av_ref, *, n_chunks):
    row = lax.broadcasted_iota(jnp.int32, (CHUNK, CHUNK), 0)
    col = lax.broadcasted_iota(jnp.int32, (CHUNK, CHUNK), 1)
    causal = col <= row
    gd = A_WIDTH // A_GROUPS
    for c in range(n_chunks):
        rows = pl.ds(c * CHUNK, CHUNK)
        v = _gelu(v_ref[rows, :])
        mu = jnp.mean(v, axis=-1, keepdims=True)
        vc = v - mu
        var = jnp.mean(vc * vc, axis=-1, keepdims=True)
        av = vc * lax.rsqrt(var + NORM_EPS) * lnw_ref[...] + lnb_ref[...]
        av_ref[rows, :] = av
        u = _gelu(u_ref[rows, :])
        for g in range(A_GROUPS):
            w = jnp.where(causal, ws_ref[g], 0.0).astype(BF16)
            s = _dot(w, av[:, g * gd:(g + 1) * gd].astype(BF16)) + bs_ref[g]
            ya_ref[rows, g * gd:(g + 1) * gd] = (u[:, g * gd:(g + 1) * gd] * s).astype(BF16)


def _gate_a(proj, lnw, lnb, ws, bs, tm):
    n_tok = proj.shape[0]
    kern = functools.partial(_gate_a_kernel, n_chunks=tm // CHUNK)
    return pl.pallas_call(
        kern,
        grid=(n_tok // tm,),
        in_specs=[
            pl.BlockSpec((tm, A_WIDTH), lambda i: (i, COL_AU // A_WIDTH)),
            pl.BlockSpec((tm, A_WIDTH), lambda i: (i, COL_AV // A_WIDTH)),
            pl.BlockSpec((1, A_WIDTH), lambda i: (0, 0)),
            pl.BlockSpec((1, A_WIDTH), lambda i: (0, 0)),
            pl.BlockSpec((A_GROUPS, CHUNK, CHUNK), lambda i: (0, 0, 0)),
            pl.BlockSpec((A_GROUPS, CHUNK, 1), lambda i: (0, 0, 0)),
        ],
        out_specs=[
            pl.BlockSpec((tm, A_WIDTH), lambda i: (i, 0)),
            pl.BlockSpec((tm, A_WIDTH), lambda i: (i, 0)),
        ],
        out_shape=[
            jax.ShapeDtypeStruct((n_tok, A_WIDTH), BF16),
            jax.ShapeDtypeStruct((n_tok, A_WIDTH), F32),
        ],
        compiler_params=_cparams(("parallel",)),
        name="gate_a",
    )(proj, proj, lnw, lnb, ws, bs)


def _ssd_prompt_kernel(xbc_ref, z_ref, dt_ref, cw_ref, cb_ref, dtb_ref, alog_ref, dsk_ref, bn_ref,
                       yb_ref, st_ref, tail_scr, h_scr, y_scr):
    L = CHUNK
    i = pl.program_id(0)

    @pl.when(i == 0)
    def _():
        tail_scr[...] = jnp.zeros_like(tail_scr)
        h_scr[...] = jnp.zeros_like(h_scr)

    x = xbc_ref[...]
    tail = tail_scr[...]
    sub8 = lax.broadcasted_iota(jnp.int32, (8, B_CONV_DIM), 0)
    acc = x * cw_ref[B_CONV - 1:B_CONV, :]
    head = x[0:8, :] * cw_ref[B_CONV - 1:B_CONV, :]
    for sh in range(1, B_CONV):
        wk = cw_ref[B_CONV - 1 - sh:B_CONV - sh, :]
        xr = pltpu.roll(x, sh, 0)
        acc = acc + xr * wk
        first = jnp.where(sub8 < sh, pltpu.roll(tail, sh, 0), xr[0:8, :])
        head = head + first * wk
    tail_scr[...] = x[L - 8:L, :]
    y_scr[...] = _silu(acc + cb_ref[...])
    y_scr[0:8, :] = _silu(head + cb_ref[...])

    lane = lax.broadcasted_iota(jnp.int32, (1, 128), 1)
    a_row = jnp.where(lane < B_HEADS, -jnp.exp(alog_ref[...]), 0.0)
    dt = _softplus(dt_ref[...] + dtb_ref[...])
    da = dt * a_row
    row = lax.broadcasted_iota(jnp.int32, (L, L), 0)
    col = lax.broadcasted_iota(jnp.int32, (L, L), 1)
    causal = col <= row
    tri = causal.astype(F32)
    cs = jnp.dot(tri, da, preferred_element_type=F32, precision=HIGHEST)
    cs_t = lax.dot_general(da, (row <= col).astype(F32), (((0,), (0,)), ((), ())),
                           preferred_element_type=F32, precision=HIGHEST)
    cs_last = cs[L - 1:L, :]
    exp_cs = jnp.exp(cs)
    dend = jnp.exp(cs_last - cs)
    dec_chunk = jnp.exp(cs_last)

    rep = B_HEADS // B_GROUPS
    for g in range(B_GROUPS):
        bg = y_scr[:, B_INNER + g * B_STATE:B_INNER + (g + 1) * B_STATE]
        cg = y_scr[:, B_INNER + (B_GROUPS + g) * B_STATE:B_INNER + (B_GROUPS + g + 1) * B_STATE]
        scores = _dot_nt(cg.astype(BF16), bg.astype(BF16))
        for hh in range(rep):
            h = g * rep + hh
            colv = cs[:, h:h + 1]
            decay = jnp.exp(jnp.where(causal, colv - cs_t[h:h + 1, :], -jnp.inf))
            xs_h = y_scr[:, h * B_HEADDIM:(h + 1) * B_HEADDIM]
            xdt = (xs_h * dt[:, h:h + 1]).astype(BF16)
            yd = _dot((scores * decay).astype(BF16), xdt)
            hprev = h_scr[h]
            yo = _dot((cg * exp_cs[:, h:h + 1]).astype(BF16), hprev.astype(BF16))
            y_scr[:, h * B_HEADDIM:(h + 1) * B_HEADDIM] = yd + yo + dsk_ref[:, h * B_HEADDIM:(h + 1) * B_HEADDIM] * xs_h
            st = _dot_tn((bg * dend[:, h:h + 1]).astype(BF16), xdt)
            h_scr[h] = dec_chunk[:, h:h + 1] * hprev + st

    y = y_scr[:, 0:B_INNER] * _silu(z_ref[...])
    yb_ref[...] = _rms(y, bn_ref[...]).astype(BF16)
    st_ref[...] = h_scr[...]


def _ssd_prompt(proj, cw, cb, dtb128, alog128, dsk_e, bnorm):
    S = proj.shape[0]
    return pl.pallas_call(
        _ssd_prompt_kernel,
        grid=(S // CHUNK,),
        in_specs=[
            pl.BlockSpec((CHUNK, B_CONV_DIM), lambda i: (i, COL_XBC // B_CONV_DIM)),
            pl.BlockSpec((CHUNK, B_INNER), lambda i: (i, COL_Z // B_INNER)),
            pl.BlockSpec((CHUNK, 128), lambda i: (i, COL_DT // 128)),
            pl.BlockSpec((B_CONV, B_CONV_DIM), lambda i: (0, 0)),
            pl.BlockSpec((1, B_CONV_DIM), lambda i: (0, 0)),
            pl.BlockSpec((1, 128), lambda i: (0, 0)),
            pl.BlockSpec((1, 128), lambda i: (0, 0)),
            pl.BlockSpec((1, B_INNER), lambda i: (0, 0)),
            pl.BlockSpec((1, B_INNER), lambda i: (0, 0)),
        ],
        out_specs=[
            pl.BlockSpec((CHUNK, B_INNER), lambda i: (i, 0)),
            pl.BlockSpec((B_HEADS, B_STATE, B_HEADDIM), lambda i: (0, 0, 0)),
        ],
        out_shape=[
            jax.ShapeDtypeStruct((S, B_INNER), BF16),
            jax.ShapeDtypeStruct((B_HEADS, B_STATE, B_HEADDIM), F32),
        ],
        scratch_shapes=[
            pltpu.VMEM((8, B_CONV_DIM), F32),
            pltpu.VMEM((B_HEADS, B_STATE, B_HEADDIM), F32),
            pltpu.VMEM((CHUNK, B_CONV_DIM), F32),
        ],
        compiler_params=_cparams(("arbitrary",)),
        name="ssd_prompt",
    )(proj, proj, proj, cw, cb, dtb128, alog128, dsk_e, bnorm)


def _ssd_sample_kernel(xbc_ref, cprev_ref, z_ref, dt_ref, h0_ref, cw_ref, cb_ref, dtb_ref, alog_ref,
                       dsk_ref, bn_ref, e_ref, bm_ref, s_ref,
                       yb_ref, hn_ref, xp_scr, r16_scr, prod_scr, ex_scr, a_scr, b_scr, *, T):
    H, P, N, G = B_HEADS, B_HEADDIM, B_STATE, B_GROUPS
    rep = H // G
    pairs = [(t, s) for t in range(T) for s in range(t + 1)]
    xp_scr[0:B_CONV - 1, :] = cprev_ref[0]
    xp_scr[B_CONV - 1:B_CONV - 1 + T, :] = xbc_ref[0]
    acc = xp_scr[0:T, :] * cw_ref[0:1, :]
    for k in range(1, B_CONV):
        acc = acc + xp_scr[k:k + T, :] * cw_ref[k:k + 1, :]
    xc = _silu(acc + cb_ref[...])
    xs = xc[:, 0:B_INNER]
    bm = xc[:, B_INNER:B_INNER + G * N]
    cm = xc[:, B_INNER + G * N:]

    lane = lax.broadcasted_iota(jnp.int32, (1, 128), 1)
    a_row = jnp.where(lane < H, -jnp.exp(alog_ref[...]), 0.0)
    dt = _softplus(dt_ref[0] + dtb_ref[...])
    da = dt * a_row
    cs_rows = [da[0:1, :]]
    for t in range(1, T):
        cs_rows.append(cs_rows[-1] + da[t:t + 1, :])
    cs_last = cs_rows[-1]

    prod_scr[...] = jnp.zeros_like(prod_scr)
    for idx, (t, s) in enumerate(pairs):
        prod_scr[idx:idx + 1, :] = cm[t:t + 1, :] * bm[s:s + 1, :]
    gh = jnp.dot(prod_scr[...], s_ref[...], preferred_element_type=F32, precision=HIGHEST)
    ex_scr[...] = jnp.zeros_like(ex_scr)
    ex_scr[0:T, :] = dt
    for t in range(T):
        ex_scr[T + t:T + t + 1, :] = jnp.exp(cs_rows[t])
    for idx, (t, s) in enumerate(pairs):
        ex_scr[2 * T + idx:2 * T + idx + 1, :] = gh[idx:idx + 1, :] * jnp.exp(cs_rows[t] - cs_rows[s])
    ex = jnp.dot(ex_scr[...], e_ref[...], preferred_element_type=F32, precision=HIGHEST)
    xdt = xs * ex[0:T, :]

    for g in range(G):
        r16_scr[g * T:(g + 1) * T, :] = cm[:, g * N:(g + 1) * N]
    h2d = h0_ref[0].reshape(H * P, N)
    r = _dot_nt(r16_scr[...].astype(BF16), h2d.astype(BF16))
    gw = rep * P
    y_rows = []
    for t in range(T):
        yo = jnp.concatenate([r[g * T + t:g * T + t + 1, g * gw:(g + 1) * gw] for g in range(G)], axis=1)
        y_rows.append(yo * ex[T + t:T + t + 1, :])
    for idx, (t, s) in enumerate(pairs):
        y_rows[t] = y_rows[t] + ex[2 * T + idx:2 * T + idx + 1, :] * xdt[s:s + 1, :]
    for t in range(T):
        y = y_rows[t] + dsk_ref[...] * xs[t:t + 1, :]
        y = y * _silu(z_ref[0, t:t + 1, :])
        yb_ref[0, t:t + 1, :] = _rms(y, bn_ref[...]).astype(BF16)

    eye = (lax.broadcasted_iota(jnp.int32, (H, 128), 0) == lax.broadcasted_iota(jnp.int32, (H, 128), 1)).astype(F32)

    def to_col(v):
        return jnp.sum(jnp.broadcast_to(v, (H, 128)) * eye, axis=1, keepdims=True)

    a_scr[...] = jnp.zeros_like(a_scr)
    b_scr[...] = jnp.zeros_like(b_scr)
    for t in range(T):
        dcol = to_col(jnp.exp(cs_last - cs_rows[t]))
        a_scr[t * H:(t + 1) * H, :] = jnp.broadcast_to(xdt[t:t + 1, :], (H, H * P)) * bm_ref[...]
        for g in range(G):
            b_scr[t * H + g * rep:t * H + (g + 1) * rep, :] = (
                jnp.broadcast_to(bm[t:t + 1, g * N:(g + 1) * N], (rep, N)) * dcol[g * rep:(g + 1) * rep, :])
    st = _dot_tn(a_scr[...].astype(BF16), b_scr[...].astype(BF16))
    dfull = jnp.broadcast_to(to_col(jnp.exp(cs_last)), (H, N))
    for h in range(H):
        hn_ref[0, h] = dfull[h:h + 1, :] * h0_ref[0, h] + st[h * P:(h + 1) * P, :]


def _ssd_sample(xbc, cprev, z, dtb, h0, layer, cw, cb, dtb128, alog128, dsk_e, bnorm, e_mat, blockmask, s_mat):
    B, T, _ = xbc.shape
    n_pair = -(-(T * (T + 1) // 2) // 8) * 8
    assert B_CONV - 1 + T <= 8 and T * B_HEADS <= 128
    kern = functools.partial(_ssd_sample_kernel, T=T)
    c2 = lambda b: (0, 0)
    return pl.pallas_call(
        kern,
        grid=(B,),
        in_specs=[
            pl.BlockSpec((1, T, B_CONV_DIM), lambda b: (b, 0, 0)),
            pl.BlockSpec((None, 1, B_CONV - 1, B_CONV_DIM), lambda b: (layer, b, 0, 0)),
            pl.BlockSpec((1, T, B_INNER), lambda b: (b, 0, 0)),
            pl.BlockSpec((1, T, 128), lambda b: (b, 0, 0)),
            pl.BlockSpec((None, 1, B_HEADS, B_HEADDIM, B_STATE), lambda b: (layer, b, 0, 0, 0)),
            pl.BlockSpec((B_CONV, B_CONV_DIM), c2),
            pl.BlockSpec((1, B_CONV_DIM), c2),
            pl.BlockSpec((1, 128), c2),
            pl.BlockSpec((1, 128), c2),
            pl.BlockSpec((1, B_INNER), c2),
            pl.BlockSpec((1, B_INNER), c2),
            pl.BlockSpec((128, B_INNER), c2),
            pl.BlockSpec((B_HEADS, B_INNER), c2),
            pl.BlockSpec((B_GROUPS * B_STATE, 128), c2),
        ],
        out_specs=[
            pl.BlockSpec((1, T, B_INNER), lambda b: (b, 0, 0)),
            pl.BlockSpec((1, B_HEADS, B_HEADDIM, B_STATE), lambda b: (b, 0, 0, 0)),
        ],
        out_shape=[
            jax.ShapeDtypeStruct((B, T, B_INNER), BF16),
            jax.ShapeDtypeStruct((B, B_HEADS, B_HEADDIM, B_STATE), F32),
        ],
        scratch_shapes=[
            pltpu.VMEM((8, B_CONV_DIM), F32),
            pltpu.VMEM((B_GROUPS * T, B_STATE), F32),
            pltpu.VMEM((n_pair, B_GROUPS * B_STATE), F32),
            pltpu.VMEM((2 * T + n_pair, 128), F32),
            pltpu.VMEM((128, B_INNER), F32),
            pltpu.VMEM((128, B_STATE), F32),
        ],
        compiler_params=_cparams(("parallel",)),
        name="ssd_sample",
    )(xbc, cprev, z, dtb, h0, cw, cb, dtb128, alog128, dsk_e, bnorm, e_mat, blockmask, s_mat)


def _c_prep_kernel(cq_ref, ckv_ref, kr_ref, krs_ref, cos_ref, sin_ref, qn_ref, kvn_ref,
                   wq1_ref, wq2_ref, wk_ref, wv_ref,
                   q_ref, k_ref, v_ref, ckvn_ref, krope_ref):
    cos = cos_ref[...]
    sin = sin_ref[...]
    cos8 = jnp.concatenate([cos] * C_HEADS, axis=1)
    sin8 = jnp.concatenate([sin] * C_HEADS, axis=1)
    cqn = _rms(cq_ref[...], qn_ref[...]).astype(BF16)
    q = _dot(cqn, wq1_ref[...]) * cos8 + _dot(cqn, wq2_ref[...]) * sin8
    q_ref[...] = (q * (C_SCALE * LOG2E)).astype(BF16)
    ckvn = _rms(ckv_ref[...], kvn_ref[...])
    ckvn_ref[...] = ckvn
    k128 = kr_ref[...] * cos + krs_ref[...] * sin
    krope_ref[...] = k128[:, C_NOPE:C_NOPE + C_ROPE]
    cb = ckvn.astype(BF16)
    k_ref[...] = (_dot(cb, wk_ref[...]) + jnp.concatenate([k128] * C_HEADS, axis=1)).astype(BF16)
    lane = lax.broadcasted_iota(jnp.int32, (1, C_HEADS * HEAD_PAD), 1)
    v_ref[...] = jnp.where(lane % HEAD_PAD == C_V, 1.0, _dot(cb, wv_ref[...])).astype(BF16)


def _c_prep(proj, cos, sin, qn, kvn, wq1, wq2, wk, wv, tm):
    n_tok = proj.shape[0]
    c2 = lambda i: (0, 0)
    hq = C_HEADS * HEAD_PAD
    return pl.pallas_call(
        _c_prep_kernel,
        grid=(n_tok // tm,),
        in_specs=[
            pl.BlockSpec((tm, C_Q_LORA), lambda i: (i, COL_CQ // C_Q_LORA)),
            pl.BlockSpec((tm, C_KV_LORA), lambda i: (i, COL_CKV // C_KV_LORA)),
            pl.BlockSpec((tm, 128), lambda i: (i, COL_KR // 128)),
            pl.BlockSpec((tm, 128), lambda i: (i, COL_KRS // 128)),
            pl.BlockSpec((tm, 128), lambda i: (i, 0)),
            pl.BlockSpec((tm, 128), lambda i: (i, 0)),
            pl.BlockSpec((1, C_Q_LORA), c2),
            pl.BlockSpec((1, C_KV_LORA), c2),
            pl.BlockSpec((C_Q_LORA, hq), c2),
            pl.BlockSpec((C_Q_LORA, hq), c2),
            pl.BlockSpec((C_KV_LORA, hq), c2),
            pl.BlockSpec((C_KV_LORA, hq), c2),
        ],
        out_specs=[
            pl.BlockSpec((tm, hq), lambda i: (i, 0)),
            pl.BlockSpec((tm, hq), lambda i: (i, 0)),
            pl.BlockSpec((tm, hq), lambda i: (i, 0)),
            pl.BlockSpec((tm, C_KV_LORA), lambda i: (i, 0)),
            pl.BlockSpec((tm, C_ROPE), lambda i: (i, 0)),
        ],
        out_shape=[
            jax.ShapeDtypeStruct((n_tok, hq), BF16),
            jax.ShapeDtypeStruct((n_tok, hq), BF16),
            jax.ShapeDtypeStruct((n_tok, hq), BF16),
            jax.ShapeDtypeStruct((n_tok, C_KV_LORA), F32),
            jax.ShapeDtypeStruct((n_tok, C_ROPE), F32),
        ],
        compiler_params=_cparams(("parallel",)),
        name="c_prep",
    )(proj, proj, proj, proj, cos, sin, qn, kvn, wq1, wq2, wk, wv)


def _flash_kernel(qi_ref, kj_ref, q_ref, k_ref, v_ref, o_ref, m_scr, acc_scr, *, tq, tk, hps):
    s_idx = pl.program_id(1)
    qi = qi_ref[s_idx]
    kj = kj_ref[s_idx]

    @pl.when(kj == 0)
    def _():
        m_scr[...] = jnp.full_like(m_scr, -jnp.inf)
        acc_scr[...] = jnp.zeros_like(acc_scr)

    def step(masked):
        if masked:
            visible = (lax.broadcasted_iota(jnp.int32, (tq, tk), 1)
                       <= lax.broadcasted_iota(jnp.int32, (tq, tk), 0))

        def scores(hh):
            lanes = slice(hh * HEAD_PAD, (hh + 1) * HEAD_PAD)
            return _dot_nt(q_ref[:, lanes], k_ref[:, lanes])

        s_next = scores(0)
        for hh in range(hps):
            lanes = slice(hh * HEAD_PAD, (hh + 1) * HEAD_PAD)
            s = s_next
            if hh + 1 < hps:
                s_next = scores(hh + 1)
            if masked:
                s = jnp.where(visible, s, -jnp.inf)
            m_prev = m_scr[hh]
            m_new = jnp.maximum(m_prev, jnp.max(s, axis=-1, keepdims=True))
            alpha = jnp.exp2(m_prev - m_new)
            p = jnp.exp2(s - jnp.concatenate([m_new] * (tk // HEAD_PAD), axis=1))
            m_scr[hh] = m_new
            acc_scr[hh] = alpha * acc_scr[hh] + _dot(p.astype(BF16), v_ref[:, lanes])

    @pl.when(kj < qi)
    def _():
        step(False)

    @pl.when(kj == qi)
    def _():
        step(True)
        for hh in range(hps):
            a = acc_scr[hh]
            o_ref[:, hh * C_V:(hh + 1) * C_V] = (a[:, 0:C_V] / a[:, C_V:C_V + 1]).astype(BF16)


def _flash(q, k, v, tq, hps):
    S = q.shape[0]
    tk = tq
    nq = S // tq
    qi = np.concatenate([np.full(i + 1, i, np.int32) for i in range(nq)])
    kj = np.concatenate([np.arange(i + 1, dtype=np.int32) for i in range(nq)])
    kern = functools.partial(_flash_kernel, tq=tq, tk=tk, hps=hps)
    grid_spec = pltpu.PrefetchScalarGridSpec(
        num_scalar_prefetch=2,
        grid=(C_HEADS // hps, int(qi.shape[0])),
        in_specs=[
            pl.BlockSpec((tq, hps * HEAD_PAD), lambda p, s, qi, kj: (qi[s], p)),
            pl.BlockSpec((tk, hps * HEAD_PAD), lambda p, s, qi, kj: (kj[s], p)),
            pl.BlockSpec((tk, hps * HEAD_PAD), lambda p, s, qi, kj: (kj[s], p)),
        ],
        out_specs=pl.BlockSpec((tq, hps * C_V), lambda p, s, qi, kj: (qi[s], p)),
        scratch_shapes=[
            pltpu.VMEM((hps, tq, HEAD_PAD), F32),
            pltpu.VMEM((hps, tq, HEAD_PAD), F32),
        ],
    )
    return pl.pallas_call(
        kern,
        grid_spec=grid_spec,
        out_shape=jax.ShapeDtypeStruct((S, C_HEADS * C_V), BF16),
        compiler_params=_cparams(("parallel", "arbitrary")),
        name="flash_prompt",
    )(jnp.asarray(qi), jnp.asarray(kj), q, k, v)


def _q_lat_kernel(q_ref, wt_ref, o_ref):
    o_ref[0] = _dot(q_ref[...], wt_ref[0]).astype(BF16)


def _q_lat(q, wukt):
    n = q.shape[0]
    return pl.pallas_call(
        _q_lat_kernel,
        grid=(C_HEADS,),
        in_specs=[
            pl.BlockSpec((n, HEAD_PAD), lambda h: (0, h)),
            pl.BlockSpec((1, HEAD_PAD, C_KV_LORA), lambda h: (h, 0, 0)),
        ],
        out_specs=pl.BlockSpec((1, n, C_KV_LORA), lambda h: (h, 0, 0)),
        out_shape=jax.ShapeDtypeStruct((C_HEADS, n, C_KV_LORA), BF16),
        compiler_params=_cparams(("parallel",)),
        name="q_lat",
    )(q, wukt)


def _attn_sample_kernel(pt_ref, qlat_ref, q128_ref, ckv_ref, krn_ref, kv_hbm, kr_hbm, o_ref,
                        kvbuf, krbuf, sem, newkv_scr, newkr_scr, *, layer, T, n_pages, cp, n_streams):
    b = pl.program_id(0)
    nb = pl.num_programs(0)
    n_chunks = n_pages // cp
    R = T * C_HEADS
    spp = cp // n_streams

    def copies(bb, c, slot):
        out = []
        for i in range(cp):
            page = pt_ref[bb, c * cp + i]
            out.append(pltpu.make_async_copy(kv_hbm.at[layer, page], kvbuf.at[slot, i], sem.at[0, slot]))
            out.append(pltpu.make_async_copy(kr_hbm.at[layer, page], krbuf.at[slot, i], sem.at[1, slot]))
        return out

    def start(bb, c, slot):
        for cpy in copies(bb, c, slot):
            cpy.start()

    def wait(bb, c, slot):
        for cpy in copies(bb, c, slot):
            cpy.wait()

    @pl.when(b == 0)
    def _():
        start(0, 0, 0)

    qlat = qlat_ref[0]
    qr = q128_ref[0][:, C_NOPE:C_NOPE + C_ROPE]

    def softmax_update(state, s):
        m_prev, l_prev, _ = state
        m_new = jnp.maximum(m_prev, jnp.max(s, axis=-1, keepdims=True))
        alpha = jnp.exp2(m_prev - m_new)
        p = jnp.exp2(s - m_new)
        l_new = alpha * l_prev + jnp.sum(p, axis=-1, keepdims=True)
        return m_new, l_new, alpha, p.astype(BF16)

    def online(state, s, kv):
        m_new, l_new, alpha, p = softmax_update(state, s)
        return m_new, l_new, state[2] * alpha + _dot(p, kv)

    def chunk(c, carry):
        slot = (b * n_chunks + c) % 2
        nxt = 1 - slot

        @pl.when(c + 1 < n_chunks)
        def _():
            start(b, c + 1, nxt)

        @pl.when((c + 1 == n_chunks) & (b + 1 < nb))
        def _():
            start(b + 1, 0, nxt)

        wait(b, c, slot)
        kvs, scores = [], []
        for si in range(n_streams):
            pages = pl.ds(si * spp, spp)
            kv = kvbuf[slot, pages].reshape(spp * PAGE_SIZE, C_KV_LORA).astype(BF16)
            kr = krbuf[slot, pages].reshape(spp * PAGE_SIZE, C_ROPE).astype(BF16)
            kvs.append(kv)
            scores.append(_dot_nt(qlat, kv) + _dot_nt(qr, kr))
        stats = [softmax_update(carry[si], scores[si]) for si in range(n_streams)]
        return tuple((m_new, l_new, carry[si][2] * alpha + _dot(p, kvs[si]))
                     for si, (m_new, l_new, alpha, p) in enumerate(stats))

    init = tuple((jnp.full((R, 1), -jnp.inf, F32), jnp.zeros((R, 1), F32), jnp.zeros((R, C_KV_LORA), F32))
                 for _ in range(n_streams))
    streams = lax.fori_loop(0, n_chunks, chunk, init)

    newkv_scr[...] = jnp.zeros_like(newkv_scr)
    newkr_scr[...] = jnp.zeros_like(newkr_scr)
    newkv_scr[0:T, :] = ckv_ref[0]
    newkr_scr[0:T, :] = krn_ref[0]
    kvn = newkv_scr[...].astype(BF16)
    krn = newkr_scr[...].astype(BF16)
    s = _dot_nt(qlat, kvn) + _dot_nt(qr, krn)
    t_row = lax.broadcasted_iota(jnp.int32, (R, 128), 0) % T
    key = lax.broadcasted_iota(jnp.int32, (R, 128), 1)
    s = jnp.where(key <= t_row, s, -jnp.inf)
    m_all, l_all, acc_all = online(streams[0], s, kvn)
    for m_i, l_i, acc_i in streams[1:]:
        m_new = jnp.maximum(m_all, m_i)
        wa = jnp.exp2(m_all - m_new)
        wi = jnp.exp2(m_i - m_new)
        l_all = wa * l_all + wi * l_i
        acc_all = wa * acc_all + wi * acc_i
        m_all = m_new
    o_ref[0] = (acc_all / l_all).astype(BF16)


def _attn_sample(page_table, qlat, q128, ckv_new, kr_new, cache_kv, cache_kr, layer, cp, n_streams):
    B, R, _ = qlat.shape
    T = ckv_new.shape[1]
    n_pages = page_table.shape[1]
    kern = functools.partial(_attn_sample_kernel, layer=layer, T=T, n_pages=n_pages, cp=cp, n_streams=n_streams)
    grid_spec = pltpu.PrefetchScalarGridSpec(
        num_scalar_prefetch=1,
        grid=(B,),
        in_specs=[
            pl.BlockSpec((1, R, C_KV_LORA), lambda b, pt: (b, 0, 0)),
            pl.BlockSpec((1, R, HEAD_PAD), lambda b, pt: (b, 0, 0)),
            pl.BlockSpec((1, T, C_KV_LORA), lambda b, pt: (b, 0, 0)),
            pl.BlockSpec((1, T, C_ROPE), lambda b, pt: (b, 0, 0)),
            pl.BlockSpec(memory_space=pl.ANY),
            pl.BlockSpec(memory_space=pl.ANY),
        ],
        out_specs=pl.BlockSpec((1, R, C_KV_LORA), lambda b, pt: (b, 0, 0)),
        scratch_shapes=[
            pltpu.VMEM((2, cp, PAGE_SIZE, C_KV_LORA), F32),
            pltpu.VMEM((2, cp, PAGE_SIZE, C_ROPE), F32),
            pltpu.SemaphoreType.DMA((2, 2)),
            pltpu.VMEM((128, C_KV_LORA), F32),
            pltpu.VMEM((128, C_ROPE), F32),
        ],
    )
    return pl.pallas_call(
        kern,
        grid_spec=grid_spec,
        out_shape=jax.ShapeDtypeStruct((B, R, C_KV_LORA), BF16),
        compiler_params=_cparams(("arbitrary",)),
        name="attn_sample",
    )(page_table, qlat, q128, ckv_new, kr_new, cache_kv, cache_kr)


def _uv_proj_kernel(o_ref, w_ref, y_ref):
    y_ref[0] = _dot(o_ref[0], w_ref[0]).astype(BF16)


def _uv_proj(olat, wuv):
    H, n, _ = olat.shape
    return pl.pallas_call(
        _uv_proj_kernel,
        grid=(H,),
        in_specs=[
            pl.BlockSpec((1, n, C_KV_LORA), lambda h: (h, 0, 0)),
            pl.BlockSpec((1, C_KV_LORA, C_V), lambda h: (h, 0, 0)),
        ],
        out_specs=pl.BlockSpec((1, n, C_V), lambda h: (h, 0, 0)),
        out_shape=jax.ShapeDtypeStruct((H, n, C_V), BF16),
        compiler_params=_cparams(("parallel",)),
        name="uv_proj",
    )(olat, wuv)


def _merge_kernel(x_ref, g0_ref, g1_ref, g2_ref, ya_ref, yb_ref, yc_ref, wpa_ref, wpb_ref, wpc_ref, wo_ref, o_ref):
    m = _sigmoid(g0_ref[...]) * _dot(ya_ref[...], wpa_ref[...])
    m = m + _sigmoid(g1_ref[...]) * _dot(yb_ref[...], wpb_ref[...])
    m = m + _sigmoid(g2_ref[...]) * _dot(yc_ref[...], wpc_ref[...])
    o_ref[...] = x_ref[...] + _dot(m.astype(BF16), wo_ref[...])


def _merge(x, proj, ya, yb, yc, wpa, wpb, wpc, wo, tm):
    n_tok = x.shape[0]
    c2 = lambda i: (0, 0)
    g0 = COL_GATE // D_MODEL
    return pl.pallas_call(
        _merge_kernel,
        grid=(n_tok // tm,),
        in_specs=[
            pl.BlockSpec((tm, D_MODEL), lambda i: (i, 0)),
            pl.BlockSpec((tm, D_MODEL), lambda i: (i, g0)),
            pl.BlockSpec((tm, D_MODEL), lambda i: (i, g0 + 1)),
            pl.BlockSpec((tm, D_MODEL), lambda i: (i, g0 + 2)),
            pl.BlockSpec((tm, A_WIDTH), lambda i: (i, 0)),
            pl.BlockSpec((tm, B_INNER), lambda i: (i, 0)),
            pl.BlockSpec((tm, C_HEADS * C_V), lambda i: (i, 0)),
            pl.BlockSpec((A_WIDTH, D_MODEL), c2),
            pl.BlockSpec((B_INNER, D_MODEL), c2),
            pl.BlockSpec((C_HEADS * C_V, D_MODEL), c2),
            pl.BlockSpec((D_MODEL, D_MODEL), c2),
        ],
        out_specs=pl.BlockSpec((tm, D_MODEL), lambda i: (i, 0)),
        out_shape=jax.ShapeDtypeStruct((n_tok, D_MODEL), F32),
        compiler_params=_cparams(("parallel",)),
        name="merge",
    )(x, proj, proj, proj, ya, yb, yc, wpa, wpb, wpc, wo)


def _ffn_kernel(x_ref, g_ref, wu_ref, wd_ref, o_ref, h_scr, acc_scr):
    j = pl.program_id(1)

    @pl.when(j == 0)
    def _():
        x = x_ref[...]
        h_scr[...] = _rms(x, g_ref[...]).astype(BF16)
        acc_scr[...] = x

    u = jnp.maximum(_dot(h_scr[...], wu_ref[...]), 0.0)
    acc_scr[...] += _dot((u * u).astype(BF16), wd_ref[...])

    @pl.when(j == pl.num_programs(1) - 1)
    def _():
        o_ref[...] = acc_scr[...]


def _ffn(x, g, wu, wd, tm, tf):
    n_tok = x.shape[0]
    return pl.pallas_call(
        _ffn_kernel,
        grid=(n_tok // tm, D_FF // tf),
        in_specs=[
            pl.BlockSpec((tm, D_MODEL), lambda i, j: (i, 0)),
            pl.BlockSpec((1, D_MODEL), lambda i, j: (0, 0)),
            pl.BlockSpec((D_MODEL, tf), lambda i, j: (0, j)),
            pl.BlockSpec((tf, D_MODEL), lambda i, j: (j, 0)),
        ],
        out_specs=pl.BlockSpec((tm, D_MODEL), lambda i, j: (i, 0)),
        out_shape=jax.ShapeDtypeStruct((n_tok, D_MODEL), F32),
        scratch_shapes=[pltpu.VMEM((tm, D_MODEL), BF16), pltpu.VMEM((tm, D_MODEL), F32)],
        compiler_params=_cparams(("parallel", "arbitrary")),
        name="ffn",
    )(x, g, wu, wd)


def _ple_kernel(x_ref, g_ref, wg_ref, p_ref, wp_ref, gf_ref, o_ref, *, final):
    x = x_ref[...]
    pg = _sigmoid(_dot(_rms(x, g_ref[...]).astype(BF16), wg_ref[...]))
    y = x + pg * _dot(p_ref[...].astype(BF16), wp_ref[...])
    if final:
        y = _rms(y, gf_ref[...])
    o_ref[...] = y


def _ple(x, g, wg, p, wp, gf, tm, final):
    n_tok = x.shape[0]
    d_ple = p.shape[1]
    c2 = lambda i: (0, 0)
    return pl.pallas_call(
        functools.partial(_ple_kernel, final=final),
        grid=(n_tok // tm,),
        in_specs=[
            pl.BlockSpec((tm, D_MODEL), lambda i: (i, 0)),
            pl.BlockSpec((1, D_MODEL), c2),
            pl.BlockSpec((D_MODEL, D_MODEL), c2),
            pl.BlockSpec((tm, d_ple), lambda i: (i, 0)),
            pl.BlockSpec((d_ple, D_MODEL), c2),
            pl.BlockSpec((1, D_MODEL), c2),
        ],
        out_specs=pl.BlockSpec((tm, D_MODEL), lambda i: (i, 0)),
        out_shape=jax.ShapeDtypeStruct((n_tok, D_MODEL), F32),
        compiler_params=_cparams(("parallel",)),
        name="ple",
    )(x, g, wg, p, wp, gf)


def _prep_layer_weights(w_in, w_uq, w_ukv):
    sizes = (N_BRANCH * D_MODEL, A_WIDTH, A_WIDTH, B_INNER, B_CONV_DIM, B_HEADS, C_Q_LORA, C_KV_LORA, C_ROPE)
    idx = [int(v) for v in np.cumsum(sizes)[:-1]]
    gates, a_u, a_v, z, xbc, dt, c_q, c_kv, k_r = jnp.split(w_in, idx, axis=1)
    half = C_ROPE // 2
    zc = lambda n: jnp.zeros((D_MODEL, n), w_in.dtype)
    k_rs = jnp.concatenate([k_r[:, half:], k_r[:, :half]], axis=1)
    w_in_r = jnp.concatenate([
        c_q, dt, zc(128 - B_HEADS), c_kv,
        zc(C_NOPE), k_r, zc(HEAD_PAD - C_NOPE - C_ROPE),
        zc(C_NOPE), k_rs, zc(HEAD_PAD - C_NOPE - C_ROPE),
        gates, xbc, z, a_u, a_v], axis=1).astype(BF16)

    uq = w_uq.reshape(C_Q_LORA, C_HEADS, C_NOPE + C_ROPE)
    uq_n, uq_r = uq[..., :C_NOPE], uq[..., C_NOPE:]
    uq_rs = jnp.concatenate([uq_r[..., half:], uq_r[..., :half]], axis=-1)
    zq = lambda n: jnp.zeros((C_Q_LORA, C_HEADS, n), w_uq.dtype)
    wq1 = jnp.concatenate([uq_n, uq_r, zq(HEAD_PAD - C_NOPE - C_ROPE)], axis=-1).reshape(C_Q_LORA, -1).astype(BF16)
    wq2 = jnp.concatenate([zq(C_NOPE), uq_rs, zq(HEAD_PAD - C_NOPE - C_ROPE)], axis=-1).reshape(C_Q_LORA, -1).astype(BF16)

    ukv = w_ukv.reshape(C_KV_LORA, C_HEADS, C_NOPE + C_V)
    uk, uv = ukv[..., :C_NOPE], ukv[..., C_NOPE:]
    wk = jnp.concatenate([uk, jnp.zeros((C_KV_LORA, C_HEADS, HEAD_PAD - C_NOPE), w_ukv.dtype)], axis=-1)
    wk = wk.reshape(C_KV_LORA, -1).astype(BF16)
    wv = jnp.concatenate([uv, jnp.zeros((C_KV_LORA, C_HEADS, HEAD_PAD - C_V), w_ukv.dtype)], axis=-1)
    wv = wv.reshape(C_KV_LORA, -1).astype(BF16)
    wukt = jnp.transpose(wk.reshape(C_KV_LORA, C_HEADS, HEAD_PAD), (1, 2, 0))
    wuv_h = jnp.transpose(uv, (1, 0, 2)).astype(BF16)
    return w_in_r, wq1, wq2, wk, wv, wukt, wuv_h


def _rope_tables(pos):
    half = C_ROPE // 2
    inv = jnp.power(ROPE_BASE, -jnp.arange(half, dtype=F32) * (2.0 / C_ROPE))
    ang = pos.astype(F32)[:, None] * inv[None, :]
    cos, sin = jnp.cos(ang), jnp.sin(ang)
    n = pos.shape[0]
    pad = jnp.zeros((n, HEAD_PAD - C_NOPE - C_ROPE), F32)
    cos_t = jnp.concatenate([jnp.ones((n, C_NOPE), F32), cos, cos, pad], axis=1)
    sin_t = jnp.concatenate([jnp.zeros((n, C_NOPE), F32), -sin, sin, pad], axis=1)
    return cos_t, sin_t


def _pad128(v):
    return jnp.concatenate([v, jnp.zeros((128 - v.shape[0],), v.dtype)])[None, :]


def _token_tile(n, pref):
    t = pref
    while n % t:
        t //= 2
    return t


def kernel(x_prompt, x_sample, cache_kv_latent, cache_k_rope, state_ssm, state_conv, page_table, p_prompt, p_sample, ln_mix, w_in, sgu_ln_w, sgu_ln_b, w_s, b_s, conv_w, conv_b, dt_bias, a_log, d_skip, b_norm, q_norm, w_uq, kv_norm, w_ukv, w_pa, w_pb, w_pc, w_o, ln_ffn, w_up, w_down, ln_ple, w_ple_gate, w_ple, ln_final):
    depth = w_in.shape[0]
    _, S, _ = x_prompt.shape
    B, T, _ = x_sample.shape
    n_pages = page_table.shape[1]
    past_len = n_pages * PAGE_SIZE
    ns = B * T
    assert x_prompt.shape[0] == 1 and S % CHUNK == 0 and ns % CHUNK == 0 and CHUNK % T == 0

    xp = x_prompt.reshape(S, D_MODEL)
    xs = x_sample.reshape(ns, D_MODEL)
    cos_p, sin_p = _rope_tables(jnp.arange(S))
    cos_s, sin_s = _rope_tables(past_len + (jnp.arange(ns) % T))

    hp = np.arange(B_INNER) // B_HEADDIM
    e_mat = jnp.asarray((np.arange(128)[:, None] == hp[None, :]).astype(np.float32))
    blockmask = jnp.asarray((np.arange(B_HEADS)[:, None] == hp[None, :]).astype(np.float32))
    gn = np.arange(B_GROUPS * B_STATE) // B_STATE
    hg = np.where(np.arange(128) < B_HEADS, np.arange(128) // (B_HEADS // B_GROUPS), -1)
    s_mat = jnp.asarray((gn[:, None] == hg[None, :]).astype(np.float32))

    tm_p = _token_tile(S, 1024)
    tm_s = _token_tile(ns, 512)
    tq = _token_tile(S, 512)
    cp = _token_tile(n_pages, ATTN_PAGES_PER_CHUNK)
    tile_rep = CHUNK // T
    eye_rep = jnp.eye(tile_rep, dtype=F32)

    outs_p, outs_s = [], []
    for i in range(depth):
        w_in_r, wq1, wq2, wk, wv, wukt, wuv_h = _prep_layer_weights(w_in[i], w_uq[i], w_ukv[i])
        g_mix = ln_mix[i][None, :]
        lnw, lnb = sgu_ln_w[i][None, :], sgu_ln_b[i][None, :]
        ws_p = w_s[i][:, :CHUNK, :CHUNK]
        bs_p = b_s[i][:, :CHUNK, None]
        ws_t = jnp.tril(w_s[i][:, :T, :T])
        ws_s = jnp.einsum('ab,gts->gatbs', eye_rep, ws_t).reshape(A_GROUPS, CHUNK, CHUNK)
        bs_s = jnp.tile(b_s[i][:, :T], (1, tile_rep))[:, :, None]
        cw, cb = conv_w[i], conv_b[i][None, :]
        dtb128, alog128 = _pad128(dt_bias[i]), _pad128(a_log[i])
        dsk_e = jnp.repeat(d_skip[i], B_HEADDIM)[None, :]
        bn = b_norm[i][None, :]
        qn, kvn = q_norm[i][None, :], kv_norm[i][None, :]
        wpa, wpb, wpc, wo = (w.astype(BF16) for w in (w_pa[i], w_pb[i], w_pc[i], w_o[i]))
        wu, wd = w_up[i].astype(BF16), w_down[i].astype(BF16)
        wg, wp = w_ple_gate[i].astype(BF16), w_ple[i].astype(BF16)
        g_ffn, g_ple, g_fin = ln_ffn[i][None, :], ln_ple[i][None, :], ln_final[None, :]
        final = i == depth - 1

        proj = _in_proj(xp, g_mix, w_in_r, tm_p, 1024)
        ya, av = _gate_a(proj, lnw, lnb, ws_p, bs_p, _token_tile(S, 512))
        yb, ssm_t = _ssd_prompt(proj, cw, cb, dtb128, alog128, dsk_e, bn)
        q, k, v, ckvn, krope = _c_prep(proj, cos_p, sin_p, qn, kvn, wq1, wq2, wk, wv, _token_tile(S, 512))
        yc = _flash(q, k, v, tq, FLASH_HEADS_PER_STEP)
        xp = _merge(xp, proj, ya, yb, yc, wpa, wpb, wpc, wo, _token_tile(S, 512))
        xp = _ffn(xp, g_ffn, wu, wd, tm_p, 1024)
        xp = _ple(xp, g_ple, wg, p_prompt[i].reshape(S, -1), wp, g_fin, _token_tile(S, 512), final)
        outs_p.append((ckvn.reshape(1, S, C_KV_LORA), krope.reshape(1, S, C_ROPE),
                       jnp.swapaxes(ssm_t, 1, 2)[None],
                       proj[S - (B_CONV - 1):, COL_XBC:COL_XBC + B_CONV_DIM][None],
                       av[S - CHUNK:][None]))

        proj = _in_proj(xs, g_mix, w_in_r, tm_s, 1024)
        ya, av = _gate_a(proj, lnw, lnb, ws_s, bs_s, tm_s)
        xbc_s = proj[:, COL_XBC:COL_XBC + B_CONV_DIM].reshape(B, T, B_CONV_DIM)
        z_s = proj[:, COL_Z:COL_Z + B_INNER].reshape(B, T, B_INNER)
        dt_s = proj[:, COL_DT:COL_DT + 128].reshape(B, T, 128)
        yb, ssm_new = _ssd_sample(xbc_s, state_conv, z_s, dt_s, state_ssm, i, cw, cb, dtb128, alog128,
                                  dsk_e, bn, e_mat, blockmask, s_mat)
        q, _, _, ckvn, krope = _c_prep(proj, cos_s, sin_s, qn, kvn, wq1, wq2, wk, wv, tm_s)
        qlat = _q_lat(q, wukt)
        qlat = jnp.transpose(qlat.reshape(C_HEADS, B, T, C_KV_LORA), (1, 0, 2, 3)).reshape(B, C_HEADS * T, C_KV_LORA)
        q128 = jnp.transpose(q.reshape(B, T, C_HEADS, HEAD_PAD), (0, 2, 1, 3)).reshape(B, C_HEADS * T, HEAD_PAD)
        olat = _attn_sample(page_table, qlat, q128, ckvn.reshape(B, T, C_KV_LORA), krope.reshape(B, T, C_ROPE),
                            cache_kv_latent, cache_k_rope, i, cp, min(ATTN_STREAMS, cp))
        olat = jnp.transpose(olat.reshape(B, C_HEADS, T, C_KV_LORA), (1, 0, 2, 3)).reshape(C_HEADS, ns, C_KV_LORA)
        yc = jnp.transpose(_uv_proj(olat, wuv_h), (1, 0, 2)).reshape(ns, C_HEADS * C_V)
        xs = _merge(xs, proj, ya, yb.reshape(ns, B_INNER), yc, wpa, wpb, wpc, wo, tm_s)
        xs = _ffn(xs, g_ffn, wu, wd, tm_s, 1024)
        xs = _ple(xs, g_ple, wg, p_sample[i].reshape(ns, -1), wp, g_fin, tm_s, final)
        outs_s.append((ckvn.reshape(B, T, C_KV_LORA), krope.reshape(B, T, C_ROPE), ssm_new,
                       xbc_s[:, T - (B_CONV - 1):], av.reshape(B, T, A_WIDTH)))

    kv_p, kr_p, ssm_p, conv_p, v_p = [jnp.stack(t) for t in zip(*outs_p)]
    kv_s, kr_s, ssm_s, conv_s, v_s = [jnp.stack(t) for t in zip(*outs_s)]
    return (xp.reshape(1, S, D_MODEL), xs.reshape(B, T, D_MODEL), kv_p, kr_p, ssm_p, conv_p, v_p,
            kv_s, kr_s, ssm_s, conv_s, v_s)
```

```python
import functools
import math

import numpy as np
import jax
import jax.numpy as jnp
from jax import lax
from jax.experimental import pallas as pl
from jax.experimental.pallas import tpu as pltpu

F32 = jnp.float32
BF16 = jnp.bfloat16

NORM_EPS = 1e-6
D_MODEL = 1024
N_BRANCH = 3
A_WIDTH = 512
A_GROUPS = 4
CHUNK = 128
B_INNER = 1024
B_HEADDIM = 64
B_HEADS = 16
B_GROUPS = 4
B_STATE = 128
B_CONV = 4
B_CONV_DIM = B_INNER + 2 * B_GROUPS * B_STATE
C_HEADS = 8
C_NOPE = 64
C_ROPE = 32
C_V = 64
C_KV_LORA = 256
C_Q_LORA = 384
ROPE_BASE = 10000.0
C_SCALE = (C_NOPE + C_ROPE) ** -0.5
LOG2E = math.log2(math.e)
FLASH_HEADS_PER_STEP = 8
FLASH_LOOKAHEAD = 2
V_ROWS = 80
ATTN_PAGES_PER_CHUNK = 32
ATTN_SLOTS = 4
ATTN_STREAMS = 4
D_FF = 4 * D_MODEL
PAGE_SIZE = 128
HEAD_PAD = 128

COL_CQ = 0
COL_DT = 384
COL_CKV = 512
COL_KR = 768
COL_KRS = 896
COL_GATE = 1024
COL_XBC = 4096
COL_Z = 6144
COL_AU = 7168
COL_AV = 7680
D_IN_PAD = 8192

VMEM_LIMIT = 56 * 1024 * 1024
HIGHEST = lax.Precision.HIGHEST


def _cparams(sem):
    return pltpu.CompilerParams(dimension_semantics=sem, vmem_limit_bytes=VMEM_LIMIT)


def _rms(x, g):
    ms = jnp.mean(x * x, axis=-1, keepdims=True)
    return x * lax.rsqrt(ms + NORM_EPS) * g


def _sigmoid(x):
    return 1.0 / (1.0 + jnp.exp(-x))


def _silu(x):
    return x * _sigmoid(x)


def _gelu(x):
    c = math.sqrt(2.0 / math.pi)
    return 0.5 * x * (1.0 + jnp.tanh(c * (x + 0.044715 * (x * x * x))))


def _softplus(x):
    return jnp.maximum(x, 0.0) + jnp.log(1.0 + jnp.exp(-jnp.abs(x)))


def _dot(a, b):
    return jnp.dot(a, b, preferred_element_type=F32)


def _dot_nt(a, b):
    return lax.dot_general(a, b, (((1,), (1,)), ((), ())), preferred_element_type=F32)


def _dot_tn(a, b):
    return lax.dot_general(a, b, (((0,), (0,)), ((), ())), preferred_element_type=F32)


def _in_proj_kernel(x_ref, g_ref, w_ref, o_ref, h_scr):
    @pl.when(pl.program_id(1) == 0)
    def _():
        h_scr[...] = _rms(x_ref[...], g_ref[...]).astype(BF16)

    o_ref[...] = _dot(h_scr[...], w_ref[...])


def _in_proj(x, g, w, tm, tn):
    n_tok = x.shape[0]
    n_out = w.shape[1]
    return pl.pallas_call(
        _in_proj_kernel,
        grid=(n_tok // tm, n_out // tn),
        in_specs=[
            pl.BlockSpec((tm, D_MODEL), lambda i, j: (i, 0)),
            pl.BlockSpec((1, D_MODEL), lambda i, j: (0, 0)),
            pl.BlockSpec((D_MODEL, tn), lambda i, j: (0, j)),
        ],
        out_specs=pl.BlockSpec((tm, tn), lambda i, j: (i, j)),
        out_shape=jax.ShapeDtypeStruct((n_tok, n_out), F32),
        scratch_shapes=[pltpu.VMEM((tm, D_MODEL), BF16)],
        compiler_params=_cparams(("parallel", "arbitrary")),
        name="in_proj",
    )(x, g, w)


def _gate_a_kernel(u_ref, v_ref, lnw_ref, lnb_ref, ws_ref, bs_ref, ya_ref, av_ref, *, n_chunks):
    row = lax.broadcasted_iota(jnp.int32, (CHUNK, CHUNK), 0)
    col = lax.broadcasted_iota(jnp.int32, (CHUNK, CHUNK), 1)
    causal = col <= row
    gd = A_WIDTH // A_GROUPS
    for c in range(n_chunks):
        rows = pl.ds(c * CHUNK, CHUNK)
        v = _gelu(v_ref[rows, :])
        mu = jnp.mean(v, axis=-1, keepdims=True)
        vc = v - mu
        var = jnp.mean(vc * vc, axis=-1, keepdims=True)
        av = vc * lax.rsqrt(var + NORM_EPS) * lnw_ref[...] + lnb_ref[...]
        av_ref[rows, :] = av
        u = _gelu(u_ref[rows, :])
        for g in range(A_GROUPS):
            w = jnp.where(causal, ws_ref[g], 0.0).astype(BF16)
            s = _dot(w, av[:, g * gd:(g + 1) * gd].astype(BF16)) + bs_ref[g]
            ya_ref[rows, g * gd:(g + 1) * gd] = (u[:, g * gd:(g + 1) * gd] * s).astype(BF16)


def _gate_a(proj, lnw, lnb, ws, bs, tm):
    n_tok = proj.shape[0]
    kern = functools.partial(_gate_a_kernel, n_chunks=tm // CHUNK)
    return pl.pallas_call(
        kern,
        grid=(n_tok // tm,),
        in_specs=[
            pl.BlockSpec((tm, A_WIDTH), lambda i: (i, COL_AU // A_WIDTH)),
            pl.BlockSpec((tm, A_WIDTH), lambda i: (i, COL_AV // A_WIDTH)),
            pl.BlockSpec((1, A_WIDTH), lambda i: (0, 0)),
            pl.BlockSpec((1, A_WIDTH), lambda i: (0, 0)),
            pl.BlockSpec((A_GROUPS, CHUNK, CHUNK), lambda i: (0, 0, 0)),
            pl.BlockSpec((A_GROUPS, CHUNK, 1), lambda i: (0, 0, 0)),
        ],
        out_specs=[
            pl.BlockSpec((tm, A_WIDTH), lambda i: (i, 0)),
            pl.BlockSpec((tm, A_WIDTH), lambda i: (i, 0)),
        ],
        out_shape=[
            jax.ShapeDtypeStruct((n_tok, A_WIDTH), BF16),
            jax.ShapeDtypeStruct((n_tok, A_WIDTH), F32),
        ],
        compiler_params=_cparams(("parallel",)),
        name="gate_a",
    )(proj, proj, lnw, lnb, ws, bs)


def _ssd_prompt_kernel(xbc_ref, z_ref, dt_ref, cw_ref, cb_ref, dtb_ref, alog_ref, dsk_ref, bn_ref, e_ref,
                       yb_ref, st_ref, xp_scr, h_scr, y_scr, xde_scr):
    L, P, N = CHUNK, B_HEADDIM, B_STATE
    i = pl.program_id(0)

    @pl.when(i == 0)
    def _():
        xp_scr[0:8, :] = jnp.zeros((8, B_CONV_DIM), F32)
        h_scr[...] = jnp.zeros_like(h_scr)

    xp_scr[8:8 + L, :] = xbc_ref[...]
    acc = xp_scr[8:8 + L, :] * cw_ref[B_CONV - 1:B_CONV, :]
    for sh in range(1, B_CONV):
        acc = acc + xp_scr[8 - sh:8 - sh + L, :] * cw_ref[B_CONV - 1 - sh:B_CONV - sh, :]
    y_scr[...] = _silu(acc + cb_ref[...])
    xp_scr[0:8, :] = xp_scr[L:L + 8, :]

    lane = lax.broadcasted_iota(jnp.int32, (1, 128), 1)
    a_row = jnp.where(lane < B_HEADS, -jnp.exp(alog_ref[...]), 0.0)
    dt = _softplus(dt_ref[...] + dtb_ref[...])
    da = dt * a_row
    row = lax.broadcasted_iota(jnp.int32, (L, L), 0)
    col = lax.broadcasted_iota(jnp.int32, (L, L), 1)
    causal = col <= row
    tri = causal.astype(F32)
    cs = jnp.dot(tri, da, preferred_element_type=F32, precision=HIGHEST) * LOG2E
    cs_t = lax.dot_general(da, (row <= col).astype(F32), (((0,), (0,)), ((), ())),
                           preferred_element_type=F32, precision=HIGHEST) * LOG2E
    cs_last = cs[L - 1:L, :]
    dec_e = jnp.dot(jnp.broadcast_to(jnp.exp2(cs_last), (8, 128)), e_ref[...],
                    preferred_element_type=F32, precision=HIGHEST)[0:1, :]

    rep = B_HEADS // B_GROUPS
    gw = rep * P
    for g in range(B_GROUPS):
        bg = y_scr[:, B_INNER + g * N:B_INNER + (g + 1) * N].astype(BF16)
        cg = y_scr[:, B_INNER + (B_GROUPS + g) * N:B_INNER + (B_GROUPS + g + 1) * N].astype(BF16)
        scores = _dot_nt(cg, bg)
        hg = h_scr[g]
        yo = _dot(cg, hg.astype(BF16))
        for pr in range(rep // 2):
            ha = g * rep + 2 * pr
            cols = slice(ha * P, (ha + 2) * P)
            col_a = jnp.broadcast_to(cs[:, ha:ha + 1], (L, L))
            col_b = jnp.broadcast_to(cs[:, ha + 1:ha + 2], (L, L))
            m_a = scores * jnp.exp2(jnp.where(causal, col_a - cs_t[ha:ha + 1, :], -jnp.inf))
            m_b = scores * jnp.exp2(jnp.where(causal, col_b - cs_t[ha + 1:ha + 2, :], -jnp.inf))
            first = lane < P
            col2 = jnp.where(first, col_a, col_b)
            dt2 = jnp.where(first, dt[:, ha:ha + 1], dt[:, ha + 1:ha + 2])
            last2 = jnp.where(first, cs_last[:, ha:ha + 1], cs_last[:, ha + 1:ha + 2])
            xs2 = y_scr[:, cols]
            xdt = xs2 * dt2
            xdt_b = xdt.astype(BF16)
            zero = jnp.zeros_like(xdt_b)
            x_diag = jnp.concatenate([jnp.where(first, xdt_b, zero), jnp.where(first, zero, xdt_b)], axis=0)
            yd = _dot(jnp.concatenate([m_a.astype(BF16), m_b.astype(BF16)], axis=1), x_diag)
            y_scr[:, cols] = yd + jnp.exp2(col2) * yo[:, 2 * pr * P:(2 * pr + 2) * P] + dsk_ref[:, cols] * xs2
            xde_scr[:, cols] = (xdt * jnp.exp2(last2 - col2)).astype(BF16)
        st = _dot_tn(bg, xde_scr[:, g * gw:(g + 1) * gw])
        h_scr[g] = dec_e[:, g * gw:(g + 1) * gw] * hg + st

    y = y_scr[:, 0:B_INNER] * _silu(z_ref[...])
    yb_ref[...] = _rms(y, bn_ref[...]).astype(BF16)
    st_ref[...] = h_scr[...]


def _ssd_prompt(proj, cw, cb, dtb128, alog128, dsk_e, bnorm, e_mat):
    S = proj.shape[0]
    gw = (B_HEADS // B_GROUPS) * B_HEADDIM
    return pl.pallas_call(
        _ssd_prompt_kernel,
        grid=(S // CHUNK,),
        in_specs=[
            pl.BlockSpec((CHUNK, B_CONV_DIM), lambda i: (i, COL_XBC // B_CONV_DIM)),
            pl.BlockSpec((CHUNK, B_INNER), lambda i: (i, COL_Z // B_INNER)),
            pl.BlockSpec((CHUNK, 128), lambda i: (i, COL_DT // 128)),
            pl.BlockSpec((B_CONV, B_CONV_DIM), lambda i: (0, 0)),
            pl.BlockSpec((1, B_CONV_DIM), lambda i: (0, 0)),
            pl.BlockSpec((1, 128), lambda i: (0, 0)),
            pl.BlockSpec((1, 128), lambda i: (0, 0)),
            pl.BlockSpec((1, B_INNER), lambda i: (0, 0)),
            pl.BlockSpec((1, B_INNER), lambda i: (0, 0)),
            pl.BlockSpec((128, B_INNER), lambda i: (0, 0)),
        ],
        out_specs=[
            pl.BlockSpec((CHUNK, B_INNER), lambda i: (i, 0)),
            pl.BlockSpec((B_GROUPS, B_STATE, gw), lambda i: (0, 0, 0)),
        ],
        out_shape=[
            jax.ShapeDtypeStruct((S, B_INNER), BF16),
            jax.ShapeDtypeStruct((B_GROUPS, B_STATE, gw), F32),
        ],
        scratch_shapes=[
            pltpu.VMEM((CHUNK + 8, B_CONV_DIM), F32),
            pltpu.VMEM((B_GROUPS, B_STATE, gw), F32),
            pltpu.VMEM((CHUNK, B_CONV_DIM), F32),
            pltpu.VMEM((CHUNK, B_INNER), BF16),
        ],
        compiler_params=_cparams(("arbitrary",)),
        name="ssd_prompt",
    )(proj, proj, proj, cw, cb, dtb128, alog128, dsk_e, bnorm, e_mat)


def _ssd_sample_kernel(xbc_ref, cprev_ref, z_ref, dt_ref, h0_ref, cw_ref, cb_ref, dtb_ref, alog_ref,
                       dsk_ref, bn_ref, e_ref, bm_ref, s_ref,
                       yb_ref, hn_ref, xp_scr, r16_scr, prod_scr, ex_scr, a_scr, b_scr, *, T):
    H, P, N, G = B_HEADS, B_HEADDIM, B_STATE, B_GROUPS
    rep = H // G
    pairs = [(t, s) for t in range(T) for s in range(t + 1)]
    xp_scr[0:B_CONV - 1, :] = cprev_ref[0]
    xp_scr[B_CONV - 1:B_CONV - 1 + T, :] = xbc_ref[0]
    acc = xp_scr[0:T, :] * cw_ref[0:1, :]
    for k in range(1, B_CONV):
        acc = acc + xp_scr[k:k + T, :] * cw_ref[k:k + 1, :]
    xc = _silu(acc + cb_ref[...])
    xs = xc[:, 0:B_INNER]
    bm = xc[:, B_INNER:B_INNER + G * N]
    cm = xc[:, B_INNER + G * N:]

    lane = lax.broadcasted_iota(jnp.int32, (1, 128), 1)
    a_row = jnp.where(lane < H, -jnp.exp(alog_ref[...]), 0.0)
    dt = _softplus(dt_ref[0] + dtb_ref[...])
    da = dt * a_row
    cs_rows = [da[0:1, :]]
    for t in range(1, T):
        cs_rows.append(cs_rows[-1] + da[t:t + 1, :])
    cs_last = cs_rows[-1]

    prod_scr[...] = jnp.zeros_like(prod_scr)
    for idx, (t, s) in enumerate(pairs):
        prod_scr[idx:idx + 1, :] = cm[t:t + 1, :] * bm[s:s + 1, :]
    gh = jnp.dot(prod_scr[...], s_ref[...], preferred_element_type=F32, precision=HIGHEST)
    ex_scr[...] = jnp.zeros_like(ex_scr)
    ex_scr[0:T, :] = dt
    for t in range(T):
        ex_scr[T + t:T + t + 1, :] = jnp.exp(cs_rows[t])
    for idx, (t, s) in enumerate(pairs):
        ex_scr[2 * T + idx:2 * T + idx + 1, :] = gh[idx:idx + 1, :] * jnp.exp(cs_rows[t] - cs_rows[s])
    ex = jnp.dot(ex_scr[...], e_ref[...], preferred_element_type=F32, precision=HIGHEST)
    xdt = xs * ex[0:T, :]

    for g in range(G):
        r16_scr[g * T:(g + 1) * T, :] = cm[:, g * N:(g + 1) * N]
    h2d = h0_ref[0].reshape(H * P, N)
    r = _dot_nt(r16_scr[...].astype(BF16), h2d.astype(BF16))
    gw = rep * P
    y_rows = []
    for t in range(T):
        yo = jnp.concatenate([r[g * T + t:g * T + t + 1, g * gw:(g + 1) * gw] for g in range(G)], axis=1)
        y_rows.append(yo * ex[T + t:T + t + 1, :])
    for idx, (t, s) in enumerate(pairs):
        y_rows[t] = y_rows[t] + ex[2 * T + idx:2 * T + idx + 1, :] * xdt[s:s + 1, :]
    for t in range(T):
        y = y_rows[t] + dsk_ref[...] * xs[t:t + 1, :]
        y = y * _silu(z_ref[0, t:t + 1, :])
        yb_ref[0, t:t + 1, :] = _rms(y, bn_ref[...]).astype(BF16)

    eye = (lax.broadcasted_iota(jnp.int32, (H, 128), 0) == lax.broadcasted_iota(jnp.int32, (H, 128), 1)).astype(F32)

    def to_col(v):
        return jnp.sum(jnp.broadcast_to(v, (H, 128)) * eye, axis=1, keepdims=True)

    a_scr[...] = jnp.zeros_like(a_scr)
    b_scr[...] = jnp.zeros_like(b_scr)
    for t in range(T):
        dcol = to_col(jnp.exp(cs_last - cs_rows[t]))
        a_scr[t * H:(t + 1) * H, :] = jnp.broadcast_to(xdt[t:t + 1, :], (H, H * P)) * bm_ref[...]
        for g in range(G):
            b_scr[t * H + g * rep:t * H + (g + 1) * rep, :] = (
                jnp.broadcast_to(bm[t:t + 1, g * N:(g + 1) * N], (rep, N)) * dcol[g * rep:(g + 1) * rep, :])
    st = _dot_tn(a_scr[...].astype(BF16), b_scr[...].astype(BF16))
    dfull = jnp.broadcast_to(to_col(jnp.exp(cs_last)), (H, N))
    for h in range(H):
        hn_ref[0, h] = dfull[h:h + 1, :] * h0_ref[0, h] + st[h * P:(h + 1) * P, :]


def _ssd_sample(xbc, cprev, z, dtb, h0, layer, cw, cb, dtb128, alog128, dsk_e, bnorm, e_mat, blockmask, s_mat):
    B, T, _ = xbc.shape
    n_pair = -(-(T * (T + 1) // 2) // 8) * 8
    assert B_CONV - 1 + T <= 8 and T * B_HEADS <= 128
    kern = functools.partial(_ssd_sample_kernel, T=T)
    c2 = lambda b: (0, 0)
    return pl.pallas_call(
        kern,
        grid=(B,),
        in_specs=[
            pl.BlockSpec((1, T, B_CONV_DIM), lambda b: (b, 0, 0)),
            pl.BlockSpec((None, 1, B_CONV - 1, B_CONV_DIM), lambda b: (layer, b, 0, 0)),
            pl.BlockSpec((1, T, B_INNER), lambda b: (b, 0, 0)),
            pl.BlockSpec((1, T, 128), lambda b: (b, 0, 0)),
            pl.BlockSpec((None, 1, B_HEADS, B_HEADDIM, B_STATE), lambda b: (layer, b, 0, 0, 0)),
            pl.BlockSpec((B_CONV, B_CONV_DIM), c2),
            pl.BlockSpec((1, B_CONV_DIM), c2),
            pl.BlockSpec((1, 128), c2),
            pl.BlockSpec((1, 128), c2),
            pl.BlockSpec((1, B_INNER), c2),
            pl.BlockSpec((1, B_INNER), c2),
            pl.BlockSpec((128, B_INNER), c2),
            pl.BlockSpec((B_HEADS, B_INNER), c2),
            pl.BlockSpec((B_GROUPS * B_STATE, 128), c2),
        ],
        out_specs=[
            pl.BlockSpec((1, T, B_INNER), lambda b: (b, 0, 0)),
            pl.BlockSpec((1, B_HEADS, B_HEADDIM, B_STATE), lambda b: (b, 0, 0, 0)),
        ],
        out_shape=[
            jax.ShapeDtypeStruct((B, T, B_INNER), BF16),
            jax.ShapeDtypeStruct((B, B_HEADS, B_HEADDIM, B_STATE), F32),
        ],
        scratch_shapes=[
            pltpu.VMEM((8, B_CONV_DIM), F32),
            pltpu.VMEM((B_GROUPS * T, B_STATE), F32),
            pltpu.VMEM((n_pair, B_GROUPS * B_STATE), F32),
            pltpu.VMEM((2 * T + n_pair, 128), F32),
            pltpu.VMEM((128, B_INNER), F32),
            pltpu.VMEM((128, B_STATE), F32),
        ],
        compiler_params=_cparams(("parallel",)),
        name="ssd_sample",
    )(xbc, cprev, z, dtb, h0, cw, cb, dtb128, alog128, dsk_e, bnorm, e_mat, blockmask, s_mat)


def _c_prep_kernel(cq_ref, ckv_ref, kr_ref, krs_ref, cos_ref, sin_ref, qn_ref, kvn_ref,
                   wq1_ref, wq2_ref, wk_ref, wvt_ref,
                   q_ref, k_ref, vt_ref, ckvn_ref, krope_ref):
    cos = cos_ref[...]
    sin = sin_ref[...]
    cos8 = jnp.concatenate([cos] * C_HEADS, axis=1)
    sin8 = jnp.concatenate([sin] * C_HEADS, axis=1)
    cqn = _rms(cq_ref[...], qn_ref[...]).astype(BF16)
    q = _dot(cqn, wq1_ref[...]) * cos8 + _dot(cqn, wq2_ref[...]) * sin8
    q_ref[...] = (q * (C_SCALE * LOG2E)).astype(BF16)
    ckvn = _rms(ckv_ref[...], kvn_ref[...])
    ckvn_ref[...] = ckvn
    k128 = kr_ref[...] * cos + krs_ref[...] * sin
    krope_ref[...] = k128[:, C_NOPE:C_NOPE + C_ROPE]
    cb = ckvn.astype(BF16)
    k_ref[...] = (_dot(cb, wk_ref[...]) + jnp.concatenate([k128] * C_HEADS, axis=1)).astype(BF16)
    row = lax.broadcasted_iota(jnp.int32, (V_ROWS, 1), 0)
    for h in range(C_HEADS):
        vt = _dot_nt(wvt_ref[h], cb)
        vt_ref[h * V_ROWS:(h + 1) * V_ROWS, :] = jnp.where(row == C_V, 1.0, vt).astype(BF16)


def _c_prep(proj, cos, sin, qn, kvn, wq1, wq2, wk, wv, tm):
    n_tok = proj.shape[0]
    c2 = lambda i: (0, 0)
    hq = C_HEADS * HEAD_PAD
    return pl.pallas_call(
        _c_prep_kernel,
        grid=(n_tok // tm,),
        in_specs=[
            pl.BlockSpec((tm, C_Q_LORA), lambda i: (i, COL_CQ // C_Q_LORA)),
            pl.BlockSpec((tm, C_KV_LORA), lambda i: (i, COL_CKV // C_KV_LORA)),
            pl.BlockSpec((tm, 128), lambda i: (i, COL_KR // 128)),
            pl.BlockSpec((tm, 128), lambda i: (i, COL_KRS // 128)),
            pl.BlockSpec((tm, 128), lambda i: (i, 0)),
            pl.BlockSpec((tm, 128), lambda i: (i, 0)),
            pl.BlockSpec((1, C_Q_LORA), c2),
            pl.BlockSpec((1, C_KV_LORA), c2),
            pl.BlockSpec((C_Q_LORA, hq), c2),
            pl.BlockSpec((C_Q_LORA, hq), c2),
            pl.BlockSpec((C_KV_LORA, hq), c2),
            pl.BlockSpec((C_HEADS, V_ROWS, C_KV_LORA), lambda i: (0, 0, 0)),
        ],
        out_specs=[
            pl.BlockSpec((tm, hq), lambda i: (i, 0)),
            pl.BlockSpec((tm, hq), lambda i: (i, 0)),
            pl.BlockSpec((C_HEADS * V_ROWS, tm), lambda i: (0, i)),
            pl.BlockSpec((tm, C_KV_LORA), lambda i: (i, 0)),
            pl.BlockSpec((tm, C_ROPE), lambda i: (i, 0)),
        ],
        out_shape=[
            jax.ShapeDtypeStruct((n_tok, hq), BF16),
            jax.ShapeDtypeStruct((n_tok, hq), BF16),
            jax.ShapeDtypeStruct((C_HEADS * V_ROWS, n_tok), BF16),
            jax.ShapeDtypeStruct((n_tok, C_KV_LORA), F32),
            jax.ShapeDtypeStruct((n_tok, C_ROPE), F32),
        ],
        compiler_params=_cparams(("parallel",)),
        name="c_prep",
    )(proj, proj, proj, proj, cos, sin, qn, kvn, wq1, wq2, wk, wv)


def _flash_kernel(qi_ref, kj_ref, q_ref, k_ref, vt_ref, o_ref, m_scr, acc_scr, s_scr, *, tq, tk, hps):
    s_idx = pl.program_id(1)
    qi = qi_ref[s_idx]
    kj = kj_ref[s_idx]

    @pl.when(kj == 0)
    def _():
        m_scr[...] = jnp.full_like(m_scr, -jnp.inf)
        acc_scr[...] = jnp.zeros_like(acc_scr)

    def step(masked):
        if masked:
            visible = (lax.broadcasted_iota(jnp.int32, (tk, tq), 0)
                       <= lax.broadcasted_iota(jnp.int32, (tk, tq), 1))

        def scores(hh):
            lanes = slice(hh * HEAD_PAD, (hh + 1) * HEAD_PAD)
            s = _dot_nt(k_ref[:, lanes], q_ref[:, lanes])
            if masked:
                s = jnp.where(visible, s, -jnp.inf)
            s_scr[hh % (FLASH_LOOKAHEAD + 1)] = s

        for hh in range(min(FLASH_LOOKAHEAD, hps)):
            scores(hh)
        for hh in range(hps):
            if hh + FLASH_LOOKAHEAD < hps:
                scores(hh + FLASH_LOOKAHEAD)
            s_tile = s_scr.at[hh % (FLASH_LOOKAHEAD + 1)]
            m_prev = m_scr[hh]
            m_new = jnp.maximum(m_prev, jnp.max(s_tile[...], axis=0, keepdims=True))
            m_scr[hh] = m_new
            alpha = jnp.exp2(m_prev[0:1, :] - m_new[0:1, :])
            p = jnp.exp2(s_tile[...] - m_new[0:1, :]).astype(BF16)
            acc_scr[hh] = alpha * acc_scr[hh] + _dot(vt_ref[hh * V_ROWS:(hh + 1) * V_ROWS, :], p)

    @pl.when(kj < qi)
    def _():
        step(False)

    @pl.when(kj == qi)
    def _():
        step(True)
        for hh in range(hps):
            a = acc_scr[hh]
            o_ref[hh * C_V:(hh + 1) * C_V, :] = (a[0:C_V, :] / a[C_V:C_V + 1, :]).astype(BF16)


def _flash(q, k, vt, tq, hps):
    S = q.shape[0]
    tk = tq
    nq = S // tq
    qi = np.concatenate([np.full(i + 1, i, np.int32) for i in range(nq)])
    kj = np.concatenate([np.arange(i + 1, dtype=np.int32) for i in range(nq)])
    kern = functools.partial(_flash_kernel, tq=tq, tk=tk, hps=hps)
    grid_spec = pltpu.PrefetchScalarGridSpec(
        num_scalar_prefetch=2,
        grid=(C_HEADS // hps, int(qi.shape[0])),
        in_specs=[
            pl.BlockSpec((tq, hps * HEAD_PAD), lambda p, s, qi, kj: (qi[s], p)),
            pl.BlockSpec((tk, hps * HEAD_PAD), lambda p, s, qi, kj: (kj[s], p)),
            pl.BlockSpec((hps * V_ROWS, tk), lambda p, s, qi, kj: (p, kj[s])),
        ],
        out_specs=pl.BlockSpec((hps * C_V, tq), lambda p, s, qi, kj: (p, qi[s])),
        scratch_shapes=[
            pltpu.VMEM((hps, 8, tq), F32),
            pltpu.VMEM((hps, V_ROWS, tq), F32),
            pltpu.VMEM((FLASH_LOOKAHEAD + 1, tk, tq), F32),
        ],
    )
    return pl.pallas_call(
        kern,
        grid_spec=grid_spec,
        out_shape=jax.ShapeDtypeStruct((C_HEADS * C_V, S), BF16),
        compiler_params=_cparams(("parallel", "arbitrary")),
        name="flash_prompt",
    )(jnp.asarray(qi), jnp.asarray(kj), q, k, vt)


def _q_lat_kernel(q_ref, wt_ref, o_ref):
    o_ref[0] = _dot(q_ref[...], wt_ref[0]).astype(BF16)


def _q_lat(q, wukt):
    n = q.shape[0]
    return pl.pallas_call(
        _q_lat_kernel,
        grid=(C_HEADS,),
        in_specs=[
            pl.BlockSpec((n, HEAD_PAD), lambda h: (0, h)),
            pl.BlockSpec((1, HEAD_PAD, C_KV_LORA), lambda h: (h, 0, 0)),
        ],
        out_specs=pl.BlockSpec((1, n, C_KV_LORA), lambda h: (h, 0, 0)),
        out_shape=jax.ShapeDtypeStruct((C_HEADS, n, C_KV_LORA), BF16),
        compiler_params=_cparams(("parallel",)),
        name="q_lat",
    )(q, wukt)


def _attn_sample_kernel(pt_ref, qlat_ref, q128_ref, ckv_ref, krn_ref, kv_hbm, kr_hbm, o_ref,
                        kvbuf, krbuf, sem, newkv_scr, newkr_scr, *, layer, T, n_pages, cp, n_streams):
    b = pl.program_id(0)
    nb = pl.num_programs(0)
    n_chunks = n_pages // cp
    R = T * C_HEADS
    spp = cp // n_streams
    n_slots = ATTN_SLOTS
    ahead = n_slots - 1

    def copies(bb, c):
        slot = c % n_slots
        out = []
        for i in range(cp):
            page = pt_ref[bb, c * cp + i]
            out.append(pltpu.make_async_copy(kv_hbm.at[layer, page], kvbuf.at[slot, i], sem.at[0, slot]))
            out.append(pltpu.make_async_copy(kr_hbm.at[layer, page], krbuf.at[slot, i], sem.at[1, slot]))
        return out

    def start(bb, c):
        for cpy in copies(bb, c):
            cpy.start()

    def wait(bb, c):
        for cpy in copies(bb, c):
            cpy.wait()

    @pl.when(b == 0)
    def _():
        for c in range(ahead):
            start(0, c)

    b_next = jnp.minimum(b + 1, nb - 1)

    qlat = qlat_ref[0]
    qr = q128_ref[0][:, C_NOPE:C_NOPE + C_ROPE]

    def softmax_update(state, s):
        m_prev, l_prev, _ = state
        m_new = jnp.maximum(m_prev, jnp.max(s, axis=-1, keepdims=True))
        alpha = jnp.exp2(m_prev - m_new)
        p = jnp.exp2(s - m_new)
        l_new = alpha * l_prev + jnp.sum(p, axis=-1, keepdims=True)
        return m_new, l_new, alpha, p.astype(BF16)

    def online(state, s, kv):
        m_new, l_new, alpha, p = softmax_update(state, s)
        return m_new, l_new, state[2] * alpha + _dot(p, kv)

    def chunk(c, carry):
        slot = c % n_slots
        wait(b, c)
        if c + ahead < n_chunks:
            start(b, c + ahead)
        else:
            start(b_next, c + ahead - n_chunks)
        kvs, scores = [], []
        for si in range(n_streams):
            kv = kvbuf[slot, si * spp:(si + 1) * spp].reshape(spp * PAGE_SIZE, C_KV_LORA).astype(BF16)
            kr_t = jnp.concatenate([krbuf[slot, si * spp + i] for i in range(spp)], axis=1).astype(BF16)
            kvs.append(kv)
            scores.append(_dot_nt(qlat, kv) + _dot(qr, kr_t))
        stats = [softmax_update(carry[si], scores[si]) for si in range(n_streams)]
        return tuple((m_new, l_new, carry[si][2] * alpha + _dot(p, kvs[si]))
                     for si, (m_new, l_new, alpha, p) in enumerate(stats))

    streams = tuple((jnp.full((R, 1), -jnp.inf, F32), jnp.zeros((R, 1), F32), jnp.zeros((R, C_KV_LORA), F32))
                    for _ in range(n_streams))
    for c in range(n_chunks):
        streams = chunk(c, streams)

    @pl.when(b == nb - 1)
    def _():
        for c in range(ahead):
            wait(b_next, c)

    newkv_scr[...] = jnp.zeros_like(newkv_scr)
    newkr_scr[...] = jnp.zeros_like(newkr_scr)
    newkv_scr[0:T, :] = ckv_ref[0]
    newkr_scr[0:T, :] = krn_ref[0]
    kvn = newkv_scr[...].astype(BF16)
    krn = newkr_scr[...].astype(BF16)
    s = _dot_nt(qlat, kvn) + _dot_nt(qr, krn)
    t_row = lax.broadcasted_iota(jnp.int32, (R, 128), 0) % T
    key = lax.broadcasted_iota(jnp.int32, (R, 128), 1)
    s = jnp.where(key <= t_row, s, -jnp.inf)
    m_all, l_all, acc_all = online(streams[0], s, kvn)
    for m_i, l_i, acc_i in streams[1:]:
        m_new = jnp.maximum(m_all, m_i)
        wa = jnp.exp2(m_all - m_new)
        wi = jnp.exp2(m_i - m_new)
        l_all = wa * l_all + wi * l_i
        acc_all = wa * acc_all + wi * acc_i
        m_all = m_new
    o_ref[0] = (acc_all / l_all).astype(BF16)


def _attn_sample(page_table, qlat, q128, ckv_new, kr_new, cache_kv, cache_kr_t, layer, cp, n_streams):
    B, R, _ = qlat.shape
    T = ckv_new.shape[1]
    n_pages = page_table.shape[1]
    assert (n_pages // cp) % ATTN_SLOTS == 0 and cp % n_streams == 0
    kern = functools.partial(_attn_sample_kernel, layer=layer, T=T, n_pages=n_pages, cp=cp, n_streams=n_streams)
    grid_spec = pltpu.PrefetchScalarGridSpec(
        num_scalar_prefetch=1,
        grid=(B,),
        in_specs=[
            pl.BlockSpec((1, R, C_KV_LORA), lambda b, pt: (b, 0, 0)),
            pl.BlockSpec((1, R, HEAD_PAD), lambda b, pt: (b, 0, 0)),
            pl.BlockSpec((1, T, C_KV_LORA), lambda b, pt: (b, 0, 0)),
            pl.BlockSpec((1, T, C_ROPE), lambda b, pt: (b, 0, 0)),
            pl.BlockSpec(memory_space=pl.ANY),
            pl.BlockSpec(memory_space=pl.ANY),
        ],
        out_specs=pl.BlockSpec((1, R, C_KV_LORA), lambda b, pt: (b, 0, 0)),
        scratch_shapes=[
            pltpu.VMEM((ATTN_SLOTS, cp, PAGE_SIZE, C_KV_LORA), F32),
            pltpu.VMEM((ATTN_SLOTS, cp, C_ROPE, PAGE_SIZE), F32),
            pltpu.SemaphoreType.DMA((2, ATTN_SLOTS)),
            pltpu.VMEM((128, C_KV_LORA), F32),
            pltpu.VMEM((128, C_ROPE), F32),
        ],
    )
    return pl.pallas_call(
        kern,
        grid_spec=grid_spec,
        out_shape=jax.ShapeDtypeStruct((B, R, C_KV_LORA), BF16),
        compiler_params=_cparams(("arbitrary",)),
        name="attn_sample",
    )(page_table, qlat, q128, ckv_new, kr_new, cache_kv, cache_kr_t)


def _uv_proj_kernel(o_ref, w_ref, y_ref):
    y_ref[0] = _dot(o_ref[0], w_ref[0]).astype(BF16)


def _uv_proj(olat, wuv):
    H, n, _ = olat.shape
    return pl.pallas_call(
        _uv_proj_kernel,
        grid=(H,),
        in_specs=[
            pl.BlockSpec((1, n, C_KV_LORA), lambda h: (h, 0, 0)),
            pl.BlockSpec((1, C_KV_LORA, C_V), lambda h: (h, 0, 0)),
        ],
        out_specs=pl.BlockSpec((1, n, C_V), lambda h: (h, 0, 0)),
        out_shape=jax.ShapeDtypeStruct((H, n, C_V), BF16),
        compiler_params=_cparams(("parallel",)),
        name="uv_proj",
    )(olat, wuv)


def _merge_kernel(x_ref, g0_ref, g1_ref, g2_ref, ya_ref, yb_ref, yc_ref, wpa_ref, wpb_ref, wpc_ref, wo_ref, o_ref):
    m = _sigmoid(g0_ref[...]) * _dot(ya_ref[...], wpa_ref[...])
    m = m + _sigmoid(g1_ref[...]) * _dot(yb_ref[...], wpb_ref[...])
    m = m + _sigmoid(g2_ref[...]) * _dot(yc_ref[...], wpc_ref[...])
    o_ref[...] = x_ref[...] + _dot(m.astype(BF16), wo_ref[...])


def _merge(x, proj, ya, yb, yc, wpa, wpb, wpc, wo, tm):
    n_tok = x.shape[0]
    c2 = lambda i: (0, 0)
    g0 = COL_GATE // D_MODEL
    return pl.pallas_call(
        _merge_kernel,
        grid=(n_tok // tm,),
        in_specs=[
            pl.BlockSpec((tm, D_MODEL), lambda i: (i, 0)),
            pl.BlockSpec((tm, D_MODEL), lambda i: (i, g0)),
            pl.BlockSpec((tm, D_MODEL), lambda i: (i, g0 + 1)),
            pl.BlockSpec((tm, D_MODEL), lambda i: (i, g0 + 2)),
            pl.BlockSpec((tm, A_WIDTH), lambda i: (i, 0)),
            pl.BlockSpec((tm, B_INNER), lambda i: (i, 0)),
            pl.BlockSpec((tm, C_HEADS * C_V), lambda i: (i, 0)),
            pl.BlockSpec((A_WIDTH, D_MODEL), c2),
            pl.BlockSpec((B_INNER, D_MODEL), c2),
            pl.BlockSpec((C_HEADS * C_V, D_MODEL), c2),
            pl.BlockSpec((D_MODEL, D_MODEL), c2),
        ],
        out_specs=pl.BlockSpec((tm, D_MODEL), lambda i: (i, 0)),
        out_shape=jax.ShapeDtypeStruct((n_tok, D_MODEL), F32),
        compiler_params=_cparams(("parallel",)),
        name="merge",
    )(x, proj, proj, proj, ya, yb, yc, wpa, wpb, wpc, wo)


def _ffn_kernel(x_ref, g_ref, wu_ref, wd_ref, o_ref, h_scr, acc_scr):
    j = pl.program_id(1)

    @pl.when(j == 0)
    def _():
        x = x_ref[...]
        h_scr[...] = _rms(x, g_ref[...]).astype(BF16)
        acc_scr[...] = x

    u = jnp.maximum(_dot(h_scr[...], wu_ref[...]), 0.0)
    acc_scr[...] += _dot((u * u).astype(BF16), wd_ref[...])

    @pl.when(j == pl.num_programs(1) - 1)
    def _():
        o_ref[...] = acc_scr[...]


def _ffn(x, g, wu, wd, tm, tf):
    n_tok = x.shape[0]
    return pl.pallas_call(
        _ffn_kernel,
        grid=(n_tok // tm, D_FF // tf),
        in_specs=[
            pl.BlockSpec((tm, D_MODEL), lambda i, j: (i, 0)),
            pl.BlockSpec((1, D_MODEL), lambda i, j: (0, 0)),
            pl.BlockSpec((D_MODEL, tf), lambda i, j: (0, j)),
            pl.BlockSpec((tf, D_MODEL), lambda i, j: (j, 0)),
        ],
        out_specs=pl.BlockSpec((tm, D_MODEL), lambda i, j: (i, 0)),
        out_shape=jax.ShapeDtypeStruct((n_tok, D_MODEL), F32),
        scratch_shapes=[pltpu.VMEM((tm, D_MODEL), BF16), pltpu.VMEM((tm, D_MODEL), F32)],
        compiler_params=_cparams(("parallel", "arbitrary")),
        name="ffn",
    )(x, g, wu, wd)


def _ple_kernel(x_ref, g_ref, wg_ref, p_ref, wp_ref, gf_ref, o_ref, *, final):
    x = x_ref[...]
    pg = _sigmoid(_dot(_rms(x, g_ref[...]).astype(BF16), wg_ref[...]))
    y = x + pg * _dot(p_ref[...].astype(BF16), wp_ref[...])
    if final:
        y = _rms(y, gf_ref[...])
    o_ref[...] = y


def _ple(x, g, wg, p, wp, gf, tm, final):
    n_tok = x.shape[0]
    d_ple = p.shape[1]
    c2 = lambda i: (0, 0)
    return pl.pallas_call(
        functools.partial(_ple_kernel, final=final),
        grid=(n_tok // tm,),
        in_specs=[
            pl.BlockSpec((tm, D_MODEL), lambda i: (i, 0)),
            pl.BlockSpec((1, D_MODEL), c2),
            pl.BlockSpec((D_MODEL, D_MODEL), c2),
            pl.BlockSpec((tm, d_ple), lambda i: (i, 0)),
            pl.BlockSpec((d_ple, D_MODEL), c2),
            pl.BlockSpec((1, D_MODEL), c2),
        ],
        out_specs=pl.BlockSpec((tm, D_MODEL), lambda i: (i, 0)),
        out_shape=jax.ShapeDtypeStruct((n_tok, D_MODEL), F32),
        compiler_params=_cparams(("parallel",)),
        name="ple",
    )(x, g, wg, p, wp, gf)


def _prep_layer_weights(w_in, w_uq, w_ukv):
    sizes = (N_BRANCH * D_MODEL, A_WIDTH, A_WIDTH, B_INNER, B_CONV_DIM, B_HEADS, C_Q_LORA, C_KV_LORA, C_ROPE)
    idx = [int(v) for v in np.cumsum(sizes)[:-1]]
    gates, a_u, a_v, z, xbc, dt, c_q, c_kv, k_r = jnp.split(w_in, idx, axis=1)
    half = C_ROPE // 2
    zc = lambda n: jnp.zeros((D_MODEL, n), w_in.dtype)
    k_rs = jnp.concatenate([k_r[:, half:], k_r[:, :half]], axis=1)
    w_in_r = jnp.concatenate([
        c_q, dt, zc(128 - B_HEADS), c_kv,
        zc(C_NOPE), k_r, zc(HEAD_PAD - C_NOPE - C_ROPE),
        zc(C_NOPE), k_rs, zc(HEAD_PAD - C_NOPE - C_ROPE),
        gates, xbc, z, a_u, a_v], axis=1).astype(BF16)

    uq = w_uq.reshape(C_Q_LORA, C_HEADS, C_NOPE + C_ROPE)
    uq_n, uq_r = uq[..., :C_NOPE], uq[..., C_NOPE:]
    uq_rs = jnp.concatenate([uq_r[..., half:], uq_r[..., :half]], axis=-1)
    zq = lambda n: jnp.zeros((C_Q_LORA, C_HEADS, n), w_uq.dtype)
    wq1 = jnp.concatenate([uq_n, uq_r, zq(HEAD_PAD - C_NOPE - C_ROPE)], axis=-1).reshape(C_Q_LORA, -1).astype(BF16)
    wq2 = jnp.concatenate([zq(C_NOPE), uq_rs, zq(HEAD_PAD - C_NOPE - C_ROPE)], axis=-1).reshape(C_Q_LORA, -1).astype(BF16)

    ukv = w_ukv.reshape(C_KV_LORA, C_HEADS, C_NOPE + C_V)
    uk, uv = ukv[..., :C_NOPE], ukv[..., C_NOPE:]
    wk = jnp.concatenate([uk, jnp.zeros((C_KV_LORA, C_HEADS, HEAD_PAD - C_NOPE), w_ukv.dtype)], axis=-1)
    wk = wk.reshape(C_KV_LORA, -1).astype(BF16)
    wv = jnp.concatenate([uv, jnp.zeros((C_KV_LORA, C_HEADS, V_ROWS - C_V), w_ukv.dtype)], axis=-1)
    wv = jnp.transpose(wv, (1, 2, 0)).astype(BF16)
    wukt = jnp.transpose(wk.reshape(C_KV_LORA, C_HEADS, HEAD_PAD), (1, 2, 0))
    wuv_h = jnp.transpose(uv, (1, 0, 2)).astype(BF16)
    return w_in_r, wq1, wq2, wk, wv, wukt, wuv_h


def _rope_tables(pos):
    half = C_ROPE // 2
    inv = jnp.power(ROPE_BASE, -jnp.arange(half, dtype=F32) * (2.0 / C_ROPE))
    ang = pos.astype(F32)[:, None] * inv[None, :]
    cos, sin = jnp.cos(ang), jnp.sin(ang)
    n = pos.shape[0]
    pad = jnp.zeros((n, HEAD_PAD - C_NOPE - C_ROPE), F32)
    cos_t = jnp.concatenate([jnp.ones((n, C_NOPE), F32), cos, cos, pad], axis=1)
    sin_t = jnp.concatenate([jnp.zeros((n, C_NOPE), F32), -sin, sin, pad], axis=1)
    return cos_t, sin_t


def _pad128(v):
    return jnp.concatenate([v, jnp.zeros((128 - v.shape[0],), v.dtype)])[None, :]


def _token_tile(n, pref):
    t = pref
    while n % t:
        t //= 2
    return t


def kernel(x_prompt, x_sample, cache_kv_latent, cache_k_rope, state_ssm, state_conv, page_table, p_prompt, p_sample, ln_mix, w_in, sgu_ln_w, sgu_ln_b, w_s, b_s, conv_w, conv_b, dt_bias, a_log, d_skip, b_norm, q_norm, w_uq, kv_norm, w_ukv, w_pa, w_pb, w_pc, w_o, ln_ffn, w_up, w_down, ln_ple, w_ple_gate, w_ple, ln_final):
    depth = w_in.shape[0]
    _, S, _ = x_prompt.shape
    B, T, _ = x_sample.shape
    n_pages = page_table.shape[1]
    past_len = n_pages * PAGE_SIZE
    ns = B * T
    assert x_prompt.shape[0] == 1 and S % CHUNK == 0 and ns % CHUNK == 0 and CHUNK % T == 0

    xp = x_prompt.reshape(S, D_MODEL)
    xs = x_sample.reshape(ns, D_MODEL)
    cos_p, sin_p = _rope_tables(jnp.arange(S))
    cos_s, sin_s = _rope_tables(past_len + (jnp.arange(ns) % T))

    hp = np.arange(B_INNER) // B_HEADDIM
    e_mat = jnp.asarray((np.arange(128)[:, None] == hp[None, :]).astype(np.float32))
    blockmask = jnp.asarray((np.arange(B_HEADS)[:, None] == hp[None, :]).astype(np.float32))
    gn = np.arange(B_GROUPS * B_STATE) // B_STATE
    hg = np.where(np.arange(128) < B_HEADS, np.arange(128) // (B_HEADS // B_GROUPS), -1)
    s_mat = jnp.asarray((gn[:, None] == hg[None, :]).astype(np.float32))

    tm_p = _token_tile(S, 1024)
    tm_s = _token_tile(ns, 512)
    tq = _token_tile(S, 512)
    cp = _token_tile(n_pages // ATTN_SLOTS, ATTN_PAGES_PER_CHUNK)
    cache_kr_t = jnp.swapaxes(cache_k_rope, 2, 3)
    tile_rep = CHUNK // T
    eye_rep = jnp.eye(tile_rep, dtype=F32)

    outs_p, outs_s = [], []
    for i in range(depth):
        w_in_r, wq1, wq2, wk, wv, wukt, wuv_h = _prep_layer_weights(w_in[i], w_uq[i], w_ukv[i])
        g_mix = ln_mix[i][None, :]
        lnw, lnb = sgu_ln_w[i][None, :], sgu_ln_b[i][None, :]
        ws_p = w_s[i][:, :CHUNK, :CHUNK]
        bs_p = b_s[i][:, :CHUNK, None]
        ws_t = jnp.tril(w_s[i][:, :T, :T])
        ws_s = jnp.einsum('ab,gts->gatbs', eye_rep, ws_t).reshape(A_GROUPS, CHUNK, CHUNK)
        bs_s = jnp.tile(b_s[i][:, :T], (1, tile_rep))[:, :, None]
        cw, cb = conv_w[i], conv_b[i][None, :]
        dtb128, alog128 = _pad128(dt_bias[i]), _pad128(a_log[i])
        dsk_e = jnp.repeat(d_skip[i], B_HEADDIM)[None, :]
        bn = b_norm[i][None, :]
        qn, kvn = q_norm[i][None, :], kv_norm[i][None, :]
        wpa, wpb, wpc, wo = (w.astype(BF16) for w in (w_pa[i], w_pb[i], w_pc[i], w_o[i]))
        wu, wd = w_up[i].astype(BF16), w_down[i].astype(BF16)
        wg, wp = w_ple_gate[i].astype(BF16), w_ple[i].astype(BF16)
        g_ffn, g_ple, g_fin = ln_ffn[i][None, :], ln_ple[i][None, :], ln_final[None, :]
        final = i == depth - 1

        proj = _in_proj(xp, g_mix, w_in_r, tm_p, 1024)
        ya, av = _gate_a(proj, lnw, lnb, ws_p, bs_p, _token_tile(S, 512))
        yb, ssm_t = _ssd_prompt(proj, cw, cb, dtb128, alog128, dsk_e, bn, e_mat)
        ssm_p = jnp.transpose(ssm_t.reshape(B_GROUPS, B_STATE, B_HEADS // B_GROUPS, B_HEADDIM),
                              (0, 2, 3, 1)).reshape(1, B_HEADS, B_HEADDIM, B_STATE)
        q, k, v, ckvn, krope = _c_prep(proj, cos_p, sin_p, qn, kvn, wq1, wq2, wk, wv, _token_tile(S, 512))
        yc = _flash(q, k, v, tq, FLASH_HEADS_PER_STEP).T
        xp = _merge(xp, proj, ya, yb, yc, wpa, wpb, wpc, wo, _token_tile(S, 512))
        xp = _ffn(xp, g_ffn, wu, wd, tm_p, 1024)
        xp = _ple(xp, g_ple, wg, p_prompt[i].reshape(S, -1), wp, g_fin, _token_tile(S, 512), final)
        outs_p.append((ckvn.reshape(1, S, C_KV_LORA), krope.reshape(1, S, C_ROPE),
                       ssm_p,
                       proj[S - (B_CONV - 1):, COL_XBC:COL_XBC + B_CONV_DIM][None],
                       av[S - CHUNK:][None]))

        proj = _in_proj(xs, g_mix, w_in_r, tm_s, 1024)
        ya, av = _gate_a(proj, lnw, lnb, ws_s, bs_s, tm_s)
        xbc_s = proj[:, COL_XBC:COL_XBC + B_CONV_DIM].reshape(B, T, B_CONV_DIM)
        z_s = proj[:, COL_Z:COL_Z + B_INNER].reshape(B, T, B_INNER)
        dt_s = proj[:, COL_DT:COL_DT + 128].reshape(B, T, 128)
        yb, ssm_new = _ssd_sample(xbc_s, state_conv, z_s, dt_s, state_ssm, i, cw, cb, dtb128, alog128,
                                  dsk_e, bn, e_mat, blockmask, s_mat)
        q, _, _, ckvn, krope = _c_prep(proj, cos_s, sin_s, qn, kvn, wq1, wq2, wk, wv, tm_s)
        qlat = _q_lat(q, wukt)
        qlat = jnp.transpose(qlat.reshape(C_HEADS, B, T, C_KV_LORA), (1, 0, 2, 3)).reshape(B, C_HEADS * T, C_KV_LORA)
        q128 = jnp.transpose(q.reshape(B, T, C_HEADS, HEAD_PAD), (0, 2, 1, 3)).reshape(B, C_HEADS * T, HEAD_PAD)
        olat = _attn_sample(page_table, qlat, q128, ckvn.reshape(B, T, C_KV_LORA), krope.reshape(B, T, C_ROPE),
                            cache_kv_latent, cache_kr_t, i, cp, min(ATTN_STREAMS, cp))
        olat = jnp.transpose(olat.reshape(B, C_HEADS, T, C_KV_LORA), (1, 0, 2, 3)).reshape(C_HEADS, ns, C_KV_LORA)
        yc = jnp.transpose(_uv_proj(olat, wuv_h), (1, 0, 2)).reshape(ns, C_HEADS * C_V)
        xs = _merge(xs, proj, ya, yb.reshape(ns, B_INNER), yc, wpa, wpb, wpc, wo, tm_s)
        xs = _ffn(xs, g_ffn, wu, wd, tm_s, 1024)
        xs = _ple(xs, g_ple, wg, p_sample[i].reshape(ns, -1), wp, g_fin, tm_s, final)
        outs_s.append((ckvn.reshape(B, T, C_KV_LORA), krope.reshape(B, T, C_ROPE), ssm_new,
                       xbc_s[:, T - (B_CONV - 1):], av.reshape(B, T, A_WIDTH)))

    kv_p, kr_p, ssm_p, conv_p, v_p = [jnp.stack(t) for t in zip(*outs_p)]
    kv_s, kr_s, ssm_s, conv_s, v_s = [jnp.stack(t) for t in zip(*outs_s)]
    return (xp.reshape(1, S, D_MODEL), xs.reshape(B, T, D_MODEL), kv_p, kr_p, ssm_p, conv_p, v_p,
            kv_s, kr_s, ssm_s, conv_s, v_s)
```

```python
import functools
import math

import numpy as np
import jax
import jax.numpy as jnp
from jax import lax
from jax.experimental import pallas as pl
from jax.experimental.pallas import tpu as pltpu

F32 = jnp.float32
BF16 = jnp.bfloat16

NORM_EPS = 1e-6
D_MODEL = 1024
N_BRANCH = 3
A_WIDTH = 512
A_GROUPS = 4
CHUNK = 128
B_INNER = 1024
B_HEADDIM = 64
B_HEADS = 16
B_GROUPS = 4
B_STATE = 128
B_CONV = 4
B_CONV_DIM = B_INNER + 2 * B_GROUPS * B_STATE
C_HEADS = 8
C_NOPE = 64
C_ROPE = 32
C_V = 64
C_KV_LORA = 256
C_Q_LORA = 384
ROPE_BASE = 10000.0
C_SCALE = (C_NOPE + C_ROPE) ** -0.5
LOG2E = math.log2(math.e)
FLASH_HEADS_PER_STEP = 8
FLASH_TILE = 1024
FLASH_LOOKAHEAD = 2
SSD_SAMPLE_SEQS = 4
V_ROWS = 80
ATTN_PAGES_PER_CHUNK = 32
ATTN_SLOTS = 4
ATTN_STREAMS = 4
D_FF = 4 * D_MODEL
PAGE_SIZE = 128
HEAD_PAD = 128

COL_CQ = 0
COL_DT = 384
COL_CKV = 512
COL_KR = 768
COL_KRS = 896
COL_Z = 1024
COL_XBC = 2048
COL_AU = 4096
COL_AV = 4608
D_IN_PAD = 5120

VMEM_LIMIT = 56 * 1024 * 1024


def _cparams(sem):
    return pltpu.CompilerParams(dimension_semantics=sem, vmem_limit_bytes=VMEM_LIMIT)


def _rms(x, g):
    ms = jnp.mean(x * x, axis=-1, keepdims=True)
    return x * lax.rsqrt(ms + NORM_EPS) * g


def _sigmoid(x):
    return 1.0 / (1.0 + jnp.exp(-x))


def _silu(x):
    return x * _sigmoid(x)


def _gelu(x):
    c = math.sqrt(2.0 / math.pi)
    return 0.5 * x * (1.0 + jnp.tanh(c * (x + 0.044715 * (x * x * x))))


def _softplus(x):
    return jnp.maximum(x, 0.0) + jnp.log(1.0 + jnp.exp(-jnp.abs(x)))


def _dot(a, b):
    return jnp.dot(a, b, preferred_element_type=F32)


def _dot_nt(a, b):
    return lax.dot_general(a, b, (((1,), (1,)), ((), ())), preferred_element_type=F32)


def _dot_tn(a, b):
    return lax.dot_general(a, b, (((0,), (0,)), ((), ())), preferred_element_type=F32)


def _split3(x, axis):
    hi = x.astype(BF16).astype(F32)
    r = x - hi
    mid = r.astype(BF16).astype(F32)
    return jnp.concatenate([hi, mid, r - mid], axis=axis).astype(BF16)


def _sum3(y, axis):
    n = y.shape[axis] // 3
    parts = [lax.slice_in_dim(y, k * n, (k + 1) * n, axis=axis) for k in range(3)]
    return parts[0] + parts[1] + parts[2]


def _in_proj_kernel(x_ref, g_ref, w_ref, o_ref, h_scr):
    @pl.when(pl.program_id(1) == 0)
    def _():
        h_scr[...] = _rms(x_ref[...], g_ref[...]).astype(BF16)

    o_ref[...] = _dot(h_scr[...], w_ref[...])


def _in_proj(x, g, w, tm, tn):
    n_tok = x.shape[0]
    n_out = w.shape[1]
    return pl.pallas_call(
        _in_proj_kernel,
        grid=(n_tok // tm, n_out // tn),
        in_specs=[
            pl.BlockSpec((tm, D_MODEL), lambda i, j: (i, 0)),
            pl.BlockSpec((1, D_MODEL), lambda i, j: (0, 0)),
            pl.BlockSpec((D_MODEL, tn), lambda i, j: (0, j)),
        ],
        out_specs=pl.BlockSpec((tm, tn), lambda i, j: (i, j)),
        out_shape=jax.ShapeDtypeStruct((n_tok, n_out), F32),
        scratch_shapes=[pltpu.VMEM((tm, D_MODEL), BF16)],
        compiler_params=_cparams(("parallel", "arbitrary")),
        name="in_proj",
    )(x, g, w)


def _gate_a_kernel(u_ref, v_ref, lnw_ref, lnb_ref, ws_ref, bs_ref, ya_ref, av_ref, *, n_chunks):
    row = lax.broadcasted_iota(jnp.int32, (CHUNK, CHUNK), 0)
    col = lax.broadcasted_iota(jnp.int32, (CHUNK, CHUNK), 1)
    causal = col <= row
    gd = A_WIDTH // A_GROUPS
    for c in range(n_chunks):
        rows = pl.ds(c * CHUNK, CHUNK)
        v = _gelu(v_ref[rows, :])
        mu = jnp.mean(v, axis=-1, keepdims=True)
        vc = v - mu
        var = jnp.mean(vc * vc, axis=-1, keepdims=True)
        av = vc * lax.rsqrt(var + NORM_EPS) * lnw_ref[...] + lnb_ref[...]
        av_ref[rows, :] = av
        u = _gelu(u_ref[rows, :])
        for g in range(A_GROUPS):
            w = jnp.where(causal, ws_ref[g], 0.0).astype(BF16)
            s = _dot(w, av[:, g * gd:(g + 1) * gd].astype(BF16)) + bs_ref[g]
            ya_ref[rows, g * gd:(g + 1) * gd] = (u[:, g * gd:(g + 1) * gd] * s).astype(BF16)


def _gate_a(proj, lnw, lnb, ws, bs, tm):
    n_tok = proj.shape[0]
    kern = functools.partial(_gate_a_kernel, n_chunks=tm // CHUNK)
    return pl.pallas_call(
        kern,
        grid=(n_tok // tm,),
        in_specs=[
            pl.BlockSpec((tm, A_WIDTH), lambda i: (i, COL_AU // A_WIDTH)),
            pl.BlockSpec((tm, A_WIDTH), lambda i: (i, COL_AV // A_WIDTH)),
            pl.BlockSpec((1, A_WIDTH), lambda i: (0, 0)),
            pl.BlockSpec((1, A_WIDTH), lambda i: (0, 0)),
            pl.BlockSpec((A_GROUPS, CHUNK, CHUNK), lambda i: (0, 0, 0)),
            pl.BlockSpec((A_GROUPS, CHUNK, 1), lambda i: (0, 0, 0)),
        ],
        out_specs=[
            pl.BlockSpec((tm, A_WIDTH), lambda i: (i, 0)),
            pl.BlockSpec((tm, A_WIDTH), lambda i: (i, 0)),
        ],
        out_shape=[
            jax.ShapeDtypeStruct((n_tok, A_WIDTH), BF16),
            jax.ShapeDtypeStruct((n_tok, A_WIDTH), F32),
        ],
        compiler_params=_cparams(("parallel",)),
        name="gate_a",
    )(proj, proj, lnw, lnb, ws, bs)


def _ssd_prompt_kernel(xbc_ref, z_ref, dt_ref, cw_ref, cb_ref, dtb_ref, alog_ref, dsk_ref, bn_ref, e_ref,
                       yb_ref, st_ref, xp_scr, h_scr, y_scr, xde_scr):
    L, P, N = CHUNK, B_HEADDIM, B_STATE
    i = pl.program_id(0)

    @pl.when(i == 0)
    def _():
        xp_scr[0:8, :] = jnp.zeros((8, B_CONV_DIM), F32)
        h_scr[...] = jnp.zeros_like(h_scr)

    xp_scr[8:8 + L, :] = xbc_ref[...]
    acc = xp_scr[8:8 + L, :] * cw_ref[B_CONV - 1:B_CONV, :]
    for sh in range(1, B_CONV):
        acc = acc + xp_scr[8 - sh:8 - sh + L, :] * cw_ref[B_CONV - 1 - sh:B_CONV - sh, :]
    y_scr[...] = _silu(acc + cb_ref[...])
    xp_scr[0:8, :] = xp_scr[L:L + 8, :]

    lane = lax.broadcasted_iota(jnp.int32, (1, 128), 1)
    a_row = jnp.where(lane < B_HEADS, -jnp.exp(alog_ref[...]), 0.0)
    dt = _softplus(dt_ref[...] + dtb_ref[...])
    da = dt * a_row
    row = lax.broadcasted_iota(jnp.int32, (L, L), 0)
    col = lax.broadcasted_iota(jnp.int32, (L, L), 1)
    causal = col <= row
    da3 = _split3(da, 1)
    cs = _sum3(_dot(jnp.where(causal, 1.0, 0.0).astype(BF16), da3), 1) * LOG2E
    cs_t = _sum3(_dot_tn(da3, jnp.where(row <= col, 1.0, 0.0).astype(BF16)), 0) * LOG2E
    cs_last = cs[L - 1:L, :]
    dec_e = _sum3(_dot(_split3(jnp.broadcast_to(jnp.exp2(cs_last), (8, 128)), 0), e_ref[...]), 0)[0:1, :]

    rep = B_HEADS // B_GROUPS
    gw = rep * P
    for g in range(B_GROUPS):
        bg = y_scr[:, B_INNER + g * N:B_INNER + (g + 1) * N].astype(BF16)
        cg = y_scr[:, B_INNER + (B_GROUPS + g) * N:B_INNER + (B_GROUPS + g + 1) * N].astype(BF16)
        scores = _dot_nt(cg, bg)
        hg = h_scr[g]
        yo = _dot(cg, hg.astype(BF16))
        for pr in range(rep // 2):
            ha = g * rep + 2 * pr
            cols = slice(ha * P, (ha + 2) * P)
            col_a = jnp.broadcast_to(cs[:, ha:ha + 1], (L, L))
            col_b = jnp.broadcast_to(cs[:, ha + 1:ha + 2], (L, L))
            m_a = scores * jnp.exp2(jnp.where(causal, col_a - cs_t[ha:ha + 1, :], -jnp.inf))
            m_b = scores * jnp.exp2(jnp.where(causal, col_b - cs_t[ha + 1:ha + 2, :], -jnp.inf))
            first = lane < P
            col2 = jnp.where(first, col_a, col_b)
            dt2 = jnp.where(first, dt[:, ha:ha + 1], dt[:, ha + 1:ha + 2])
            last2 = jnp.where(first, cs_last[:, ha:ha + 1], cs_last[:, ha + 1:ha + 2])
            xs2 = y_scr[:, cols]
            xdt = xs2 * dt2
            xdt_b = xdt.astype(BF16)
            zero = jnp.zeros_like(xdt_b)
            x_diag = jnp.concatenate([jnp.where(first, xdt_b, zero), jnp.where(first, zero, xdt_b)], axis=0)
            yd = _dot(jnp.concatenate([m_a.astype(BF16), m_b.astype(BF16)], axis=1), x_diag)
            y_scr[:, cols] = yd + jnp.exp2(col2) * yo[:, 2 * pr * P:(2 * pr + 2) * P] + dsk_ref[:, cols] * xs2
            xde_scr[:, cols] = (xdt * jnp.exp2(last2 - col2)).astype(BF16)
        st = _dot_tn(bg, xde_scr[:, g * gw:(g + 1) * gw])
        h_scr[g] = dec_e[:, g * gw:(g + 1) * gw] * hg + st

    y = y_scr[:, 0:B_INNER] * _silu(z_ref[...])
    yb_ref[...] = _rms(y, bn_ref[...]).astype(BF16)
    st_ref[...] = h_scr[...]


def _ssd_prompt(proj, cw, cb, dtb128, alog128, dsk_e, bnorm, e_mat):
    S = proj.shape[0]
    gw = (B_HEADS // B_GROUPS) * B_HEADDIM
    return pl.pallas_call(
        _ssd_prompt_kernel,
        grid=(S // CHUNK,),
        in_specs=[
            pl.BlockSpec((CHUNK, B_CONV_DIM), lambda i: (i, COL_XBC // B_CONV_DIM)),
            pl.BlockSpec((CHUNK, B_INNER), lambda i: (i, COL_Z // B_INNER)),
            pl.BlockSpec((CHUNK, 128), lambda i: (i, COL_DT // 128)),
            pl.BlockSpec((B_CONV, B_CONV_DIM), lambda i: (0, 0)),
            pl.BlockSpec((1, B_CONV_DIM), lambda i: (0, 0)),
            pl.BlockSpec((1, 128), lambda i: (0, 0)),
            pl.BlockSpec((1, 128), lambda i: (0, 0)),
            pl.BlockSpec((1, B_INNER), lambda i: (0, 0)),
            pl.BlockSpec((1, B_INNER), lambda i: (0, 0)),
            pl.BlockSpec((128, B_INNER), lambda i: (0, 0)),
        ],
        out_specs=[
            pl.BlockSpec((CHUNK, B_INNER), lambda i: (i, 0)),
            pl.BlockSpec((B_GROUPS, B_STATE, gw), lambda i: (0, 0, 0)),
        ],
        out_shape=[
            jax.ShapeDtypeStruct((S, B_INNER), BF16),
            jax.ShapeDtypeStruct((B_GROUPS, B_STATE, gw), F32),
        ],
        scratch_shapes=[
            pltpu.VMEM((CHUNK + 8, B_CONV_DIM), F32),
            pltpu.VMEM((B_GROUPS, B_STATE, gw), F32),
            pltpu.VMEM((CHUNK, B_CONV_DIM), F32),
            pltpu.VMEM((CHUNK, B_INNER), BF16),
        ],
        compiler_params=_cparams(("arbitrary",)),
        name="ssd_prompt",
    )(proj, proj, proj, cw, cb, dtb128, alog128, dsk_e, bnorm, e_mat)


def _ssd_sample_kernel(xbc_ref, cprev_ref, z_ref, dt_ref, h0_ref, cw_ref, cb_ref, dtb_ref, alog_ref,
                       dsk_ref, bn_ref, e_ref, bm_ref, s_ref,
                       yb_ref, hn_ref, xp_scr, r16_scr, prod_scr, ex_scr, a_scr, b_scr, *, T, seqs):
    for e in range(seqs):
        _ssd_sample_one(e, xbc_ref, cprev_ref, z_ref, dt_ref, h0_ref, cw_ref, cb_ref, dtb_ref, alog_ref,
                        dsk_ref, bn_ref, e_ref, bm_ref, s_ref, yb_ref, hn_ref,
                        xp_scr.at[e], r16_scr.at[e], prod_scr.at[e], ex_scr.at[e], a_scr.at[e], b_scr.at[e], T)


def _ssd_sample_one(e, xbc_ref, cprev_ref, z_ref, dt_ref, h0_ref, cw_ref, cb_ref, dtb_ref, alog_ref,
                    dsk_ref, bn_ref, e_ref, bm_ref, s_ref,
                    yb_ref, hn_ref, xp_scr, r16_scr, prod_scr, ex_scr, a_scr, b_scr, T):
    H, P, N, G = B_HEADS, B_HEADDIM, B_STATE, B_GROUPS
    rep = H // G
    pairs = [(t, s) for t in range(T) for s in range(t + 1)]
    xp_scr[0:B_CONV - 1, :] = cprev_ref[e]
    xp_scr[B_CONV - 1:B_CONV - 1 + T, :] = xbc_ref[e]
    acc = xp_scr[0:T, :] * cw_ref[0:1, :]
    for k in range(1, B_CONV):
        acc = acc + xp_scr[k:k + T, :] * cw_ref[k:k + 1, :]
    xc = _silu(acc + cb_ref[...])
    xs = xc[:, 0:B_INNER]
    bm = xc[:, B_INNER:B_INNER + G * N]
    cm = xc[:, B_INNER + G * N:]

    lane = lax.broadcasted_iota(jnp.int32, (1, 128), 1)
    a_row = jnp.where(lane < H, -jnp.exp(alog_ref[...]), 0.0)
    dt = _softplus(dt_ref[e] + dtb_ref[...])
    da = dt * a_row
    cs_rows = [da[0:1, :]]
    for t in range(1, T):
        cs_rows.append(cs_rows[-1] + da[t:t + 1, :])
    cs_last = cs_rows[-1]

    prod_scr[...] = jnp.zeros_like(prod_scr)
    for idx, (t, s) in enumerate(pairs):
        prod_scr[idx:idx + 1, :] = cm[t:t + 1, :] * bm[s:s + 1, :]
    gh = _sum3(_dot(_split3(prod_scr[...], 0), s_ref[...]), 0)
    ex_scr[...] = jnp.zeros_like(ex_scr)
    ex_scr[0:T, :] = dt
    for t in range(T):
        ex_scr[T + t:T + t + 1, :] = jnp.exp(cs_rows[t])
    for idx, (t, s) in enumerate(pairs):
        ex_scr[2 * T + idx:2 * T + idx + 1, :] = gh[idx:idx + 1, :] * jnp.exp(cs_rows[t] - cs_rows[s])
    ex = _sum3(_dot(_split3(ex_scr[...], 0), e_ref[...]), 0)
    xdt = xs * ex[0:T, :]

    for g in range(G):
        r16_scr[g * T:(g + 1) * T, :] = cm[:, g * N:(g + 1) * N]
    h2d = h0_ref[e].reshape(H * P, N)
    r = _dot_nt(r16_scr[...].astype(BF16), h2d.astype(BF16))
    gw = rep * P
    y_rows = []
    for t in range(T):
        yo = jnp.concatenate([r[g * T + t:g * T + t + 1, g * gw:(g + 1) * gw] for g in range(G)], axis=1)
        y_rows.append(yo * ex[T + t:T + t + 1, :])
    for idx, (t, s) in enumerate(pairs):
        y_rows[t] = y_rows[t] + ex[2 * T + idx:2 * T + idx + 1, :] * xdt[s:s + 1, :]
    for t in range(T):
        y = y_rows[t] + dsk_ref[...] * xs[t:t + 1, :]
        y = y * _silu(z_ref[e, t:t + 1, :])
        yb_ref[e, t:t + 1, :] = _rms(y, bn_ref[...]).astype(BF16)

    eye = (lax.broadcasted_iota(jnp.int32, (H, 128), 0) == lax.broadcasted_iota(jnp.int32, (H, 128), 1)).astype(F32)

    def to_col(v):
        return jnp.sum(jnp.broadcast_to(v, (H, 128)) * eye, axis=1, keepdims=True)

    a_scr[...] = jnp.zeros_like(a_scr)
    b_scr[...] = jnp.zeros_like(b_scr)
    for t in range(T):
        dcol = to_col(jnp.exp(cs_last - cs_rows[t]))
        a_scr[t * H:(t + 1) * H, :] = jnp.broadcast_to(xdt[t:t + 1, :], (H, H * P)) * bm_ref[...]
        for g in range(G):
            b_scr[t * H + g * rep:t * H + (g + 1) * rep, :] = (
                jnp.broadcast_to(bm[t:t + 1, g * N:(g + 1) * N], (rep, N)) * dcol[g * rep:(g + 1) * rep, :])
    st = _dot_tn(a_scr[...].astype(BF16), b_scr[...].astype(BF16))
    dfull = jnp.broadcast_to(to_col(jnp.exp(cs_last)), (H, N))
    for h in range(H):
        hn_ref[e, h] = dfull[h:h + 1, :] * h0_ref[e, h] + st[h * P:(h + 1) * P, :]


def _ssd_sample(xbc, cprev, z, dtb, h0, layer, cw, cb, dtb128, alog128, dsk_e, bnorm, e_mat, blockmask, s_mat):
    B, T, _ = xbc.shape
    n_pair = -(-(T * (T + 1) // 2) // 8) * 8
    assert B_CONV - 1 + T <= 8 and T * B_HEADS <= 128
    nseq = SSD_SAMPLE_SEQS if B % SSD_SAMPLE_SEQS == 0 else 1
    kern = functools.partial(_ssd_sample_kernel, T=T, seqs=nseq)
    c2 = lambda b: (0, 0)
    return pl.pallas_call(
        kern,
        grid=(B // nseq,),
        in_specs=[
            pl.BlockSpec((nseq, T, B_CONV_DIM), lambda b: (b, 0, 0)),
            pl.BlockSpec((None, nseq, B_CONV - 1, B_CONV_DIM), lambda b: (layer, b, 0, 0)),
            pl.BlockSpec((nseq, T, B_INNER), lambda b: (b, 0, 0)),
            pl.BlockSpec((nseq, T, 128), lambda b: (b, 0, 0)),
            pl.BlockSpec((None, nseq, B_HEADS, B_HEADDIM, B_STATE), lambda b: (layer, b, 0, 0, 0)),
            pl.BlockSpec((B_CONV, B_CONV_DIM), c2),
            pl.BlockSpec((1, B_CONV_DIM), c2),
            pl.BlockSpec((1, 128), c2),
            pl.BlockSpec((1, 128), c2),
            pl.BlockSpec((1, B_INNER), c2),
            pl.BlockSpec((1, B_INNER), c2),
            pl.BlockSpec((128, B_INNER), c2),
            pl.BlockSpec((B_HEADS, B_INNER), c2),
            pl.BlockSpec((B_GROUPS * B_STATE, 128), c2),
        ],
        out_specs=[
            pl.BlockSpec((nseq, T, B_INNER), lambda b: (b, 0, 0)),
            pl.BlockSpec((nseq, B_HEADS, B_HEADDIM, B_STATE), lambda b: (b, 0, 0, 0)),
        ],
        out_shape=[
            jax.ShapeDtypeStruct((B, T, B_INNER), BF16),
            jax.ShapeDtypeStruct((B, B_HEADS, B_HEADDIM, B_STATE), F32),
        ],
        scratch_shapes=[
            pltpu.VMEM((nseq, 8, B_CONV_DIM), F32),
            pltpu.VMEM((nseq, B_GROUPS * T, B_STATE), F32),
            pltpu.VMEM((nseq, n_pair, B_GROUPS * B_STATE), F32),
            pltpu.VMEM((nseq, 2 * T + n_pair, 128), F32),
            pltpu.VMEM((nseq, 128, B_INNER), F32),
            pltpu.VMEM((nseq, 128, B_STATE), F32),
        ],
        compiler_params=_cparams(("parallel",)),
        name="ssd_sample",
    )(xbc, cprev, z, dtb, h0, cw, cb, dtb128, alog128, dsk_e, bnorm, e_mat, blockmask, s_mat)


def _c_prep_kernel(cq_ref, ckv_ref, kr_ref, krs_ref, cos_ref, sin_ref, qn_ref, kvn_ref,
                   wq1_ref, wq2_ref, wk_ref, wvt_ref,
                   q_ref, k_ref, vt_ref, ckvn_ref, krope_ref):
    cos = cos_ref[...]
    sin = sin_ref[...]
    cos8 = jnp.concatenate([cos] * C_HEADS, axis=1)
    sin8 = jnp.concatenate([sin] * C_HEADS, axis=1)
    cqn = _rms(cq_ref[...], qn_ref[...]).astype(BF16)
    q = _dot(cqn, wq1_ref[...]) * cos8 + _dot(cqn, wq2_ref[...]) * sin8
    q_ref[...] = (q * (C_SCALE * LOG2E)).astype(BF16)
    ckvn = _rms(ckv_ref[...], kvn_ref[...])
    ckvn_ref[...] = ckvn
    k128 = kr_ref[...] * cos + krs_ref[...] * sin
    krope_ref[...] = k128[:, C_NOPE:C_NOPE + C_ROPE]
    cb = ckvn.astype(BF16)
    k_ref[...] = (_dot(cb, wk_ref[...]) + jnp.concatenate([k128] * C_HEADS, axis=1)).astype(BF16)
    row = lax.broadcasted_iota(jnp.int32, (V_ROWS, 1), 0)
    for h in range(C_HEADS):
        vt = _dot_nt(wvt_ref[h], cb)
        vt_ref[h * V_ROWS:(h + 1) * V_ROWS, :] = jnp.where(row == C_V, 1.0, vt).astype(BF16)


def _c_prep(proj, cos, sin, qn, kvn, wq1, wq2, wk, wv, tm):
    n_tok = proj.shape[0]
    c2 = lambda i: (0, 0)
    hq = C_HEADS * HEAD_PAD
    return pl.pallas_call(
        _c_prep_kernel,
        grid=(n_tok // tm,),
        in_specs=[
            pl.BlockSpec((tm, C_Q_LORA), lambda i: (i, COL_CQ // C_Q_LORA)),
            pl.BlockSpec((tm, C_KV_LORA), lambda i: (i, COL_CKV // C_KV_LORA)),
            pl.BlockSpec((tm, 128), lambda i: (i, COL_KR // 128)),
            pl.BlockSpec((tm, 128), lambda i: (i, COL_KRS // 128)),
            pl.BlockSpec((tm, 128), lambda i: (i, 0)),
            pl.BlockSpec((tm, 128), lambda i: (i, 0)),
            pl.BlockSpec((1, C_Q_LORA), c2),
            pl.BlockSpec((1, C_KV_LORA), c2),
            pl.BlockSpec((C_Q_LORA, hq), c2),
            pl.BlockSpec((C_Q_LORA, hq), c2),
            pl.BlockSpec((C_KV_LORA, hq), c2),
            pl.BlockSpec((C_HEADS, V_ROWS, C_KV_LORA), lambda i: (0, 0, 0)),
        ],
        out_specs=[
            pl.BlockSpec((tm, hq), lambda i: (i, 0)),
            pl.BlockSpec((tm, hq), lambda i: (i, 0)),
            pl.BlockSpec((C_HEADS * V_ROWS, tm), lambda i: (0, i)),
            pl.BlockSpec((tm, C_KV_LORA), lambda i: (i, 0)),
            pl.BlockSpec((tm, C_ROPE), lambda i: (i, 0)),
        ],
        out_shape=[
            jax.ShapeDtypeStruct((n_tok, hq), BF16),
            jax.ShapeDtypeStruct((n_tok, hq), BF16),
            jax.ShapeDtypeStruct((C_HEADS * V_ROWS, n_tok), BF16),
            jax.ShapeDtypeStruct((n_tok, C_KV_LORA), F32),
            jax.ShapeDtypeStruct((n_tok, C_ROPE), F32),
        ],
        compiler_params=_cparams(("parallel",)),
        name="c_prep",
    )(proj, proj, proj, proj, cos, sin, qn, kvn, wq1, wq2, wk, wv)


def _flash_kernel(qi_ref, kj_ref, q_ref, k_ref, vt_ref, o_ref, m_scr, acc_scr, s_scr, *, tq, tk, hps):
    s_idx = pl.program_id(1)
    qi = qi_ref[s_idx]
    kj = kj_ref[s_idx]

    @pl.when(kj == 0)
    def _():
        m_scr[...] = jnp.full_like(m_scr, -jnp.inf)
        acc_scr[...] = jnp.zeros_like(acc_scr)

    def step(masked):
        if masked:
            visible = (lax.broadcasted_iota(jnp.int32, (tk, tq), 0)
                       <= lax.broadcasted_iota(jnp.int32, (tk, tq), 1))

        def scores(hh):
            lanes = slice(hh * HEAD_PAD, (hh + 1) * HEAD_PAD)
            s = _dot_nt(k_ref[:, lanes], q_ref[:, lanes])
            if masked:
                s = jnp.where(visible, s, -jnp.inf)
            s_scr[hh % (FLASH_LOOKAHEAD + 1)] = s

        for hh in range(min(FLASH_LOOKAHEAD, hps)):
            scores(hh)
        for hh in range(hps):
            if hh + FLASH_LOOKAHEAD < hps:
                scores(hh + FLASH_LOOKAHEAD)
            s_tile = s_scr.at[hh % (FLASH_LOOKAHEAD + 1)]
            m_prev = m_scr[hh]
            m_new = jnp.maximum(m_prev, jnp.max(s_tile[...], axis=0, keepdims=True))
            m_scr[hh] = m_new
            alpha = jnp.exp2(m_prev[0:1, :] - m_new[0:1, :])
            p = jnp.exp2(s_tile[...] - m_new[0:1, :]).astype(BF16)
            acc_scr[hh] = alpha * acc_scr[hh] + _dot(vt_ref[hh * V_ROWS:(hh + 1) * V_ROWS, :], p)

    @pl.when(kj < qi)
    def _():
        step(False)

    @pl.when(kj == qi)
    def _():
        step(True)
        for hh in range(hps):
            a = acc_scr[hh]
            o_ref[hh * C_V:(hh + 1) * C_V, :] = (a[0:C_V, :] / a[C_V:C_V + 1, :]).astype(BF16)


def _flash(q, k, vt, tq, hps):
    S = q.shape[0]
    tk = tq
    nq = S // tq
    qi = np.concatenate([np.full(i + 1, i, np.int32) for i in range(nq)])
    kj = np.concatenate([np.arange(i + 1, dtype=np.int32) for i in range(nq)])
    kern = functools.partial(_flash_kernel, tq=tq, tk=tk, hps=hps)
    grid_spec = pltpu.PrefetchScalarGridSpec(
        num_scalar_prefetch=2,
        grid=(C_HEADS // hps, int(qi.shape[0])),
        in_specs=[
            pl.BlockSpec((tq, hps * HEAD_PAD), lambda p, s, qi, kj: (qi[s], p)),
            pl.BlockSpec((tk, hps * HEAD_PAD), lambda p, s, qi, kj: (kj[s], p)),
            pl.BlockSpec((hps * V_ROWS, tk), lambda p, s, qi, kj: (p, kj[s])),
        ],
        out_specs=pl.BlockSpec((hps * C_V, tq), lambda p, s, qi, kj: (p, qi[s])),
        scratch_shapes=[
            pltpu.VMEM((hps, 8, tq), F32),
            pltpu.VMEM((hps, V_ROWS, tq), F32),
            pltpu.VMEM((FLASH_LOOKAHEAD + 1, tk, tq), F32),
        ],
    )
    return pl.pallas_call(
        kern,
        grid_spec=grid_spec,
        out_shape=jax.ShapeDtypeStruct((C_HEADS * C_V, S), BF16),
        compiler_params=_cparams(("parallel", "arbitrary")),
        name="flash_prompt",
    )(jnp.asarray(qi), jnp.asarray(kj), q, k, vt)


def _q_lat_kernel(q_ref, wt_ref, o_ref):
    o_ref[0] = _dot(q_ref[...], wt_ref[0]).astype(BF16)


def _q_lat(q, wukt):
    n = q.shape[0]
    return pl.pallas_call(
        _q_lat_kernel,
        grid=(C_HEADS,),
        in_specs=[
            pl.BlockSpec((n, HEAD_PAD), lambda h: (0, h)),
            pl.BlockSpec((1, HEAD_PAD, C_KV_LORA), lambda h: (h, 0, 0)),
        ],
        out_specs=pl.BlockSpec((1, n, C_KV_LORA), lambda h: (h, 0, 0)),
        out_shape=jax.ShapeDtypeStruct((C_HEADS, n, C_KV_LORA), BF16),
        compiler_params=_cparams(("parallel",)),
        name="q_lat",
    )(q, wukt)


def _attn_sample_kernel(pt_ref, qlat_ref, q128_ref, ckv_ref, krn_ref, kv_hbm, kr_hbm, o_ref,
                        kvbuf, krbuf, sem, newkv_scr, newkr_scr, *, layer, T, n_pages, cp, n_streams):
    b = pl.program_id(0)
    nb = pl.num_programs(0)
    n_chunks = n_pages // cp
    R = T * C_HEADS
    spp = cp // n_streams
    n_slots = ATTN_SLOTS
    ahead = n_slots - 1

    def copies(bb, c):
        slot = c % n_slots
        out = []
        for i in range(cp):
            page = pt_ref[bb, c * cp + i]
            out.append(pltpu.make_async_copy(kv_hbm.at[layer, page], kvbuf.at[slot, i], sem.at[0, slot]))
            out.append(pltpu.make_async_copy(kr_hbm.at[layer, page], krbuf.at[slot, i], sem.at[1, slot]))
        return out

    def start(bb, c):
        for cpy in copies(bb, c):
            cpy.start()

    def wait(bb, c):
        for cpy in copies(bb, c):
            cpy.wait()

    @pl.when(b == 0)
    def _():
        for c in range(ahead):
            start(0, c)

    b_next = jnp.minimum(b + 1, nb - 1)

    qlat = qlat_ref[0]
    qr = q128_ref[0][:, C_NOPE:C_NOPE + C_ROPE]

    def softmax_update(state, s):
        m_prev, l_prev, _ = state
        m_new = jnp.maximum(m_prev, jnp.max(s, axis=-1, keepdims=True))
        alpha = jnp.exp2(m_prev - m_new)
        p = jnp.exp2(s - m_new)
        l_new = alpha * l_prev + jnp.sum(p, axis=-1, keepdims=True)
        return m_new, l_new, alpha, p.astype(BF16)

    def online(state, s, kv):
        m_new, l_new, alpha, p = softmax_update(state, s)
        return m_new, l_new, state[2] * alpha + _dot(p, kv)

    def chunk(c, carry):
        slot = c % n_slots
        wait(b, c)
        if c + ahead < n_chunks:
            start(b, c + ahead)
        else:
            start(b_next, c + ahead - n_chunks)
        kvs, scores = [], []
        for si in range(n_streams):
            kv = kvbuf[slot, si * spp:(si + 1) * spp].reshape(spp * PAGE_SIZE, C_KV_LORA).astype(BF16)
            kr_t = jnp.concatenate([krbuf[slot, si * spp + i] for i in range(spp)], axis=1).astype(BF16)
            kvs.append(kv)
            scores.append(_dot_nt(qlat, kv) + _dot(qr, kr_t))
        stats = [softmax_update(carry[si], scores[si]) for si in range(n_streams)]
        return tuple((m_new, l_new, carry[si][2] * alpha + _dot(p, kvs[si]))
                     for si, (m_new, l_new, alpha, p) in enumerate(stats))

    streams = tuple((jnp.full((R, 1), -jnp.inf, F32), jnp.zeros((R, 1), F32), jnp.zeros((R, C_KV_LORA), F32))
                    for _ in range(n_streams))
    for c in range(n_chunks):
        streams = chunk(c, streams)

    @pl.when(b == nb - 1)
    def _():
        for c in range(ahead):
            wait(b_next, c)

    newkv_scr[...] = jnp.zeros_like(newkv_scr)
    newkr_scr[...] = jnp.zeros_like(newkr_scr)
    newkv_scr[0:T, :] = ckv_ref[0]
    newkr_scr[0:T, :] = krn_ref[0]
    kvn = newkv_scr[...].astype(BF16)
    krn = newkr_scr[...].astype(BF16)
    s = _dot_nt(qlat, kvn) + _dot_nt(qr, krn)
    t_row = lax.broadcasted_iota(jnp.int32, (R, 128), 0) % T
    key = lax.broadcasted_iota(jnp.int32, (R, 128), 1)
    s = jnp.where(key <= t_row, s, -jnp.inf)
    m_all, l_all, acc_all = online(streams[0], s, kvn)
    for m_i, l_i, acc_i in streams[1:]:
        m_new = jnp.maximum(m_all, m_i)
        wa = jnp.exp2(m_all - m_new)
        wi = jnp.exp2(m_i - m_new)
        l_all = wa * l_all + wi * l_i
        acc_all = wa * acc_all + wi * acc_i
        m_all = m_new
    o_ref[0] = (acc_all / l_all).astype(BF16)


def _attn_sample(page_table, qlat, q128, ckv_new, kr_new, cache_kv, cache_kr_t, layer, cp, n_streams):
    B, R, _ = qlat.shape
    T = ckv_new.shape[1]
    n_pages = page_table.shape[1]
    assert (n_pages // cp) % ATTN_SLOTS == 0 and cp % n_streams == 0
    kern = functools.partial(_attn_sample_kernel, layer=layer, T=T, n_pages=n_pages, cp=cp, n_streams=n_streams)
    grid_spec = pltpu.PrefetchScalarGridSpec(
        num_scalar_prefetch=1,
        grid=(B,),
        in_specs=[
            pl.BlockSpec((1, R, C_KV_LORA), lambda b, pt: (b, 0, 0)),
            pl.BlockSpec((1, R, HEAD_PAD), lambda b, pt: (b, 0, 0)),
            pl.BlockSpec((1, T, C_KV_LORA), lambda b, pt: (b, 0, 0)),
            pl.BlockSpec((1, T, C_ROPE), lambda b, pt: (b, 0, 0)),
            pl.BlockSpec(memory_space=pl.ANY),
            pl.BlockSpec(memory_space=pl.ANY),
        ],
        out_specs=pl.BlockSpec((1, R, C_KV_LORA), lambda b, pt: (b, 0, 0)),
        scratch_shapes=[
            pltpu.VMEM((ATTN_SLOTS, cp, PAGE_SIZE, C_KV_LORA), F32),
            pltpu.VMEM((ATTN_SLOTS, cp, C_ROPE, PAGE_SIZE), F32),
            pltpu.SemaphoreType.DMA((2, ATTN_SLOTS)),
            pltpu.VMEM((128, C_KV_LORA), F32),
            pltpu.VMEM((128, C_ROPE), F32),
        ],
    )
    return pl.pallas_call(
        kern,
        grid_spec=grid_spec,
        out_shape=jax.ShapeDtypeStruct((B, R, C_KV_LORA), BF16),
        compiler_params=_cparams(("arbitrary",)),
        name="attn_sample",
    )(page_table, qlat, q128, ckv_new, kr_new, cache_kv, cache_kr_t)


def _uv_proj_kernel(o_ref, wt_ref, y_ref):
    y_ref[0] = _dot_nt(wt_ref[0], o_ref[0]).astype(BF16)


def _uv_proj(olat, wuvt):
    H, n, _ = olat.shape
    return pl.pallas_call(
        _uv_proj_kernel,
        grid=(H,),
        in_specs=[
            pl.BlockSpec((1, n, C_KV_LORA), lambda h: (h, 0, 0)),
            pl.BlockSpec((1, C_V, C_KV_LORA), lambda h: (h, 0, 0)),
        ],
        out_specs=pl.BlockSpec((1, C_V, n), lambda h: (h, 0, 0)),
        out_shape=jax.ShapeDtypeStruct((H, C_V, n), BF16),
        compiler_params=_cparams(("parallel",)),
        name="uv_proj",
    )(olat, wuvt)


def _merge_kernel(x_ref, g_ref, wg_ref, ya_ref, yb_ref, yct_ref, wpa_ref, wpb_ref, wpc_ref, wo_ref, o_ref):
    x = x_ref[...]
    gates = _dot(_rms(x, g_ref[...]).astype(BF16), wg_ref[...])
    m = _sigmoid(gates[:, 0:D_MODEL]) * _dot(ya_ref[...], wpa_ref[...])
    m = m + _sigmoid(gates[:, D_MODEL:2 * D_MODEL]) * _dot(yb_ref[...], wpb_ref[...])
    m = m + _sigmoid(gates[:, 2 * D_MODEL:]) * _dot_tn(yct_ref[...], wpc_ref[...])
    o_ref[...] = x + _dot(m.astype(BF16), wo_ref[...])


def _merge(x, g_mix, w_gate, ya, yb, yct, wpa, wpb, wpc, wo, tm):
    n_tok = x.shape[0]
    c2 = lambda i: (0, 0)
    return pl.pallas_call(
        _merge_kernel,
        grid=(n_tok // tm,),
        in_specs=[
            pl.BlockSpec((tm, D_MODEL), lambda i: (i, 0)),
            pl.BlockSpec((1, D_MODEL), c2),
            pl.BlockSpec((D_MODEL, N_BRANCH * D_MODEL), c2),
            pl.BlockSpec((tm, A_WIDTH), lambda i: (i, 0)),
            pl.BlockSpec((tm, B_INNER), lambda i: (i, 0)),
            pl.BlockSpec((C_HEADS * C_V, tm), lambda i: (0, i)),
            pl.BlockSpec((A_WIDTH, D_MODEL), c2),
            pl.BlockSpec((B_INNER, D_MODEL), c2),
            pl.BlockSpec((C_HEADS * C_V, D_MODEL), c2),
            pl.BlockSpec((D_MODEL, D_MODEL), c2),
        ],
        out_specs=pl.BlockSpec((tm, D_MODEL), lambda i: (i, 0)),
        out_shape=jax.ShapeDtypeStruct((n_tok, D_MODEL), F32),
        compiler_params=_cparams(("parallel",)),
        name="merge",
    )(x, g_mix, w_gate, ya, yb, yct, wpa, wpb, wpc, wo)


def _ffn_kernel(x_ref, g_ref, wu_ref, wd_ref, o_ref, h_scr, acc_scr):
    j = pl.program_id(1)

    @pl.when(j == 0)
    def _():
        x = x_ref[...]
        h_scr[...] = _rms(x, g_ref[...]).astype(BF16)
        acc_scr[...] = x

    u = jnp.maximum(_dot(h_scr[...], wu_ref[...]), 0.0)
    acc_scr[...] += _dot((u * u).astype(BF16), wd_ref[...])

    @pl.when(j == pl.num_programs(1) - 1)
    def _():
        o_ref[...] = acc_scr[...]


def _ffn(x, g, wu, wd, tm, tf):
    n_tok = x.shape[0]
    return pl.pallas_call(
        _ffn_kernel,
        grid=(n_tok // tm, D_FF // tf),
        in_specs=[
            pl.BlockSpec((tm, D_MODEL), lambda i, j: (i, 0)),
            pl.BlockSpec((1, D_MODEL), lambda i, j: (0, 0)),
            pl.BlockSpec((D_MODEL, tf), lambda i, j: (0, j)),
            pl.BlockSpec((tf, D_MODEL), lambda i, j: (j, 0)),
        ],
        out_specs=pl.BlockSpec((tm, D_MODEL), lambda i, j: (i, 0)),
        out_shape=jax.ShapeDtypeStruct((n_tok, D_MODEL), F32),
        scratch_shapes=[pltpu.VMEM((tm, D_MODEL), BF16), pltpu.VMEM((tm, D_MODEL), F32)],
        compiler_params=_cparams(("parallel", "arbitrary")),
        name="ffn",
    )(x, g, wu, wd)


def _ple_kernel(x_ref, g_ref, wg_ref, p_ref, wp_ref, gf_ref, o_ref, *, final):
    x = x_ref[...]
    pg = _sigmoid(_dot(_rms(x, g_ref[...]).astype(BF16), wg_ref[...]))
    y = x + pg * _dot(p_ref[...].astype(BF16), wp_ref[...])
    if final:
        y = _rms(y, gf_ref[...])
    o_ref[...] = y


def _ple(x, g, wg, p, layer, wp, gf, tm, final):
    n_tok = x.shape[0]
    d_ple = p.shape[2]
    c2 = lambda i: (0, 0)
    return pl.pallas_call(
        functools.partial(_ple_kernel, final=final),
        grid=(n_tok // tm,),
        in_specs=[
            pl.BlockSpec((tm, D_MODEL), lambda i: (i, 0)),
            pl.BlockSpec((1, D_MODEL), c2),
            pl.BlockSpec((D_MODEL, D_MODEL), c2),
            pl.BlockSpec((None, tm, d_ple), lambda i: (layer, i, 0)),
            pl.BlockSpec((d_ple, D_MODEL), c2),
            pl.BlockSpec((1, D_MODEL), c2),
        ],
        out_specs=pl.BlockSpec((tm, D_MODEL), lambda i: (i, 0)),
        out_shape=jax.ShapeDtypeStruct((n_tok, D_MODEL), F32),
        compiler_params=_cparams(("parallel",)),
        name="ple",
    )(x, g, wg, p, wp, gf)


def _prep_layer_weights(w_in, w_uq, w_ukv):
    sizes = (N_BRANCH * D_MODEL, A_WIDTH, A_WIDTH, B_INNER, B_CONV_DIM, B_HEADS, C_Q_LORA, C_KV_LORA, C_ROPE)
    idx = [int(v) for v in np.cumsum(sizes)[:-1]]
    gates, a_u, a_v, z, xbc, dt, c_q, c_kv, k_r = jnp.split(w_in, idx, axis=1)
    half = C_ROPE // 2
    zc = lambda n: jnp.zeros((D_MODEL, n), w_in.dtype)
    k_rs = jnp.concatenate([k_r[:, half:], k_r[:, :half]], axis=1)
    w_in_r = jnp.concatenate([
        c_q, dt, zc(128 - B_HEADS), c_kv,
        zc(C_NOPE), k_r, zc(HEAD_PAD - C_NOPE - C_ROPE),
        zc(C_NOPE), k_rs, zc(HEAD_PAD - C_NOPE - C_ROPE),
        z, xbc, a_u, a_v], axis=1).astype(BF16)
    w_gate = gates.astype(BF16)

    uq = w_uq.reshape(C_Q_LORA, C_HEADS, C_NOPE + C_ROPE)
    uq_n, uq_r = uq[..., :C_NOPE], uq[..., C_NOPE:]
    uq_rs = jnp.concatenate([uq_r[..., half:], uq_r[..., :half]], axis=-1)
    zq = lambda n: jnp.zeros((C_Q_LORA, C_HEADS, n), w_uq.dtype)
    wq1 = jnp.concatenate([uq_n, uq_r, zq(HEAD_PAD - C_NOPE - C_ROPE)], axis=-1).reshape(C_Q_LORA, -1).astype(BF16)
    wq2 = jnp.concatenate([zq(C_NOPE), uq_rs, zq(HEAD_PAD - C_NOPE - C_ROPE)], axis=-1).reshape(C_Q_LORA, -1).astype(BF16)

    ukv = w_ukv.reshape(C_KV_LORA, C_HEADS, C_NOPE + C_V)
    uk, uv = ukv[..., :C_NOPE], ukv[..., C_NOPE:]
    wk = jnp.concatenate([uk, jnp.zeros((C_KV_LORA, C_HEADS, HEAD_PAD - C_NOPE), w_ukv.dtype)], axis=-1)
    wk = wk.reshape(C_KV_LORA, -1).astype(BF16)
    wv = jnp.concatenate([uv, jnp.zeros((C_KV_LORA, C_HEADS, V_ROWS - C_V), w_ukv.dtype)], axis=-1)
    wv = jnp.transpose(wv, (1, 2, 0)).astype(BF16)
    wukt = jnp.transpose(wk.reshape(C_KV_LORA, C_HEADS, HEAD_PAD), (1, 2, 0))
    wuv_h = jnp.transpose(uv, (1, 2, 0)).astype(BF16)
    return w_in_r, w_gate, wq1, wq2, wk, wv, wukt, wuv_h


def _rope_tables(pos):
    half = C_ROPE // 2
    inv = jnp.power(ROPE_BASE, -jnp.arange(half, dtype=F32) * (2.0 / C_ROPE))
    ang = pos.astype(F32)[:, None] * inv[None, :]
    cos, sin = jnp.cos(ang), jnp.sin(ang)
    n = pos.shape[0]
    pad = jnp.zeros((n, HEAD_PAD - C_NOPE - C_ROPE), F32)
    cos_t = jnp.concatenate([jnp.ones((n, C_NOPE), F32), cos, cos, pad], axis=1)
    sin_t = jnp.concatenate([jnp.zeros((n, C_NOPE), F32), -sin, sin, pad], axis=1)
    return cos_t, sin_t


def _pad128(v):
    return jnp.concatenate([v, jnp.zeros((128 - v.shape[0],), v.dtype)])[None, :]


def _token_tile(n, pref):
    t = pref
    while n % t:
        t //= 2
    return t


def kernel(x_prompt, x_sample, cache_kv_latent, cache_k_rope, state_ssm, state_conv, page_table, p_prompt, p_sample, ln_mix, w_in, sgu_ln_w, sgu_ln_b, w_s, b_s, conv_w, conv_b, dt_bias, a_log, d_skip, b_norm, q_norm, w_uq, kv_norm, w_ukv, w_pa, w_pb, w_pc, w_o, ln_ffn, w_up, w_down, ln_ple, w_ple_gate, w_ple, ln_final):
    depth = w_in.shape[0]
    _, S, _ = x_prompt.shape
    B, T, _ = x_sample.shape
    n_pages = page_table.shape[1]
    past_len = n_pages * PAGE_SIZE
    ns = B * T
    assert x_prompt.shape[0] == 1 and S % CHUNK == 0 and ns % CHUNK == 0 and CHUNK % T == 0

    xp = x_prompt.reshape(S, D_MODEL)
    xs = x_sample.reshape(ns, D_MODEL)
    cos_p, sin_p = _rope_tables(jnp.arange(S))
    cos_s, sin_s = _rope_tables(past_len + (jnp.arange(ns) % T))

    hp = np.arange(B_INNER) // B_HEADDIM
    e_mat = jnp.asarray((np.arange(128)[:, None] == hp[None, :]).astype(np.float32)).astype(BF16)
    blockmask = jnp.asarray((np.arange(B_HEADS)[:, None] == hp[None, :]).astype(np.float32))
    gn = np.arange(B_GROUPS * B_STATE) // B_STATE
    hg = np.where(np.arange(128) < B_HEADS, np.arange(128) // (B_HEADS // B_GROUPS), -1)
    s_mat = jnp.asarray((gn[:, None] == hg[None, :]).astype(np.float32)).astype(BF16)

    tm_p = _token_tile(S, 1024)
    tm_s = _token_tile(ns, 512)
    tq = _token_tile(S, FLASH_TILE)
    cp = _token_tile(n_pages // ATTN_SLOTS, ATTN_PAGES_PER_CHUNK)
    cache_kr_t = jnp.swapaxes(cache_k_rope, 2, 3)
    tile_rep = CHUNK // T
    eye_rep = jnp.eye(tile_rep, dtype=F32)

    outs_p, outs_s = [], []
    for i in range(depth):
        w_in_r, w_gate, wq1, wq2, wk, wv, wukt, wuv_h = _prep_layer_weights(w_in[i], w_uq[i], w_ukv[i])
        g_mix = ln_mix[i][None, :]
        lnw, lnb = sgu_ln_w[i][None, :], sgu_ln_b[i][None, :]
        ws_p = w_s[i][:, :CHUNK, :CHUNK]
        bs_p = b_s[i][:, :CHUNK, None]
        ws_t = jnp.tril(w_s[i][:, :T, :T])
        ws_s = jnp.einsum('ab,gts->gatbs', eye_rep, ws_t).reshape(A_GROUPS, CHUNK, CHUNK)
        bs_s = jnp.tile(b_s[i][:, :T], (1, tile_rep))[:, :, None]
        cw, cb = conv_w[i], conv_b[i][None, :]
        dtb128, alog128 = _pad128(dt_bias[i]), _pad128(a_log[i])
        dsk_e = jnp.repeat(d_skip[i], B_HEADDIM)[None, :]
        bn = b_norm[i][None, :]
        qn, kvn = q_norm[i][None, :], kv_norm[i][None, :]
        wpa, wpb, wpc, wo = (w.astype(BF16) for w in (w_pa[i], w_pb[i], w_pc[i], w_o[i]))
        wu, wd = w_up[i].astype(BF16), w_down[i].astype(BF16)
        wg, wp = w_ple_gate[i].astype(BF16), w_ple[i].astype(BF16)
        g_ffn, g_ple, g_fin = ln_ffn[i][None, :], ln_ple[i][None, :], ln_final[None, :]
        final = i == depth - 1

        proj = _in_proj(xp, g_mix, w_in_r, tm_p, 1024)
        ya, av = _gate_a(proj, lnw, lnb, ws_p, bs_p, _token_tile(S, 512))
        yb, ssm_t = _ssd_prompt(proj, cw, cb, dtb128, alog128, dsk_e, bn, e_mat)
        ssm_p = jnp.transpose(ssm_t.reshape(B_GROUPS, B_STATE, B_HEADS // B_GROUPS, B_HEADDIM),
                              (0, 2, 3, 1)).reshape(1, B_HEADS, B_HEADDIM, B_STATE)
        q, k, v, ckvn, krope = _c_prep(proj, cos_p, sin_p, qn, kvn, wq1, wq2, wk, wv, _token_tile(S, 512))
        yc = _flash(q, k, v, tq, FLASH_HEADS_PER_STEP)
        xp = _merge(xp, g_mix, w_gate, ya, yb, yc, wpa, wpb, wpc, wo, _token_tile(S, 512))
        xp = _ffn(xp, g_ffn, wu, wd, tm_p, 1024)
        xp = _ple(xp, g_ple, wg, p_prompt.reshape(depth, S, -1), i, wp, g_fin, _token_tile(S, 512), final)
        outs_p.append((ckvn.reshape(1, S, C_KV_LORA), krope.reshape(1, S, C_ROPE),
                       ssm_p,
                       proj[S - (B_CONV - 1):, COL_XBC:COL_XBC + B_CONV_DIM][None],
                       av[S - CHUNK:][None]))

        proj = _in_proj(xs, g_mix, w_in_r, tm_s, 1024)
        ya, av = _gate_a(proj, lnw, lnb, ws_s, bs_s, tm_s)
        xbc_s = proj[:, COL_XBC:COL_XBC + B_CONV_DIM].reshape(B, T, B_CONV_DIM)
        z_s = proj[:, COL_Z:COL_Z + B_INNER].reshape(B, T, B_INNER)
        dt_s = proj[:, COL_DT:COL_DT + 128].reshape(B, T, 128)
        yb, ssm_new = _ssd_sample(xbc_s, state_conv, z_s, dt_s, state_ssm, i, cw, cb, dtb128, alog128,
                                  dsk_e, bn, e_mat, blockmask, s_mat)
        q, _, _, ckvn, krope = _c_prep(proj, cos_s, sin_s, qn, kvn, wq1, wq2, wk, wv, tm_s)
        qlat = _q_lat(q, wukt)
        qlat = jnp.transpose(qlat.reshape(C_HEADS, B, T, C_KV_LORA), (1, 0, 2, 3)).reshape(B, C_HEADS * T, C_KV_LORA)
        q128 = jnp.transpose(q.reshape(B, T, C_HEADS, HEAD_PAD), (0, 2, 1, 3)).reshape(B, C_HEADS * T, HEAD_PAD)
        olat = _attn_sample(page_table, qlat, q128, ckvn.reshape(B, T, C_KV_LORA), krope.reshape(B, T, C_ROPE),
                            cache_kv_latent, cache_kr_t, i, cp, min(ATTN_STREAMS, cp))
        olat = jnp.transpose(olat.reshape(B, C_HEADS, T, C_KV_LORA), (1, 0, 2, 3)).reshape(C_HEADS, ns, C_KV_LORA)
        yc = _uv_proj(olat, wuv_h).reshape(C_HEADS * C_V, ns)
        xs = _merge(xs, g_mix, w_gate, ya, yb.reshape(ns, B_INNER), yc, wpa, wpb, wpc, wo, tm_s)
        xs = _ffn(xs, g_ffn, wu, wd, tm_s, 1024)
        xs = _ple(xs, g_ple, wg, p_sample.reshape(depth, ns, -1), i, wp, g_fin, tm_s, final)
        outs_s.append((ckvn.reshape(B, T, C_KV_LORA), krope.reshape(B, T, C_ROPE), ssm_new,
                       xbc_s[:, T - (B_CONV - 1):], av.reshape(B, T, A_WIDTH)))

    kv_p, kr_p, ssm_p, conv_p, v_p = [jnp.stack(t) for t in zip(*outs_p)]
    kv_s, kr_s, ssm_s, conv_s, v_s = [jnp.stack(t) for t in zip(*outs_s)]
    return (xp.reshape(1, S, D_MODEL), xs.reshape(B, T, D_MODEL), kv_p, kr_p, ssm_p, conv_p, v_p,
            kv_s, kr_s, ssm_s, conv_s, v_s)
```

```python
import functools
import math

import numpy as np
import jax
import jax.numpy as jnp
from jax import lax
from jax.experimental import pallas as pl
from jax.experimental.pallas import tpu as pltpu

F32 = jnp.float32
BF16 = jnp.bfloat16

NORM_EPS = 1e-6
D_MODEL = 1024
N_BRANCH = 3
A_WIDTH = 512
A_GROUPS = 4
CHUNK = 128
B_INNER = 1024
B_HEADDIM = 64
B_HEADS = 16
B_GROUPS = 4
B_STATE = 128
B_CONV = 4
B_CONV_DIM = B_INNER + 2 * B_GROUPS * B_STATE
C_HEADS = 8
C_NOPE = 64
C_ROPE = 32
C_V = 64
C_KV_LORA = 256
C_Q_LORA = 384
ROPE_BASE = 10000.0
C_SCALE = (C_NOPE + C_ROPE) ** -0.5
LOG2E = math.log2(math.e)
FLASH_HEADS_PER_STEP = 8
FLASH_TILE = 1024
FLASH_LOOKAHEAD = 2
SSD_SAMPLE_SEQS = 4
V_ROWS = 80
ATTN_PAGES_PER_CHUNK = 32
ATTN_SLOTS = 4
ATTN_STREAMS = 4
D_FF = 4 * D_MODEL
PAGE_SIZE = 128
HEAD_PAD = 128

COL_CQ = 0
COL_DT = 384
COL_CKV = 512
COL_KR = 768
COL_KRS = 896
COL_Z = 1024
COL_XBC = 2048
COL_AU = 4096
COL_AV = 4608
D_IN_PAD = 5120

VMEM_LIMIT = 56 * 1024 * 1024


def _cparams(sem):
    return pltpu.CompilerParams(dimension_semantics=sem, vmem_limit_bytes=VMEM_LIMIT)


def _rms(x, g):
    ms = jnp.mean(x * x, axis=-1, keepdims=True)
    return x * lax.rsqrt(ms + NORM_EPS) * g


def _sigmoid(x):
    return 1.0 / (1.0 + jnp.exp(-x))


def _silu(x):
    return x * _sigmoid(x)


def _gelu(x):
    c = math.sqrt(2.0 / math.pi)
    return 0.5 * x * (1.0 + jnp.tanh(c * (x + 0.044715 * (x * x * x))))


def _softplus(x):
    return jnp.maximum(x, 0.0) + jnp.log(1.0 + jnp.exp(-jnp.abs(x)))


def _dot(a, b):
    return jnp.dot(a, b, preferred_element_type=F32)


def _dot_nt(a, b):
    return lax.dot_general(a, b, (((1,), (1,)), ((), ())), preferred_element_type=F32)


def _dot_tn(a, b):
    return lax.dot_general(a, b, (((0,), (0,)), ((), ())), preferred_element_type=F32)


def _split3(x, axis):
    hi = x.astype(BF16).astype(F32)
    r = x - hi
    mid = r.astype(BF16).astype(F32)
    return jnp.concatenate([hi, mid, r - mid], axis=axis).astype(BF16)


def _sum3(y, axis):
    n = y.shape[axis] // 3
    parts = [lax.slice_in_dim(y, k * n, (k + 1) * n, axis=axis) for k in range(3)]
    return parts[0] + parts[1] + parts[2]


def _in_proj_kernel(x_ref, g_ref, w_ref, o_ref, h_scr):
    @pl.when(pl.program_id(1) == 0)
    def _():
        h_scr[...] = _rms(x_ref[...], g_ref[...]).astype(BF16)

    o_ref[...] = _dot(h_scr[...], w_ref[...])


def _in_proj(x, g, w, tm, tn):
    n_tok = x.shape[0]
    n_out = w.shape[1]
    return pl.pallas_call(
        _in_proj_kernel,
        grid=(n_tok // tm, n_out // tn),
        in_specs=[
            pl.BlockSpec((tm, D_MODEL), lambda i, j: (i, 0)),
            pl.BlockSpec((1, D_MODEL), lambda i, j: (0, 0)),
            pl.BlockSpec((D_MODEL, tn), lambda i, j: (0, j)),
        ],
        out_specs=pl.BlockSpec((tm, tn), lambda i, j: (i, j)),
        out_shape=jax.ShapeDtypeStruct((n_tok, n_out), F32),
        scratch_shapes=[pltpu.VMEM((tm, D_MODEL), BF16)],
        compiler_params=_cparams(("parallel", "arbitrary")),
        name="in_proj",
    )(x, g, w)


def _gate_a_kernel(u_ref, v_ref, lnw_ref, lnb_ref, ws_ref, bs_ref, ya_ref, av_ref, *, n_chunks):
    row = lax.broadcasted_iota(jnp.int32, (CHUNK, CHUNK), 0)
    col = lax.broadcasted_iota(jnp.int32, (CHUNK, CHUNK), 1)
    causal = col <= row
    gd = A_WIDTH // A_GROUPS
    for c in range(n_chunks):
        rows = pl.ds(c * CHUNK, CHUNK)
        v = _gelu(v_ref[rows, :])
        mu = jnp.mean(v, axis=-1, keepdims=True)
        vc = v - mu
        var = jnp.mean(vc * vc, axis=-1, keepdims=True)
        av = vc * lax.rsqrt(var + NORM_EPS) * lnw_ref[...] + lnb_ref[...]
        av_ref[rows, :] = av
        u = _gelu(u_ref[rows, :])
        for g in range(A_GROUPS):
            w = jnp.where(causal, ws_ref[g], 0.0).astype(BF16)
            s = _dot(w, av[:, g * gd:(g + 1) * gd].astype(BF16)) + bs_ref[g]
            ya_ref[rows, g * gd:(g + 1) * gd] = (u[:, g * gd:(g + 1) * gd] * s).astype(BF16)


def _gate_a(proj, lnw, lnb, ws, bs, tm):
    n_tok = proj.shape[0]
    kern = functools.partial(_gate_a_kernel, n_chunks=tm // CHUNK)
    return pl.pallas_call(
        kern,
        grid=(n_tok // tm,),
        in_specs=[
            pl.BlockSpec((tm, A_WIDTH), lambda i: (i, COL_AU // A_WIDTH)),
            pl.BlockSpec((tm, A_WIDTH), lambda i: (i, COL_AV // A_WIDTH)),
            pl.BlockSpec((1, A_WIDTH), lambda i: (0, 0)),
            pl.BlockSpec((1, A_WIDTH), lambda i: (0, 0)),
            pl.BlockSpec((A_GROUPS, CHUNK, CHUNK), lambda i: (0, 0, 0)),
            pl.BlockSpec((A_GROUPS, CHUNK, 1), lambda i: (0, 0, 0)),
        ],
        out_specs=[
            pl.BlockSpec((tm, A_WIDTH), lambda i: (i, 0)),
            pl.BlockSpec((tm, A_WIDTH), lambda i: (i, 0)),
        ],
        out_shape=[
            jax.ShapeDtypeStruct((n_tok, A_WIDTH), BF16),
            jax.ShapeDtypeStruct((n_tok, A_WIDTH), F32),
        ],
        compiler_params=_cparams(("parallel",)),
        name="gate_a",
    )(proj, proj, lnw, lnb, ws, bs)


def _ssd_prompt_kernel(xbc_ref, z_ref, dt_ref, cw_ref, cb_ref, dtb_ref, alog_ref, dsk_ref, bn_ref, e_ref,
                       yb_ref, st_ref, xp_scr, h_scr, y_scr, xde_scr):
    L, P, N = CHUNK, B_HEADDIM, B_STATE
    i = pl.program_id(0)

    @pl.when(i == 0)
    def _():
        xp_scr[0:8, :] = jnp.zeros((8, B_CONV_DIM), F32)
        h_scr[...] = jnp.zeros_like(h_scr)

    xp_scr[8:8 + L, :] = xbc_ref[...]
    acc = xp_scr[8:8 + L, :] * cw_ref[B_CONV - 1:B_CONV, :]
    for sh in range(1, B_CONV):
        acc = acc + xp_scr[8 - sh:8 - sh + L, :] * cw_ref[B_CONV - 1 - sh:B_CONV - sh, :]
    y_scr[...] = _silu(acc + cb_ref[...])
    xp_scr[0:8, :] = xp_scr[L:L + 8, :]

    lane = lax.broadcasted_iota(jnp.int32, (1, 128), 1)
    a_row = jnp.where(lane < B_HEADS, -jnp.exp(alog_ref[...]), 0.0)
    dt = _softplus(dt_ref[...] + dtb_ref[...])
    da = dt * a_row
    row = lax.broadcasted_iota(jnp.int32, (L, L), 0)
    col = lax.broadcasted_iota(jnp.int32, (L, L), 1)
    causal = col <= row
    da3 = _split3(da, 1)
    cs = _sum3(_dot(jnp.where(causal, 1.0, 0.0).astype(BF16), da3), 1) * LOG2E
    cs_t = _sum3(_dot_tn(da3, jnp.where(row <= col, 1.0, 0.0).astype(BF16)), 0) * LOG2E
    cs_last = cs[L - 1:L, :]
    dec_e = _sum3(_dot(_split3(jnp.broadcast_to(jnp.exp2(cs_last), (8, 128)), 0), e_ref[...]), 0)[0:1, :]

    rep = B_HEADS // B_GROUPS
    gw = rep * P
    for g in range(B_GROUPS):
        bg = y_scr[:, B_INNER + g * N:B_INNER + (g + 1) * N].astype(BF16)
        cg = y_scr[:, B_INNER + (B_GROUPS + g) * N:B_INNER + (B_GROUPS + g + 1) * N].astype(BF16)
        scores = _dot_nt(cg, bg)
        hg = h_scr[g]
        yo = _dot(cg, hg.astype(BF16))
        for pr in range(rep // 2):
            ha = g * rep + 2 * pr
            cols = slice(ha * P, (ha + 2) * P)
            col_a = jnp.broadcast_to(cs[:, ha:ha + 1], (L, L))
            col_b = jnp.broadcast_to(cs[:, ha + 1:ha + 2], (L, L))
            m_a = scores * jnp.exp2(jnp.where(causal, col_a - cs_t[ha:ha + 1, :], -jnp.inf))
            m_b = scores * jnp.exp2(jnp.where(causal, col_b - cs_t[ha + 1:ha + 2, :], -jnp.inf))
            first = lane < P
            col2 = jnp.where(first, col_a, col_b)
            dt2 = jnp.where(first, dt[:, ha:ha + 1], dt[:, ha + 1:ha + 2])
            last2 = jnp.where(first, cs_last[:, ha:ha + 1], cs_last[:, ha + 1:ha + 2])
            xs2 = y_scr[:, cols]
            xdt = xs2 * dt2
            xdt_b = xdt.astype(BF16)
            zero = jnp.zeros_like(xdt_b)
            x_diag = jnp.concatenate([jnp.where(first, xdt_b, zero), jnp.where(first, zero, xdt_b)], axis=0)
            yd = _dot(jnp.concatenate([m_a.astype(BF16), m_b.astype(BF16)], axis=1), x_diag)
            y_scr[:, cols] = yd + jnp.exp2(col2) * yo[:, 2 * pr * P:(2 * pr + 2) * P] + dsk_ref[:, cols] * xs2
            xde_scr[:, cols] = (xdt * jnp.exp2(last2 - col2)).astype(BF16)
        st = _dot_tn(bg, xde_scr[:, g * gw:(g + 1) * gw])
        h_scr[g] = dec_e[:, g * gw:(g + 1) * gw] * hg + st

    y = y_scr[:, 0:B_INNER] * _silu(z_ref[...])
    yb_ref[...] = _rms(y, bn_ref[...]).astype(BF16)
    st_ref[...] = h_scr[...]


def _ssd_prompt(proj, cw, cb, dtb128, alog128, dsk_e, bnorm, e_mat):
    S = proj.shape[0]
    gw = (B_HEADS // B_GROUPS) * B_HEADDIM
    return pl.pallas_call(
        _ssd_prompt_kernel,
        grid=(S // CHUNK,),
        in_specs=[
            pl.BlockSpec((CHUNK, B_CONV_DIM), lambda i: (i, COL_XBC // B_CONV_DIM)),
            pl.BlockSpec((CHUNK, B_INNER), lambda i: (i, COL_Z // B_INNER)),
            pl.BlockSpec((CHUNK, 128), lambda i: (i, COL_DT // 128)),
            pl.BlockSpec((B_CONV, B_CONV_DIM), lambda i: (0, 0)),
            pl.BlockSpec((1, B_CONV_DIM), lambda i: (0, 0)),
            pl.BlockSpec((1, 128), lambda i: (0, 0)),
            pl.BlockSpec((1, 128), lambda i: (0, 0)),
            pl.BlockSpec((1, B_INNER), lambda i: (0, 0)),
            pl.BlockSpec((1, B_INNER), lambda i: (0, 0)),
            pl.BlockSpec((128, B_INNER), lambda i: (0, 0)),
        ],
        out_specs=[
            pl.BlockSpec((CHUNK, B_INNER), lambda i: (i, 0)),
            pl.BlockSpec((B_GROUPS, B_STATE, gw), lambda i: (0, 0, 0)),
        ],
        out_shape=[
            jax.ShapeDtypeStruct((S, B_INNER), BF16),
            jax.ShapeDtypeStruct((B_GROUPS, B_STATE, gw), F32),
        ],
        scratch_shapes=[
            pltpu.VMEM((CHUNK + 8, B_CONV_DIM), F32),
            pltpu.VMEM((B_GROUPS, B_STATE, gw), F32),
            pltpu.VMEM((CHUNK, B_CONV_DIM), F32),
            pltpu.VMEM((CHUNK, B_INNER), BF16),
        ],
        compiler_params=_cparams(("arbitrary",)),
        name="ssd_prompt",
    )(proj, proj, proj, cw, cb, dtb128, alog128, dsk_e, bnorm, e_mat)


def _ssd_sample_kernel(xbc_ref, cprev_ref, z_ref, dt_ref, h0_ref, cw_ref, cb_ref, dtb_ref, alog_ref,
                       dsk_ref, bn_ref, e_ref, bm_ref, s_ref,
                       yb_ref, hn_ref, xp_scr, r16_scr, prod_scr, ex_scr, a_scr, b_scr, *, T, seqs):
    for e in range(seqs):
        _ssd_sample_one(e, xbc_ref, cprev_ref, z_ref, dt_ref, h0_ref, cw_ref, cb_ref, dtb_ref, alog_ref,
                        dsk_ref, bn_ref, e_ref, bm_ref, s_ref, yb_ref, hn_ref,
                        xp_scr.at[e], r16_scr.at[e], prod_scr.at[e], ex_scr.at[e], a_scr.at[e], b_scr.at[e], T)


def _ssd_sample_one(e, xbc_ref, cprev_ref, z_ref, dt_ref, h0_ref, cw_ref, cb_ref, dtb_ref, alog_ref,
                    dsk_ref, bn_ref, e_ref, bm_ref, s_ref,
                    yb_ref, hn_ref, xp_scr, r16_scr, prod_scr, ex_scr, a_scr, b_scr, T):
    H, P, N, G = B_HEADS, B_HEADDIM, B_STATE, B_GROUPS
    rep = H // G
    pairs = [(t, s) for t in range(T) for s in range(t + 1)]
    xp_scr[0:B_CONV - 1, :] = cprev_ref[e]
    xp_scr[B_CONV - 1:B_CONV - 1 + T, :] = xbc_ref[e]
    acc = xp_scr[0:T, :] * cw_ref[0:1, :]
    for k in range(1, B_CONV):
        acc = acc + xp_scr[k:k + T, :] * cw_ref[k:k + 1, :]
    xc = _silu(acc + cb_ref[...])
    xs = xc[:, 0:B_INNER]
    bm = xc[:, B_INNER:B_INNER + G * N]
    cm = xc[:, B_INNER + G * N:]

    lane = lax.broadcasted_iota(jnp.int32, (1, 128), 1)
    a_row = jnp.where(lane < H, -jnp.exp(alog_ref[...]), 0.0)
    dt = _softplus(dt_ref[e] + dtb_ref[...])
    da = dt * a_row
    cs_rows = [da[0:1, :]]
    for t in range(1, T):
        cs_rows.append(cs_rows[-1] + da[t:t + 1, :])
    cs_last = cs_rows[-1]

    prod_scr[...] = jnp.zeros_like(prod_scr)
    for idx, (t, s) in enumerate(pairs):
        prod_scr[idx:idx + 1, :] = cm[t:t + 1, :] * bm[s:s + 1, :]
    gh = _sum3(_dot(_split3(prod_scr[...], 0), s_ref[...]), 0)
    ex_scr[...] = jnp.zeros_like(ex_scr)
    ex_scr[0:T, :] = dt
    for t in range(T):
        ex_scr[T + t:T + t + 1, :] = jnp.exp(cs_rows[t])
    for idx, (t, s) in enumerate(pairs):
        ex_scr[2 * T + idx:2 * T + idx + 1, :] = gh[idx:idx + 1, :] * jnp.exp(cs_rows[t] - cs_rows[s])
    ex = _sum3(_dot(_split3(ex_scr[...], 0), e_ref[...]), 0)
    xdt = xs * ex[0:T, :]

    for g in range(G):
        r16_scr[g * T:(g + 1) * T, :] = cm[:, g * N:(g + 1) * N]
    h2d = h0_ref[e].reshape(H * P, N)
    r = _dot_nt(r16_scr[...].astype(BF16), h2d.astype(BF16))
    gw = rep * P
    y_rows = []
    for t in range(T):
        yo = jnp.concatenate([r[g * T + t:g * T + t + 1, g * gw:(g + 1) * gw] for g in range(G)], axis=1)
        y_rows.append(yo * ex[T + t:T + t + 1, :])
    for idx, (t, s) in enumerate(pairs):
        y_rows[t] = y_rows[t] + ex[2 * T + idx:2 * T + idx + 1, :] * xdt[s:s + 1, :]
    for t in range(T):
        y = y_rows[t] + dsk_ref[...] * xs[t:t + 1, :]
        y = y * _silu(z_ref[e, t:t + 1, :])
        yb_ref[e, t:t + 1, :] = _rms(y, bn_ref[...]).astype(BF16)

    eye = (lax.broadcasted_iota(jnp.int32, (H, 128), 0) == lax.broadcasted_iota(jnp.int32, (H, 128), 1)).astype(F32)

    def to_col(v):
        return jnp.sum(jnp.broadcast_to(v, (H, 128)) * eye, axis=1, keepdims=True)

    a_scr[...] = jnp.zeros_like(a_scr)
    b_scr[...] = jnp.zeros_like(b_scr)
    for t in range(T):
        dcol = to_col(jnp.exp(cs_last - cs_rows[t]))
        a_scr[t * H:(t + 1) * H, :] = jnp.broadcast_to(xdt[t:t + 1, :], (H, H * P)) * bm_ref[...]
        for g in range(G):
            b_scr[t * H + g * rep:t * H + (g + 1) * rep, :] = (
                jnp.broadcast_to(bm[t:t + 1, g * N:(g + 1) * N], (rep, N)) * dcol[g * rep:(g + 1) * rep, :])
    st = _dot_tn(a_scr[...].astype(BF16), b_scr[...].astype(BF16))
    dfull = jnp.broadcast_to(to_col(jnp.exp(cs_last)), (H, N))
    for h in range(H):
        hn_ref[e, h] = dfull[h:h + 1, :] * h0_ref[e, h] + st[h * P:(h + 1) * P, :]


def _ssd_sample(xbc, cprev, z, dtb, h0, layer, cw, cb, dtb128, alog128, dsk_e, bnorm, e_mat, blockmask, s_mat):
    B, T, _ = xbc.shape
    n_pair = -(-(T * (T + 1) // 2) // 8) * 8
    assert B_CONV - 1 + T <= 8 and T * B_HEADS <= 128
    nseq = SSD_SAMPLE_SEQS if B % SSD_SAMPLE_SEQS == 0 else 1
    kern = functools.partial(_ssd_sample_kernel, T=T, seqs=nseq)
    c2 = lambda b: (0, 0)
    return pl.pallas_call(
        kern,
        grid=(B // nseq,),
        in_specs=[
            pl.BlockSpec((nseq, T, B_CONV_DIM), lambda b: (b, 0, 0)),
            pl.BlockSpec((None, nseq, B_CONV - 1, B_CONV_DIM), lambda b: (layer, b, 0, 0)),
            pl.BlockSpec((nseq, T, B_INNER), lambda b: (b, 0, 0)),
            pl.BlockSpec((nseq, T, 128), lambda b: (b, 0, 0)),
            pl.BlockSpec((None, nseq, B_HEADS, B_HEADDIM, B_STATE), lambda b: (layer, b, 0, 0, 0)),
            pl.BlockSpec((B_CONV, B_CONV_DIM), c2),
            pl.BlockSpec((1, B_CONV_DIM), c2),
            pl.BlockSpec((1, 128), c2),
            pl.BlockSpec((1, 128), c2),
            pl.BlockSpec((1, B_INNER), c2),
            pl.BlockSpec((1, B_INNER), c2),
            pl.BlockSpec((128, B_INNER), c2),
            pl.BlockSpec((B_HEADS, B_INNER), c2),
            pl.BlockSpec((B_GROUPS * B_STATE, 128), c2),
        ],
        out_specs=[
            pl.BlockSpec((nseq, T, B_INNER), lambda b: (b, 0, 0)),
            pl.BlockSpec((nseq, B_HEADS, B_HEADDIM, B_STATE), lambda b: (b, 0, 0, 0)),
        ],
        out_shape=[
            jax.ShapeDtypeStruct((B, T, B_INNER), BF16),
            jax.ShapeDtypeStruct((B, B_HEADS, B_HEADDIM, B_STATE), F32),
        ],
        scratch_shapes=[
            pltpu.VMEM((nseq, 8, B_CONV_DIM), F32),
            pltpu.VMEM((nseq, B_GROUPS * T, B_STATE), F32),
            pltpu.VMEM((nseq, n_pair, B_GROUPS * B_STATE), F32),
            pltpu.VMEM((nseq, 2 * T + n_pair, 128), F32),
            pltpu.VMEM((nseq, 128, B_INNER), F32),
            pltpu.VMEM((nseq, 128, B_STATE), F32),
        ],
        compiler_params=_cparams(("parallel",)),
        name="ssd_sample",
    )(xbc, cprev, z, dtb, h0, cw, cb, dtb128, alog128, dsk_e, bnorm, e_mat, blockmask, s_mat)


def _c_prep_kernel(cq_ref, ckv_ref, kr_ref, krs_ref, cos_ref, sin_ref, qn_ref, kvn_ref,
                   wq1_ref, wq2_ref, wk_ref, wvt_ref,
                   q_ref, k_ref, vt_ref, ckvn_ref, krope_ref):
    cos = cos_ref[...]
    sin = sin_ref[...]
    cos8 = jnp.concatenate([cos] * C_HEADS, axis=1)
    sin8 = jnp.concatenate([sin] * C_HEADS, axis=1)
    cqn = _rms(cq_ref[...], qn_ref[...]).astype(BF16)
    q = _dot(cqn, wq1_ref[...]) * cos8 + _dot(cqn, wq2_ref[...]) * sin8
    q_ref[...] = (q * (C_SCALE * LOG2E)).astype(BF16)
    ckvn = _rms(ckv_ref[...], kvn_ref[...])
    ckvn_ref[...] = ckvn
    k128 = kr_ref[...] * cos + krs_ref[...] * sin
    krope_ref[...] = k128[:, C_NOPE:C_NOPE + C_ROPE]
    cb = ckvn.astype(BF16)
    k_ref[...] = (_dot(cb, wk_ref[...]) + jnp.concatenate([k128] * C_HEADS, axis=1)).astype(BF16)
    row = lax.broadcasted_iota(jnp.int32, (V_ROWS, 1), 0)
    for h in range(C_HEADS):
        vt = _dot_nt(wvt_ref[h], cb)
        vt_ref[h * V_ROWS:(h + 1) * V_ROWS, :] = jnp.where(row == C_V, 1.0, vt).astype(BF16)


def _c_prep(proj, cos, sin, qn, kvn, wq1, wq2, wk, wv, tm):
    n_tok = proj.shape[0]
    c2 = lambda i: (0, 0)
    hq = C_HEADS * HEAD_PAD
    return pl.pallas_call(
        _c_prep_kernel,
        grid=(n_tok // tm,),
        in_specs=[
            pl.BlockSpec((tm, C_Q_LORA), lambda i: (i, COL_CQ // C_Q_LORA)),
            pl.BlockSpec((tm, C_KV_LORA), lambda i: (i, COL_CKV // C_KV_LORA)),
            pl.BlockSpec((tm, 128), lambda i: (i, COL_KR // 128)),
            pl.BlockSpec((tm, 128), lambda i: (i, COL_KRS // 128)),
            pl.BlockSpec((tm, 128), lambda i: (i, 0)),
            pl.BlockSpec((tm, 128), lambda i: (i, 0)),
            pl.BlockSpec((1, C_Q_LORA), c2),
            pl.BlockSpec((1, C_KV_LORA), c2),
            pl.BlockSpec((C_Q_LORA, hq), c2),
            pl.BlockSpec((C_Q_LORA, hq), c2),
            pl.BlockSpec((C_KV_LORA, hq), c2),
            pl.BlockSpec((C_HEADS, V_ROWS, C_KV_LORA), lambda i: (0, 0, 0)),
        ],
        out_specs=[
            pl.BlockSpec((tm, hq), lambda i: (i, 0)),
            pl.BlockSpec((tm, hq), lambda i: (i, 0)),
            pl.BlockSpec((C_HEADS * V_ROWS, tm), lambda i: (0, i)),
            pl.BlockSpec((tm, C_KV_LORA), lambda i: (i, 0)),
            pl.BlockSpec((tm, C_ROPE), lambda i: (i, 0)),
        ],
        out_shape=[
            jax.ShapeDtypeStruct((n_tok, hq), BF16),
            jax.ShapeDtypeStruct((n_tok, hq), BF16),
            jax.ShapeDtypeStruct((C_HEADS * V_ROWS, n_tok), BF16),
            jax.ShapeDtypeStruct((n_tok, C_KV_LORA), F32),
            jax.ShapeDtypeStruct((n_tok, C_ROPE), F32),
        ],
        compiler_params=_cparams(("parallel",)),
        name="c_prep",
    )(proj, proj, proj, proj, cos, sin, qn, kvn, wq1, wq2, wk, wv)


def _flash_kernel(qi_ref, kj_ref, q_ref, k_ref, vt_ref, o_ref, m_scr, acc_scr, s_scr, *, tq, tk, hps):
    s_idx = pl.program_id(1)
    qi = qi_ref[s_idx]
    kj = kj_ref[s_idx]

    @pl.when(kj == 0)
    def _():
        m_scr[...] = jnp.full_like(m_scr, -jnp.inf)
        acc_scr[...] = jnp.zeros_like(acc_scr)

    def step(masked):
        if masked:
            visible = (lax.broadcasted_iota(jnp.int32, (tk, tq), 0)
                       <= lax.broadcasted_iota(jnp.int32, (tk, tq), 1))

        def scores(hh):
            lanes = slice(hh * HEAD_PAD, (hh + 1) * HEAD_PAD)
            s = _dot_nt(k_ref[:, lanes], q_ref[:, lanes])
            if masked:
                s = jnp.where(visible, s, -jnp.inf)
            s_scr[hh % (FLASH_LOOKAHEAD + 1)] = s

        for hh in range(min(FLASH_LOOKAHEAD, hps)):
            scores(hh)
        for hh in range(hps):
            if hh + FLASH_LOOKAHEAD < hps:
                scores(hh + FLASH_LOOKAHEAD)
            s_tile = s_scr.at[hh % (FLASH_LOOKAHEAD + 1)]
            m_prev = m_scr[hh]
            m_new = jnp.maximum(m_prev, jnp.max(s_tile[...], axis=0, keepdims=True))
            m_scr[hh] = m_new
            alpha = jnp.exp2(m_prev[0:1, :] - m_new[0:1, :])
            p = jnp.exp2(s_tile[...] - m_new[0:1, :]).astype(BF16)
            acc_scr[hh] = alpha * acc_scr[hh] + _dot(vt_ref[hh * V_ROWS:(hh + 1) * V_ROWS, :], p)

    @pl.when(kj < qi)
    def _():
        step(False)

    @pl.when(kj == qi)
    def _():
        step(True)
        for hh in range(hps):
            a = acc_scr[hh]
            o_ref[hh * C_V:(hh + 1) * C_V, :] = (a[0:C_V, :] / a[C_V:C_V + 1, :]).astype(BF16)


def _flash(q, k, vt, tq, hps):
    S = q.shape[0]
    tk = tq
    nq = S // tq
    qi = np.concatenate([np.full(i + 1, i, np.int32) for i in range(nq)])
    kj = np.concatenate([np.arange(i + 1, dtype=np.int32) for i in range(nq)])
    kern = functools.partial(_flash_kernel, tq=tq, tk=tk, hps=hps)
    grid_spec = pltpu.PrefetchScalarGridSpec(
        num_scalar_prefetch=2,
        grid=(C_HEADS // hps, int(qi.shape[0])),
        in_specs=[
            pl.BlockSpec((tq, hps * HEAD_PAD), lambda p, s, qi, kj: (qi[s], p)),
            pl.BlockSpec((tk, hps * HEAD_PAD), lambda p, s, qi, kj: (kj[s], p)),
            pl.BlockSpec((hps * V_ROWS, tk), lambda p, s, qi, kj: (p, kj[s])),
        ],
        out_specs=pl.BlockSpec((hps * C_V, tq), lambda p, s, qi, kj: (p, qi[s])),
        scratch_shapes=[
            pltpu.VMEM((hps, 8, tq), F32),
            pltpu.VMEM((hps, V_ROWS, tq), F32),
            pltpu.VMEM((FLASH_LOOKAHEAD + 1, tk, tq), F32),
        ],
    )
    return pl.pallas_call(
        kern,
        grid_spec=grid_spec,
        out_shape=jax.ShapeDtypeStruct((C_HEADS * C_V, S), BF16),
        compiler_params=_cparams(("parallel", "arbitrary")),
        name="flash_prompt",
    )(jnp.asarray(qi), jnp.asarray(kj), q, k, vt)


def _q_lat_kernel(q_ref, wt_ref, o_ref):
    o_ref[0] = _dot(q_ref[...], wt_ref[0]).astype(BF16)


def _q_lat(q, wukt):
    n = q.shape[0]
    return pl.pallas_call(
        _q_lat_kernel,
        grid=(C_HEADS,),
        in_specs=[
            pl.BlockSpec((n, HEAD_PAD), lambda h: (0, h)),
            pl.BlockSpec((1, HEAD_PAD, C_KV_LORA), lambda h: (h, 0, 0)),
        ],
        out_specs=pl.BlockSpec((1, n, C_KV_LORA), lambda h: (h, 0, 0)),
        out_shape=jax.ShapeDtypeStruct((C_HEADS, n, C_KV_LORA), BF16),
        compiler_params=_cparams(("parallel",)),
        name="q_lat",
    )(q, wukt)


def _attn_sample_kernel(pt_ref, qlat_ref, q128_ref, ckv_ref, krn_ref, kv_hbm, kr_hbm, o_ref,
                        kvbuf, krbuf, sem, newkv_scr, newkr_scr, *, layer, T, n_pages, cp, n_streams):
    b = pl.program_id(0)
    nb = pl.num_programs(0)
    n_chunks = n_pages // cp
    R = T * C_HEADS
    spp = cp // n_streams
    n_slots = ATTN_SLOTS
    ahead = n_slots - 1

    def copies(bb, c):
        slot = c % n_slots
        out = []
        for i in range(cp):
            page = pt_ref[bb, c * cp + i]
            out.append(pltpu.make_async_copy(kv_hbm.at[layer, page], kvbuf.at[slot, i], sem.at[0, slot]))
            out.append(pltpu.make_async_copy(kr_hbm.at[layer, page], krbuf.at[slot, i], sem.at[1, slot]))
        return out

    def start(bb, c):
        for cpy in copies(bb, c):
            cpy.start()

    def wait(bb, c):
        for cpy in copies(bb, c):
            cpy.wait()

    @pl.when(b == 0)
    def _():
        for c in range(ahead):
            start(0, c)

    b_next = jnp.minimum(b + 1, nb - 1)

    qlat = qlat_ref[0]
    qr = q128_ref[0][:, C_NOPE:C_NOPE + C_ROPE]

    def softmax_update(state, s):
        m_prev, l_prev, _ = state
        m_new = jnp.maximum(m_prev, jnp.max(s, axis=-1, keepdims=True))
        alpha = jnp.exp2(m_prev - m_new)
        p = jnp.exp2(s - m_new)
        l_new = alpha * l_prev + jnp.sum(p, axis=-1, keepdims=True)
        return m_new, l_new, alpha, p.astype(BF16)

    def online(state, s, kv):
        m_new, l_new, alpha, p = softmax_update(state, s)
        return m_new, l_new, state[2] * alpha + _dot(p, kv)

    def chunk(c, carry):
        slot = c % n_slots
        wait(b, c)
        kvs, scores = [], []
        for si in range(n_streams):
            kv = kvbuf[slot, si * spp:(si + 1) * spp].reshape(spp * PAGE_SIZE, C_KV_LORA).astype(BF16)
            kr_t = jnp.concatenate([krbuf[slot, si * spp + i] for i in range(spp)], axis=1).astype(BF16)
            kvs.append(kv)
            scores.append(_dot_nt(qlat, kv) + _dot(qr, kr_t))
        if c + ahead < n_chunks:
            start(b, c + ahead)
        else:
            start(b_next, c + ahead - n_chunks)
        stats = [softmax_update(carry[si], scores[si]) for si in range(n_streams)]
        return tuple((m_new, l_new, carry[si][2] * alpha + _dot(p, kvs[si]))
                     for si, (m_new, l_new, alpha, p) in enumerate(stats))

    streams = tuple((jnp.full((R, 1), -jnp.inf, F32), jnp.zeros((R, 1), F32), jnp.zeros((R, C_KV_LORA), F32))
                    for _ in range(n_streams))
    for c in range(n_chunks):
        streams = chunk(c, streams)

    @pl.when(b == nb - 1)
    def _():
        for c in range(ahead):
            wait(b_next, c)

    newkv_scr[...] = jnp.zeros_like(newkv_scr)
    newkr_scr[...] = jnp.zeros_like(newkr_scr)
    newkv_scr[0:T, :] = ckv_ref[0]
    newkr_scr[0:T, :] = krn_ref[0]
    kvn = newkv_scr[...].astype(BF16)
    krn = newkr_scr[...].astype(BF16)
    s = _dot_nt(qlat, kvn) + _dot_nt(qr, krn)
    t_row = lax.broadcasted_iota(jnp.int32, (R, 128), 0) % T
    key = lax.broadcasted_iota(jnp.int32, (R, 128), 1)
    s = jnp.where(key <= t_row, s, -jnp.inf)
    m_all, l_all, acc_all = online(streams[0], s, kvn)
    for m_i, l_i, acc_i in streams[1:]:
        m_new = jnp.maximum(m_all, m_i)
        wa = jnp.exp2(m_all - m_new)
        wi = jnp.exp2(m_i - m_new)
        l_all = wa * l_all + wi * l_i
        acc_all = wa * acc_all + wi * acc_i
        m_all = m_new
    o_ref[0] = (acc_all / l_all).astype(BF16)


def _attn_sample(page_table, qlat, q128, ckv_new, kr_new, cache_kv, cache_kr_t, layer, cp, n_streams):
    B, R, _ = qlat.shape
    T = ckv_new.shape[1]
    n_pages = page_table.shape[1]
    assert (n_pages // cp) % ATTN_SLOTS == 0 and cp % n_streams == 0
    kern = functools.partial(_attn_sample_kernel, layer=layer, T=T, n_pages=n_pages, cp=cp, n_streams=n_streams)
    grid_spec = pltpu.PrefetchScalarGridSpec(
        num_scalar_prefetch=1,
        grid=(B,),
        in_specs=[
            pl.BlockSpec((1, R, C_KV_LORA), lambda b, pt: (b, 0, 0)),
            pl.BlockSpec((1, R, HEAD_PAD), lambda b, pt: (b, 0, 0)),
            pl.BlockSpec((1, T, C_KV_LORA), lambda b, pt: (b, 0, 0)),
            pl.BlockSpec((1, T, C_ROPE), lambda b, pt: (b, 0, 0)),
            pl.BlockSpec(memory_space=pl.ANY),
            pl.BlockSpec(memory_space=pl.ANY),
        ],
        out_specs=pl.BlockSpec((1, R, C_KV_LORA), lambda b, pt: (b, 0, 0)),
        scratch_shapes=[
            pltpu.VMEM((ATTN_SLOTS, cp, PAGE_SIZE, C_KV_LORA), F32),
            pltpu.VMEM((ATTN_SLOTS, cp, C_ROPE, PAGE_SIZE), F32),
            pltpu.SemaphoreType.DMA((2, ATTN_SLOTS)),
            pltpu.VMEM((128, C_KV_LORA), F32),
            pltpu.VMEM((128, C_ROPE), F32),
        ],
    )
    return pl.pallas_call(
        kern,
        grid_spec=grid_spec,
        out_shape=jax.ShapeDtypeStruct((B, R, C_KV_LORA), BF16),
        compiler_params=_cparams(("arbitrary",)),
        name="attn_sample",
    )(page_table, qlat, q128, ckv_new, kr_new, cache_kv, cache_kr_t)


def _uv_proj_kernel(o_ref, wt_ref, y_ref):
    y_ref[0] = _dot_nt(wt_ref[0], o_ref[0]).astype(BF16)


def _uv_proj(olat, wuvt):
    H, n, _ = olat.shape
    return pl.pallas_call(
        _uv_proj_kernel,
        grid=(H,),
        in_specs=[
            pl.BlockSpec((1, n, C_KV_LORA), lambda h: (h, 0, 0)),
            pl.BlockSpec((1, C_V, C_KV_LORA), lambda h: (h, 0, 0)),
        ],
        out_specs=pl.BlockSpec((1, C_V, n), lambda h: (h, 0, 0)),
        out_shape=jax.ShapeDtypeStruct((H, C_V, n), BF16),
        compiler_params=_cparams(("parallel",)),
        name="uv_proj",
    )(olat, wuvt)


def _merge_kernel(x_ref, g_ref, wg_ref, ya_ref, yb_ref, yct_ref, wpa_ref, wpb_ref, wpc_ref, wo_ref, o_ref):
    x = x_ref[...]
    gates = _dot(_rms(x, g_ref[...]).astype(BF16), wg_ref[...])
    m = _sigmoid(gates[:, 0:D_MODEL]) * _dot(ya_ref[...], wpa_ref[...])
    m = m + _sigmoid(gates[:, D_MODEL:2 * D_MODEL]) * _dot(yb_ref[...], wpb_ref[...])
    m = m + _sigmoid(gates[:, 2 * D_MODEL:]) * _dot_tn(yct_ref[...], wpc_ref[...])
    o_ref[...] = x + _dot(m.astype(BF16), wo_ref[...])


def _merge(x, g_mix, w_gate, ya, yb, yct, wpa, wpb, wpc, wo, tm):
    n_tok = x.shape[0]
    c2 = lambda i: (0, 0)
    return pl.pallas_call(
        _merge_kernel,
        grid=(n_tok // tm,),
        in_specs=[
            pl.BlockSpec((tm, D_MODEL), lambda i: (i, 0)),
            pl.BlockSpec((1, D_MODEL), c2),
            pl.BlockSpec((D_MODEL, N_BRANCH * D_MODEL), c2),
            pl.BlockSpec((tm, A_WIDTH), lambda i: (i, 0)),
            pl.BlockSpec((tm, B_INNER), lambda i: (i, 0)),
            pl.BlockSpec((C_HEADS * C_V, tm), lambda i: (0, i)),
            pl.BlockSpec((A_WIDTH, D_MODEL), c2),
            pl.BlockSpec((B_INNER, D_MODEL), c2),
            pl.BlockSpec((C_HEADS * C_V, D_MODEL), c2),
            pl.BlockSpec((D_MODEL, D_MODEL), c2),
        ],
        out_specs=pl.BlockSpec((tm, D_MODEL), lambda i: (i, 0)),
        out_shape=jax.ShapeDtypeStruct((n_tok, D_MODEL), F32),
        compiler_params=_cparams(("parallel",)),
        name="merge",
    )(x, g_mix, w_gate, ya, yb, yct, wpa, wpb, wpc, wo)


def _ffn_ple_kernel(x_ref, g_ref, wu_ref, wd_ref, gp_ref, wg_ref, p_ref, wp_ref, gf_ref, o_ref,
                    h_scr, acc_scr, *, final):
    j = pl.program_id(1)

    @pl.when(j == 0)
    def _():
        x = x_ref[...]
        h_scr[...] = _rms(x, g_ref[...]).astype(BF16)
        acc_scr[...] = x

    u = jnp.maximum(_dot(h_scr[...], wu_ref[...]), 0.0)
    acc_scr[...] += _dot((u * u).astype(BF16), wd_ref[...])

    @pl.when(j == pl.num_programs(1) - 1)
    def _():
        x = acc_scr[...]
        pg = _sigmoid(_dot(_rms(x, gp_ref[...]).astype(BF16), wg_ref[...]))
        y = x + pg * _dot(p_ref[...].astype(BF16), wp_ref[...])
        if final:
            y = _rms(y, gf_ref[...])
        o_ref[...] = y


def _ffn_ple(x, g, wu, wd, gp, wg, p, layer, wp, gf, tm, tf, final):
    n_tok = x.shape[0]
    d_ple = p.shape[2]
    c2 = lambda i, j: (0, 0)
    return pl.pallas_call(
        functools.partial(_ffn_ple_kernel, final=final),
        grid=(n_tok // tm, D_FF // tf),
        in_specs=[
            pl.BlockSpec((tm, D_MODEL), lambda i, j: (i, 0)),
            pl.BlockSpec((1, D_MODEL), c2),
            pl.BlockSpec((D_MODEL, tf), lambda i, j: (0, j)),
            pl.BlockSpec((tf, D_MODEL), lambda i, j: (j, 0)),
            pl.BlockSpec((1, D_MODEL), c2),
            pl.BlockSpec((D_MODEL, D_MODEL), c2),
            pl.BlockSpec((None, tm, d_ple), lambda i, j: (layer, i, 0)),
            pl.BlockSpec((d_ple, D_MODEL), c2),
            pl.BlockSpec((1, D_MODEL), c2),
        ],
        out_specs=pl.BlockSpec((tm, D_MODEL), lambda i, j: (i, 0)),
        out_shape=jax.ShapeDtypeStruct((n_tok, D_MODEL), F32),
        scratch_shapes=[pltpu.VMEM((tm, D_MODEL), BF16), pltpu.VMEM((tm, D_MODEL), F32)],
        compiler_params=_cparams(("parallel", "arbitrary")),
        name="ffn_ple",
    )(x, g, wu, wd, gp, wg, p, wp, gf)


def _prep_layer_weights(w_in, w_uq, w_ukv):
    sizes = (N_BRANCH * D_MODEL, A_WIDTH, A_WIDTH, B_INNER, B_CONV_DIM, B_HEADS, C_Q_LORA, C_KV_LORA, C_ROPE)
    idx = [int(v) for v in np.cumsum(sizes)[:-1]]
    gates, a_u, a_v, z, xbc, dt, c_q, c_kv, k_r = jnp.split(w_in, idx, axis=1)
    half = C_ROPE // 2
    zc = lambda n: jnp.zeros((D_MODEL, n), w_in.dtype)
    k_rs = jnp.concatenate([k_r[:, half:], k_r[:, :half]], axis=1)
    w_in_r = jnp.concatenate([
        c_q, dt, zc(128 - B_HEADS), c_kv,
        zc(C_NOPE), k_r, zc(HEAD_PAD - C_NOPE - C_ROPE),
        zc(C_NOPE), k_rs, zc(HEAD_PAD - C_NOPE - C_ROPE),
        z, xbc, a_u, a_v], axis=1).astype(BF16)
    w_gate = gates.astype(BF16)

    uq = w_uq.reshape(C_Q_LORA, C_HEADS, C_NOPE + C_ROPE)
    uq_n, uq_r = uq[..., :C_NOPE], uq[..., C_NOPE:]
    uq_rs = jnp.concatenate([uq_r[..., half:], uq_r[..., :half]], axis=-1)
    zq = lambda n: jnp.zeros((C_Q_LORA, C_HEADS, n), w_uq.dtype)
    wq1 = jnp.concatenate([uq_n, uq_r, zq(HEAD_PAD - C_NOPE - C_ROPE)], axis=-1).reshape(C_Q_LORA, -1).astype(BF16)
    wq2 = jnp.concatenate([zq(C_NOPE), uq_rs, zq(HEAD_PAD - C_NOPE - C_ROPE)], axis=-1).reshape(C_Q_LORA, -1).astype(BF16)

    ukv = w_ukv.reshape(C_KV_LORA, C_HEADS, C_NOPE + C_V)
    uk, uv = ukv[..., :C_NOPE], ukv[..., C_NOPE:]
    wk = jnp.concatenate([uk, jnp.zeros((C_KV_LORA, C_HEADS, HEAD_PAD - C_NOPE), w_ukv.dtype)], axis=-1)
    wk = wk.reshape(C_KV_LORA, -1).astype(BF16)
    wv = jnp.concatenate([uv, jnp.zeros((C_KV_LORA, C_HEADS, V_ROWS - C_V), w_ukv.dtype)], axis=-1)
    wv = jnp.transpose(wv, (1, 2, 0)).astype(BF16)
    wukt = jnp.transpose(wk.reshape(C_KV_LORA, C_HEADS, HEAD_PAD), (1, 2, 0))
    wuv_h = jnp.transpose(uv, (1, 2, 0)).astype(BF16)
    return w_in_r, w_gate, wq1, wq2, wk, wv, wukt, wuv_h


def _rope_tables(pos):
    half = C_ROPE // 2
    inv = jnp.power(ROPE_BASE, -jnp.arange(half, dtype=F32) * (2.0 / C_ROPE))
    ang = pos.astype(F32)[:, None] * inv[None, :]
    cos, sin = jnp.cos(ang), jnp.sin(ang)
    n = pos.shape[0]
    pad = jnp.zeros((n, HEAD_PAD - C_NOPE - C_ROPE), F32)
    cos_t = jnp.concatenate([jnp.ones((n, C_NOPE), F32), cos, cos, pad], axis=1)
    sin_t = jnp.concatenate([jnp.zeros((n, C_NOPE), F32), -sin, sin, pad], axis=1)
    return cos_t, sin_t


def _pad128(v):
    return jnp.concatenate([v, jnp.zeros((128 - v.shape[0],), v.dtype)])[None, :]


def _token_tile(n, pref):
    t = pref
    while n % t:
        t //= 2
    return t


def kernel(x_prompt, x_sample, cache_kv_latent, cache_k_rope, state_ssm, state_conv, page_table, p_prompt, p_sample, ln_mix, w_in, sgu_ln_w, sgu_ln_b, w_s, b_s, conv_w, conv_b, dt_bias, a_log, d_skip, b_norm, q_norm, w_uq, kv_norm, w_ukv, w_pa, w_pb, w_pc, w_o, ln_ffn, w_up, w_down, ln_ple, w_ple_gate, w_ple, ln_final):
    depth = w_in.shape[0]
    _, S, _ = x_prompt.shape
    B, T, _ = x_sample.shape
    n_pages = page_table.shape[1]
    past_len = n_pages * PAGE_SIZE
    ns = B * T
    assert x_prompt.shape[0] == 1 and S % CHUNK == 0 and ns % CHUNK == 0 and CHUNK % T == 0

    xp = x_prompt.reshape(S, D_MODEL)
    xs = x_sample.reshape(ns, D_MODEL)
    cos_p, sin_p = _rope_tables(jnp.arange(S))
    cos_s, sin_s = _rope_tables(past_len + (jnp.arange(ns) % T))

    hp = np.arange(B_INNER) // B_HEADDIM
    e_mat = jnp.asarray((np.arange(128)[:, None] == hp[None, :]).astype(np.float32)).astype(BF16)
    blockmask = jnp.asarray((np.arange(B_HEADS)[:, None] == hp[None, :]).astype(np.float32))
    gn = np.arange(B_GROUPS * B_STATE) // B_STATE
    hg = np.where(np.arange(128) < B_HEADS, np.arange(128) // (B_HEADS // B_GROUPS), -1)
    s_mat = jnp.asarray((gn[:, None] == hg[None, :]).astype(np.float32)).astype(BF16)

    tm_p = _token_tile(S, 1024)
    tm_s = _token_tile(ns, 512)
    tq = _token_tile(S, FLASH_TILE)
    cp = _token_tile(n_pages // ATTN_SLOTS, ATTN_PAGES_PER_CHUNK)
    cache_kr_t = jnp.swapaxes(cache_k_rope, 2, 3)
    tile_rep = CHUNK // T
    eye_rep = jnp.eye(tile_rep, dtype=F32)

    outs_p, outs_s = [], []
    for i in range(depth):
        w_in_r, w_gate, wq1, wq2, wk, wv, wukt, wuv_h = _prep_layer_weights(w_in[i], w_uq[i], w_ukv[i])
        g_mix = ln_mix[i][None, :]
        lnw, lnb = sgu_ln_w[i][None, :], sgu_ln_b[i][None, :]
        ws_p = w_s[i][:, :CHUNK, :CHUNK]
        bs_p = b_s[i][:, :CHUNK, None]
        ws_t = jnp.tril(w_s[i][:, :T, :T])
        ws_s = jnp.einsum('ab,gts->gatbs', eye_rep, ws_t).reshape(A_GROUPS, CHUNK, CHUNK)
        bs_s = jnp.tile(b_s[i][:, :T], (1, tile_rep))[:, :, None]
        cw, cb = conv_w[i], conv_b[i][None, :]
        dtb128, alog128 = _pad128(dt_bias[i]), _pad128(a_log[i])
        dsk_e = jnp.repeat(d_skip[i], B_HEADDIM)[None, :]
        bn = b_norm[i][None, :]
        qn, kvn = q_norm[i][None, :], kv_norm[i][None, :]
        wpa, wpb, wpc, wo = (w.astype(BF16) for w in (w_pa[i], w_pb[i], w_pc[i], w_o[i]))
        wu, wd = w_up[i].astype(BF16), w_down[i].astype(BF16)
        wg, wp = w_ple_gate[i].astype(BF16), w_ple[i].astype(BF16)
        g_ffn, g_ple, g_fin = ln_ffn[i][None, :], ln_ple[i][None, :], ln_final[None, :]
        final = i == depth - 1

        proj = _in_proj(xp, g_mix, w_in_r, tm_p, 1024)
        ya, av = _gate_a(proj, lnw, lnb, ws_p, bs_p, _token_tile(S, 512))
        yb, ssm_t = _ssd_prompt(proj, cw, cb, dtb128, alog128, dsk_e, bn, e_mat)
        ssm_p = jnp.transpose(ssm_t.reshape(B_GROUPS, B_STATE, B_HEADS // B_GROUPS, B_HEADDIM),
                              (0, 2, 3, 1)).reshape(1, B_HEADS, B_HEADDIM, B_STATE)
        q, k, v, ckvn, krope = _c_prep(proj, cos_p, sin_p, qn, kvn, wq1, wq2, wk, wv, _token_tile(S, 512))
        yc = _flash(q, k, v, tq, FLASH_HEADS_PER_STEP)
        xp = _merge(xp, g_mix, w_gate, ya, yb, yc, wpa, wpb, wpc, wo, _token_tile(S, 512))
        xp = _ffn_ple(xp, g_ffn, wu, wd, g_ple, wg, p_prompt.reshape(depth, S, -1), i, wp, g_fin, tm_p, 1024, final)
        outs_p.append((ckvn.reshape(1, S, C_KV_LORA), krope.reshape(1, S, C_ROPE),
                       ssm_p,
                       proj[S - (B_CONV - 1):, COL_XBC:COL_XBC + B_CONV_DIM][None],
                       av[S - CHUNK:][None]))

        proj = _in_proj(xs, g_mix, w_in_r, tm_s, 1024)
        ya, av = _gate_a(proj, lnw, lnb, ws_s, bs_s, tm_s)
        xbc_s = proj[:, COL_XBC:COL_XBC + B_CONV_DIM].reshape(B, T, B_CONV_DIM)
        z_s = proj[:, COL_Z:COL_Z + B_INNER].reshape(B, T, B_INNER)
        dt_s = proj[:, COL_DT:COL_DT + 128].reshape(B, T, 128)
        yb, ssm_new = _ssd_sample(xbc_s, state_conv, z_s, dt_s, state_ssm, i, cw, cb, dtb128, alog128,
                                  dsk_e, bn, e_mat, blockmask, s_mat)
        q, _, _, ckvn, krope = _c_prep(proj, cos_s, sin_s, qn, kvn, wq1, wq2, wk, wv, tm_s)
        qlat = _q_lat(q, wukt)
        qlat = jnp.transpose(qlat.reshape(C_HEADS, B, T, C_KV_LORA), (1, 0, 2, 3)).reshape(B, C_HEADS * T, C_KV_LORA)
        q128 = jnp.transpose(q.reshape(B, T, C_HEADS, HEAD_PAD), (0, 2, 1, 3)).reshape(B, C_HEADS * T, HEAD_PAD)
        olat = _attn_sample(page_table, qlat, q128, ckvn.reshape(B, T, C_KV_LORA), krope.reshape(B, T, C_ROPE),
                            cache_kv_latent, cache_kr_t, i, cp, min(ATTN_STREAMS, cp))
        olat = jnp.transpose(olat.reshape(B, C_HEADS, T, C_KV_LORA), (1, 0, 2, 3)).reshape(C_HEADS, ns, C_KV_LORA)
        yc = _uv_proj(olat, wuv_h).reshape(C_HEADS * C_V, ns)
        xs = _merge(xs, g_mix, w_gate, ya, yb.reshape(ns, B_INNER), yc, wpa, wpb, wpc, wo, tm_s)
        xs = _ffn_ple(xs, g_ffn, wu, wd, g_ple, wg, p_sample.reshape(depth, ns, -1), i, wp, g_fin, tm_s, 1024, final)
        outs_s.append((ckvn.reshape(B, T, C_KV_LORA), krope.reshape(B, T, C_ROPE), ssm_new,
                       xbc_s[:, T - (B_CONV - 1):], av.reshape(B, T, A_WIDTH)))

    kv_p, kr_p, ssm_p, conv_p, v_p = [jnp.stack(t) for t in zip(*outs_p)]
    kv_s, kr_s, ssm_s, conv_s, v_s = [jnp.stack(t) for t in zip(*outs_s)]
    return (xp.reshape(1, S, D_MODEL), xs.reshape(B, T, D_MODEL), kv_p, kr_p, ssm_p, conv_p, v_p,
            kv_s, kr_s, ssm_s, conv_s, v_s)
```

```python
import functools
import math

import numpy as np
import jax
import jax.numpy as jnp
from jax import lax
from jax.experimental import pallas as pl
from jax.experimental.pallas import tpu as pltpu

F32 = jnp.float32
BF16 = jnp.bfloat16

NORM_EPS = 1e-6
D_MODEL = 1024
N_BRANCH = 3
A_WIDTH = 512
A_GROUPS = 4
CHUNK = 128
B_INNER = 1024
B_HEADDIM = 64
B_HEADS = 16
B_GROUPS = 4
B_STATE = 128
B_CONV = 4
B_CONV_DIM = B_INNER + 2 * B_GROUPS * B_STATE
C_HEADS = 8
C_NOPE = 64
C_ROPE = 32
C_V = 64
C_KV_LORA = 256
C_Q_LORA = 384
ROPE_BASE = 10000.0
C_SCALE = (C_NOPE + C_ROPE) ** -0.5
LOG2E = math.log2(math.e)
FLASH_HEADS_PER_STEP = 8
FLASH_TILE = 1024
FLASH_LOOKAHEAD = 2
SSD_SAMPLE_SEQS = 4
V_ROWS = 80
ATTN_PAGES_PER_CHUNK = 32
ATTN_SLOTS = 4
ATTN_STREAMS = 4
D_FF = 4 * D_MODEL
PAGE_SIZE = 128
HEAD_PAD = 128

COL_CQ = 0
COL_DT = 384
COL_CKV = 512
COL_KR = 768
COL_KRS = 896
COL_Z = 1024
COL_XBC = 2048
COL_AU = 4096
COL_AV = 4608
D_IN_PAD = 5120
IN_TILE = 1024
IN_TAIL0 = N_BRANCH * D_MODEL + 2 * A_WIDTH + B_INNER + B_CONV_DIM

VMEM_LIMIT = 56 * 1024 * 1024


def _cparams(sem):
    return pltpu.CompilerParams(dimension_semantics=sem, vmem_limit_bytes=VMEM_LIMIT)


def _rms(x, g):
    ms = jnp.mean(x * x, axis=-1, keepdims=True)
    return x * lax.rsqrt(ms + NORM_EPS) * g


def _sigmoid(x):
    return 1.0 / (1.0 + jnp.exp(-x))


def _silu(x):
    return x * _sigmoid(x)


def _gelu(x):
    c = math.sqrt(2.0 / math.pi)
    return 0.5 * x * (1.0 + jnp.tanh(c * (x + 0.044715 * (x * x * x))))


def _softplus(x):
    return jnp.maximum(x, 0.0) + jnp.log(1.0 + jnp.exp(-jnp.abs(x)))


def _dot(a, b):
    return jnp.dot(a, b, preferred_element_type=F32)


def _dot_nt(a, b):
    return lax.dot_general(a, b, (((1,), (1,)), ((), ())), preferred_element_type=F32)


def _dot_tn(a, b):
    return lax.dot_general(a, b, (((0,), (0,)), ((), ())), preferred_element_type=F32)


def _split3(x, axis):
    hi = x.astype(BF16).astype(F32)
    r = x - hi
    mid = r.astype(BF16).astype(F32)
    return jnp.concatenate([hi, mid, r - mid], axis=axis).astype(BF16)


def _sum3(y, axis):
    n = y.shape[axis] // 3
    parts = [lax.slice_in_dim(y, k * n, (k + 1) * n, axis=axis) for k in range(3)]
    return parts[0] + parts[1] + parts[2]


def _in_proj_kernel(x_ref, g_ref, wt_ref, wm_ref, o_ref, h_scr):
    j = pl.program_id(1)

    @pl.when(j == 0)
    def _():
        h_scr[...] = _rms(x_ref[...], g_ref[...]).astype(BF16)
        o_ref[...] = _dot_nt(h_scr[...], wt_ref[...])

    @pl.when(j > 0)
    def _():
        o_ref[...] = _dot_nt(h_scr[...], wm_ref[...])


def _in_proj(x, g, w_tail_t, w_in_t, layer, tm):
    n_tok = x.shape[0]
    tn = IN_TILE
    first = (N_BRANCH * D_MODEL) // tn
    n_main = (2 * A_WIDTH + B_INNER + B_CONV_DIM) // tn
    assert (N_BRANCH * D_MODEL) % tn == 0 and 2 * A_WIDTH == tn and B_INNER % tn == 0 and B_CONV_DIM % tn == 0
    assert D_IN_PAD == (1 + n_main) * tn
    main_block = lambda i, j: (layer, first + jnp.where(j == 0, 1, j % n_main), 0)
    return pl.pallas_call(
        _in_proj_kernel,
        grid=(n_tok // tm, 1 + n_main),
        in_specs=[
            pl.BlockSpec((tm, D_MODEL), lambda i, j: (i, 0)),
            pl.BlockSpec((1, D_MODEL), lambda i, j: (0, 0)),
            pl.BlockSpec((tn, D_MODEL), lambda i, j: (0, 0)),
            pl.BlockSpec((None, tn, D_MODEL), main_block),
        ],
        out_specs=pl.BlockSpec((tm, tn), lambda i, j: (i, j)),
        out_shape=jax.ShapeDtypeStruct((n_tok, D_IN_PAD), F32),
        scratch_shapes=[pltpu.VMEM((tm, D_MODEL), BF16)],
        compiler_params=_cparams(("parallel", "arbitrary")),
        name="in_proj",
    )(x, g, w_tail_t, w_in_t)


def _gate_a_kernel(u_ref, v_ref, lnw_ref, lnb_ref, ws_ref, bs_ref, ya_ref, av_ref, *, n_chunks):
    row = lax.broadcasted_iota(jnp.int32, (CHUNK, CHUNK), 0)
    col = lax.broadcasted_iota(jnp.int32, (CHUNK, CHUNK), 1)
    causal = col <= row
    gd = A_WIDTH // A_GROUPS
    for c in range(n_chunks):
        rows = pl.ds(c * CHUNK, CHUNK)
        v = _gelu(v_ref[rows, :])
        mu = jnp.mean(v, axis=-1, keepdims=True)
        vc = v - mu
        var = jnp.mean(vc * vc, axis=-1, keepdims=True)
        av = vc * lax.rsqrt(var + NORM_EPS) * lnw_ref[...] + lnb_ref[...]
        av_ref[rows, :] = av
        u = _gelu(u_ref[rows, :])
        for g in range(A_GROUPS):
            w = jnp.where(causal, ws_ref[g], 0.0).astype(BF16)
            s = _dot(w, av[:, g * gd:(g + 1) * gd].astype(BF16)) + bs_ref[g]
            ya_ref[rows, g * gd:(g + 1) * gd] = (u[:, g * gd:(g + 1) * gd] * s).astype(BF16)


def _gate_a(proj, lnw, lnb, ws, bs, tm):
    n_tok = proj.shape[0]
    kern = functools.partial(_gate_a_kernel, n_chunks=tm // CHUNK)
    return pl.pallas_call(
        kern,
        grid=(n_tok // tm,),
        in_specs=[
            pl.BlockSpec((tm, A_WIDTH), lambda i: (i, COL_AU // A_WIDTH)),
            pl.BlockSpec((tm, A_WIDTH), lambda i: (i, COL_AV // A_WIDTH)),
            pl.BlockSpec((1, A_WIDTH), lambda i: (0, 0)),
            pl.BlockSpec((1, A_WIDTH), lambda i: (0, 0)),
            pl.BlockSpec((A_GROUPS, CHUNK, CHUNK), lambda i: (0, 0, 0)),
            pl.BlockSpec((A_GROUPS, CHUNK, 1), lambda i: (0, 0, 0)),
        ],
        out_specs=[
            pl.BlockSpec((tm, A_WIDTH), lambda i: (i, 0)),
            pl.BlockSpec((tm, A_WIDTH), lambda i: (i, 0)),
        ],
        out_shape=[
            jax.ShapeDtypeStruct((n_tok, A_WIDTH), BF16),
            jax.ShapeDtypeStruct((n_tok, A_WIDTH), F32),
        ],
        compiler_params=_cparams(("parallel",)),
        name="gate_a",
    )(proj, proj, lnw, lnb, ws, bs)


def _ssd_prompt_kernel(xbc_ref, z_ref, dt_ref, cw_ref, cb_ref, dtb_ref, alog_ref, dsk_ref, bn_ref, e_ref,
                       yb_ref, st_ref, xp_scr, h_scr, y_scr, xde_scr):
    L, P, N = CHUNK, B_HEADDIM, B_STATE
    i = pl.program_id(0)

    @pl.when(i == 0)
    def _():
        xp_scr[0:8, :] = jnp.zeros((8, B_CONV_DIM), F32)
        h_scr[...] = jnp.zeros_like(h_scr)

    xp_scr[8:8 + L, :] = xbc_ref[...]
    acc = xp_scr[8:8 + L, :] * cw_ref[B_CONV - 1:B_CONV, :]
    for sh in range(1, B_CONV):
        acc = acc + xp_scr[8 - sh:8 - sh + L, :] * cw_ref[B_CONV - 1 - sh:B_CONV - sh, :]
    y_scr[...] = _silu(acc + cb_ref[...])
    xp_scr[0:8, :] = xp_scr[L:L + 8, :]

    lane = lax.broadcasted_iota(jnp.int32, (1, 128), 1)
    a_row = jnp.where(lane < B_HEADS, -jnp.exp(alog_ref[...]), 0.0)
    dt = _softplus(dt_ref[...] + dtb_ref[...])
    da = dt * a_row
    row = lax.broadcasted_iota(jnp.int32, (L, L), 0)
    col = lax.broadcasted_iota(jnp.int32, (L, L), 1)
    causal = col <= row
    da3 = _split3(da, 1)
    cs = _sum3(_dot(jnp.where(causal, 1.0, 0.0).astype(BF16), da3), 1) * LOG2E
    cs_t = _sum3(_dot_tn(da3, jnp.where(row <= col, 1.0, 0.0).astype(BF16)), 0) * LOG2E
    cs_last = cs[L - 1:L, :]
    dec_e = _sum3(_dot(_split3(jnp.broadcast_to(jnp.exp2(cs_last), (8, 128)), 0), e_ref[...]), 0)[0:1, :]

    rep = B_HEADS // B_GROUPS
    gw = rep * P
    for g in range(B_GROUPS):
        bg = y_scr[:, B_INNER + g * N:B_INNER + (g + 1) * N].astype(BF16)
        cg = y_scr[:, B_INNER + (B_GROUPS + g) * N:B_INNER + (B_GROUPS + g + 1) * N].astype(BF16)
        scores = _dot_nt(cg, bg)
        hg = h_scr[g]
        yo = _dot(cg, hg.astype(BF16))
        for pr in range(rep // 2):
            ha = g * rep + 2 * pr
            cols = slice(ha * P, (ha + 2) * P)
            col_a = jnp.broadcast_to(cs[:, ha:ha + 1], (L, L))
            col_b = jnp.broadcast_to(cs[:, ha + 1:ha + 2], (L, L))
            m_a = scores * jnp.exp2(jnp.where(causal, col_a - cs_t[ha:ha + 1, :], -jnp.inf))
            m_b = scores * jnp.exp2(jnp.where(causal, col_b - cs_t[ha + 1:ha + 2, :], -jnp.inf))
            first = lane < P
            col2 = jnp.where(first, col_a, col_b)
            dt2 = jnp.where(first, dt[:, ha:ha + 1], dt[:, ha + 1:ha + 2])
            last2 = jnp.where(first, cs_last[:, ha:ha + 1], cs_last[:, ha + 1:ha + 2])
            xs2 = y_scr[:, cols]
            xdt = xs2 * dt2
            xdt_b = xdt.astype(BF16)
            zero = jnp.zeros_like(xdt_b)
            x_diag = jnp.concatenate([jnp.where(first, xdt_b, zero), jnp.where(first, zero, xdt_b)], axis=0)
            yd = _dot(jnp.concatenate([m_a.astype(BF16), m_b.astype(BF16)], axis=1), x_diag)
            y_scr[:, cols] = yd + jnp.exp2(col2) * yo[:, 2 * pr * P:(2 * pr + 2) * P] + dsk_ref[:, cols] * xs2
            xde_scr[:, cols] = (xdt * jnp.exp2(last2 - col2)).astype(BF16)
        st = _dot_tn(bg, xde_scr[:, g * gw:(g + 1) * gw])
        h_scr[g] = dec_e[:, g * gw:(g + 1) * gw] * hg + st

    y = y_scr[:, 0:B_INNER] * _silu(z_ref[...])
    yb_ref[...] = _rms(y, bn_ref[...]).astype(BF16)
    st_ref[...] = h_scr[...]


def _ssd_prompt(proj, cw, cb, dtb128, alog128, dsk_e, bnorm, e_mat):
    S = proj.shape[0]
    gw = (B_HEADS // B_GROUPS) * B_HEADDIM
    return pl.pallas_call(
        _ssd_prompt_kernel,
        grid=(S // CHUNK,),
        in_specs=[
            pl.BlockSpec((CHUNK, B_CONV_DIM), lambda i: (i, COL_XBC // B_CONV_DIM)),
            pl.BlockSpec((CHUNK, B_INNER), lambda i: (i, COL_Z // B_INNER)),
            pl.BlockSpec((CHUNK, 128), lambda i: (i, COL_DT // 128)),
            pl.BlockSpec((B_CONV, B_CONV_DIM), lambda i: (0, 0)),
            pl.BlockSpec((1, B_CONV_DIM), lambda i: (0, 0)),
            pl.BlockSpec((1, 128), lambda i: (0, 0)),
            pl.BlockSpec((1, 128), lambda i: (0, 0)),
            pl.BlockSpec((1, B_INNER), lambda i: (0, 0)),
            pl.BlockSpec((1, B_INNER), lambda i: (0, 0)),
            pl.BlockSpec((128, B_INNER), lambda i: (0, 0)),
        ],
        out_specs=[
            pl.BlockSpec((CHUNK, B_INNER), lambda i: (i, 0)),
            pl.BlockSpec((B_GROUPS, B_STATE, gw), lambda i: (0, 0, 0)),
        ],
        out_shape=[
            jax.ShapeDtypeStruct((S, B_INNER), BF16),
            jax.ShapeDtypeStruct((B_GROUPS, B_STATE, gw), F32),
        ],
        scratch_shapes=[
            pltpu.VMEM((CHUNK + 8, B_CONV_DIM), F32),
            pltpu.VMEM((B_GROUPS, B_STATE, gw), F32),
            pltpu.VMEM((CHUNK, B_CONV_DIM), F32),
            pltpu.VMEM((CHUNK, B_INNER), BF16),
        ],
        compiler_params=_cparams(("arbitrary",)),
        name="ssd_prompt",
    )(proj, proj, proj, cw, cb, dtb128, alog128, dsk_e, bnorm, e_mat)


def _ssd_sample_kernel(xbc_ref, cprev_ref, z_ref, dt_ref, h0_ref, cw_ref, cb_ref, dtb_ref, alog_ref,
                       dsk_ref, bn_ref, e_ref, bm_ref, s_ref,
                       yb_ref, hn_ref, xp_scr, r16_scr, prod_scr, ex_scr, a_scr, b_scr, *, T, seqs):
    for e in range(seqs):
        _ssd_sample_one(e, xbc_ref, cprev_ref, z_ref, dt_ref, h0_ref, cw_ref, cb_ref, dtb_ref, alog_ref,
                        dsk_ref, bn_ref, e_ref, bm_ref, s_ref, yb_ref, hn_ref,
                        xp_scr.at[e], r16_scr.at[e], prod_scr.at[e], ex_scr.at[e], a_scr.at[e], b_scr.at[e], T)


def _ssd_sample_one(e, xbc_ref, cprev_ref, z_ref, dt_ref, h0_ref, cw_ref, cb_ref, dtb_ref, alog_ref,
                    dsk_ref, bn_ref, e_ref, bm_ref, s_ref,
                    yb_ref, hn_ref, xp_scr, r16_scr, prod_scr, ex_scr, a_scr, b_scr, T):
    H, P, N, G = B_HEADS, B_HEADDIM, B_STATE, B_GROUPS
    rep = H // G
    pairs = [(t, s) for t in range(T) for s in range(t + 1)]
    xp_scr[0:B_CONV - 1, :] = cprev_ref[e]
    xp_scr[B_CONV - 1:B_CONV - 1 + T, :] = xbc_ref[e]
    acc = xp_scr[0:T, :] * cw_ref[0:1, :]
    for k in range(1, B_CONV):
        acc = acc + xp_scr[k:k + T, :] * cw_ref[k:k + 1, :]
    xc = _silu(acc + cb_ref[...])
    xs = xc[:, 0:B_INNER]
    bm = xc[:, B_INNER:B_INNER + G * N]
    cm = xc[:, B_INNER + G * N:]

    lane = lax.broadcasted_iota(jnp.int32, (1, 128), 1)
    a_row = jnp.where(lane < H, -jnp.exp(alog_ref[...]), 0.0)
    dt = _softplus(dt_ref[e] + dtb_ref[...])
    da = dt * a_row
    cs_rows = [da[0:1, :]]
    for t in range(1, T):
        cs_rows.append(cs_rows[-1] + da[t:t + 1, :])
    cs_last = cs_rows[-1]

    prod_scr[...] = jnp.zeros_like(prod_scr)
    for idx, (t, s) in enumerate(pairs):
        prod_scr[idx:idx + 1, :] = cm[t:t + 1, :] * bm[s:s + 1, :]
    gh = _sum3(_dot(_split3(prod_scr[...], 0), s_ref[...]), 0)
    ex_scr[...] = jnp.zeros_like(ex_scr)
    ex_scr[0:T, :] = dt
    for t in range(T):
        ex_scr[T + t:T + t + 1, :] = jnp.exp(cs_rows[t])
    for idx, (t, s) in enumerate(pairs):
        ex_scr[2 * T + idx:2 * T + idx + 1, :] = gh[idx:idx + 1, :] * jnp.exp(cs_rows[t] - cs_rows[s])
    ex = _sum3(_dot(_split3(ex_scr[...], 0), e_ref[...]), 0)
    xdt = xs * ex[0:T, :]

    for g in range(G):
        r16_scr[g * T:(g + 1) * T, :] = cm[:, g * N:(g + 1) * N]
    h2d = h0_ref[e].reshape(H * P, N)
    r = _dot_nt(r16_scr[...].astype(BF16), h2d.astype(BF16))
    gw = rep * P
    y_rows = []
    for t in range(T):
        yo = jnp.concatenate([r[g * T + t:g * T + t + 1, g * gw:(g + 1) * gw] for g in range(G)], axis=1)
        y_rows.append(yo * ex[T + t:T + t + 1, :])
    for idx, (t, s) in enumerate(pairs):
        y_rows[t] = y_rows[t] + ex[2 * T + idx:2 * T + idx + 1, :] * xdt[s:s + 1, :]
    for t in range(T):
        y = y_rows[t] + dsk_ref[...] * xs[t:t + 1, :]
        y = y * _silu(z_ref[e, t:t + 1, :])
        yb_ref[e, t:t + 1, :] = _rms(y, bn_ref[...]).astype(BF16)

    eye = (lax.broadcasted_iota(jnp.int32, (H, 128), 0) == lax.broadcasted_iota(jnp.int32, (H, 128), 1)).astype(F32)

    def to_col(v):
        return jnp.sum(jnp.broadcast_to(v, (H, 128)) * eye, axis=1, keepdims=True)

    a_scr[...] = jnp.zeros_like(a_scr)
    b_scr[...] = jnp.zeros_like(b_scr)
    for t in range(T):
        dcol = to_col(jnp.exp(cs_last - cs_rows[t]))
        a_scr[t * H:(t + 1) * H, :] = jnp.broadcast_to(xdt[t:t + 1, :], (H, H * P)) * bm_ref[...]
        for g in range(G):
            b_scr[t * H + g * rep:t * H + (g + 1) * rep, :] = (
                jnp.broadcast_to(bm[t:t + 1, g * N:(g + 1) * N], (rep, N)) * dcol[g * rep:(g + 1) * rep, :])
    st = _dot_tn(a_scr[...].astype(BF16), b_scr[...].astype(BF16))
    dfull = jnp.broadcast_to(to_col(jnp.exp(cs_last)), (H, N))
    for h in range(H):
        hn_ref[e, h] = dfull[h:h + 1, :] * h0_ref[e, h] + st[h * P:(h + 1) * P, :]


def _ssd_sample(xbc, cprev, z, dtb, h0, layer, cw, cb, dtb128, alog128, dsk_e, bnorm, e_mat, blockmask, s_mat):
    B, T, _ = xbc.shape
    n_pair = -(-(T * (T + 1) // 2) // 8) * 8
    assert B_CONV - 1 + T <= 8 and T * B_HEADS <= 128
    nseq = SSD_SAMPLE_SEQS if B % SSD_SAMPLE_SEQS == 0 else 1
    kern = functools.partial(_ssd_sample_kernel, T=T, seqs=nseq)
    c2 = lambda b: (0, 0)
    return pl.pallas_call(
        kern,
        grid=(B // nseq,),
        in_specs=[
            pl.BlockSpec((nseq, T, B_CONV_DIM), lambda b: (b, 0, 0)),
            pl.BlockSpec((None, nseq, B_CONV - 1, B_CONV_DIM), lambda b: (layer, b, 0, 0)),
            pl.BlockSpec((nseq, T, B_INNER), lambda b: (b, 0, 0)),
            pl.BlockSpec((nseq, T, 128), lambda b: (b, 0, 0)),
            pl.BlockSpec((None, nseq, B_HEADS, B_HEADDIM, B_STATE), lambda b: (layer, b, 0, 0, 0)),
            pl.BlockSpec((B_CONV, B_CONV_DIM), c2),
            pl.BlockSpec((1, B_CONV_DIM), c2),
            pl.BlockSpec((1, 128), c2),
            pl.BlockSpec((1, 128), c2),
            pl.BlockSpec((1, B_INNER), c2),
            pl.BlockSpec((1, B_INNER), c2),
            pl.BlockSpec((128, B_INNER), c2),
            pl.BlockSpec((B_HEADS, B_INNER), c2),
            pl.BlockSpec((B_GROUPS * B_STATE, 128), c2),
        ],
        out_specs=[
            pl.BlockSpec((nseq, T, B_INNER), lambda b: (b, 0, 0)),
            pl.BlockSpec((nseq, B_HEADS, B_HEADDIM, B_STATE), lambda b: (b, 0, 0, 0)),
        ],
        out_shape=[
            jax.ShapeDtypeStruct((B, T, B_INNER), BF16),
            jax.ShapeDtypeStruct((B, B_HEADS, B_HEADDIM, B_STATE), F32),
        ],
        scratch_shapes=[
            pltpu.VMEM((nseq, 8, B_CONV_DIM), F32),
            pltpu.VMEM((nseq, B_GROUPS * T, B_STATE), F32),
            pltpu.VMEM((nseq, n_pair, B_GROUPS * B_STATE), F32),
            pltpu.VMEM((nseq, 2 * T + n_pair, 128), F32),
            pltpu.VMEM((nseq, 128, B_INNER), F32),
            pltpu.VMEM((nseq, 128, B_STATE), F32),
        ],
        compiler_params=_cparams(("parallel",)),
        name="ssd_sample",
    )(xbc, cprev, z, dtb, h0, cw, cb, dtb128, alog128, dsk_e, bnorm, e_mat, blockmask, s_mat)


def _c_prep_kernel(cq_ref, ckv_ref, kr_ref, krs_ref, cos_ref, sin_ref, qn_ref, kvn_ref,
                   wq1_ref, wq2_ref, wk_ref, wvt_ref,
                   q_ref, k_ref, vt_ref, ckvn_ref, krope_ref):
    cos = cos_ref[...]
    sin = sin_ref[...]
    cos8 = jnp.concatenate([cos] * C_HEADS, axis=1)
    sin8 = jnp.concatenate([sin] * C_HEADS, axis=1)
    cqn = _rms(cq_ref[...], qn_ref[...]).astype(BF16)
    q = _dot(cqn, wq1_ref[...]) * cos8 + _dot(cqn, wq2_ref[...]) * sin8
    q_ref[...] = (q * (C_SCALE * LOG2E)).astype(BF16)
    ckvn = _rms(ckv_ref[...], kvn_ref[...])
    ckvn_ref[...] = ckvn
    k128 = kr_ref[...] * cos + krs_ref[...] * sin
    krope_ref[...] = k128[:, C_NOPE:C_NOPE + C_ROPE]
    cb = ckvn.astype(BF16)
    k_ref[...] = (_dot(cb, wk_ref[...]) + jnp.concatenate([k128] * C_HEADS, axis=1)).astype(BF16)
    row = lax.broadcasted_iota(jnp.int32, (V_ROWS, 1), 0)
    for h in range(C_HEADS):
        vt = _dot_nt(wvt_ref[h], cb)
        vt_ref[h * V_ROWS:(h + 1) * V_ROWS, :] = jnp.where(row == C_V, 1.0, vt).astype(BF16)


def _c_prep(proj, cos, sin, qn, kvn, wq1, wq2, wk, wv, tm):
    n_tok = proj.shape[0]
    c2 = lambda i: (0, 0)
    hq = C_HEADS * HEAD_PAD
    return pl.pallas_call(
        _c_prep_kernel,
        grid=(n_tok // tm,),
        in_specs=[
            pl.BlockSpec((tm, C_Q_LORA), lambda i: (i, COL_CQ // C_Q_LORA)),
            pl.BlockSpec((tm, C_KV_LORA), lambda i: (i, COL_CKV // C_KV_LORA)),
            pl.BlockSpec((tm, 128), lambda i: (i, COL_KR // 128)),
            pl.BlockSpec((tm, 128), lambda i: (i, COL_KRS // 128)),
            pl.BlockSpec((tm, 128), lambda i: (i, 0)),
            pl.BlockSpec((tm, 128), lambda i: (i, 0)),
            pl.BlockSpec((1, C_Q_LORA), c2),
            pl.BlockSpec((1, C_KV_LORA), c2),
            pl.BlockSpec((C_Q_LORA, hq), c2),
            pl.BlockSpec((C_Q_LORA, hq), c2),
            pl.BlockSpec((C_KV_LORA, hq), c2),
            pl.BlockSpec((C_HEADS, V_ROWS, C_KV_LORA), lambda i: (0, 0, 0)),
        ],
        out_specs=[
            pl.BlockSpec((tm, hq), lambda i: (i, 0)),
            pl.BlockSpec((tm, hq), lambda i: (i, 0)),
            pl.BlockSpec((C_HEADS * V_ROWS, tm), lambda i: (0, i)),
            pl.BlockSpec((tm, C_KV_LORA), lambda i: (i, 0)),
            pl.BlockSpec((tm, C_ROPE), lambda i: (i, 0)),
        ],
        out_shape=[
            jax.ShapeDtypeStruct((n_tok, hq), BF16),
            jax.ShapeDtypeStruct((n_tok, hq), BF16),
            jax.ShapeDtypeStruct((C_HEADS * V_ROWS, n_tok), BF16),
            jax.ShapeDtypeStruct((n_tok, C_KV_LORA), F32),
            jax.ShapeDtypeStruct((n_tok, C_ROPE), F32),
        ],
        compiler_params=_cparams(("parallel",)),
        name="c_prep",
    )(proj, proj, proj, proj, cos, sin, qn, kvn, wq1, wq2, wk, wv)


def _flash_kernel(qi_ref, kj_ref, q_ref, k_ref, vt_ref, o_ref, m_scr, acc_scr, s_scr, *, tq, tk, hps):
    s_idx = pl.program_id(1)
    qi = qi_ref[s_idx]
    kj = kj_ref[s_idx]

    @pl.when(kj == 0)
    def _():
        m_scr[...] = jnp.full_like(m_scr, -jnp.inf)
        acc_scr[...] = jnp.zeros_like(acc_scr)

    def step(masked):
        if masked:
            visible = (lax.broadcasted_iota(jnp.int32, (tk, tq), 0)
                       <= lax.broadcasted_iota(jnp.int32, (tk, tq), 1))

        def scores(hh):
            lanes = slice(hh * HEAD_PAD, (hh + 1) * HEAD_PAD)
            s = _dot_nt(k_ref[:, lanes], q_ref[:, lanes])
            if masked:
                s = jnp.where(visible, s, -jnp.inf)
            s_scr[hh % (FLASH_LOOKAHEAD + 1)] = s

        for hh in range(min(FLASH_LOOKAHEAD, hps)):
            scores(hh)
        for hh in range(hps):
            if hh + FLASH_LOOKAHEAD < hps:
                scores(hh + FLASH_LOOKAHEAD)
            s_tile = s_scr.at[hh % (FLASH_LOOKAHEAD + 1)]
            m_prev = m_scr[hh]
            m_new = jnp.maximum(m_prev, jnp.max(s_tile[...], axis=0, keepdims=True))
            m_scr[hh] = m_new
            alpha = jnp.exp2(m_prev[0:1, :] - m_new[0:1, :])
            p = jnp.exp2(s_tile[...] - m_new[0:1, :]).astype(BF16)
            acc_scr[hh] = alpha * acc_scr[hh] + _dot(vt_ref[hh * V_ROWS:(hh + 1) * V_ROWS, :], p)

    @pl.when(kj < qi)
    def _():
        step(False)

    @pl.when(kj == qi)
    def _():
        step(True)
        for hh in range(hps):
            a = acc_scr[hh]
            o_ref[hh * C_V:(hh + 1) * C_V, :] = (a[0:C_V, :] / a[C_V:C_V + 1, :]).astype(BF16)


def _flash(q, k, vt, tq, hps):
    S = q.shape[0]
    tk = tq
    nq = S // tq
    qi = np.concatenate([np.full(i + 1, i, np.int32) for i in range(nq)])
    kj = np.concatenate([np.arange(i + 1, dtype=np.int32) for i in range(nq)])
    kern = functools.partial(_flash_kernel, tq=tq, tk=tk, hps=hps)
    grid_spec = pltpu.PrefetchScalarGridSpec(
        num_scalar_prefetch=2,
        grid=(C_HEADS // hps, int(qi.shape[0])),
        in_specs=[
            pl.BlockSpec((tq, hps * HEAD_PAD), lambda p, s, qi, kj: (qi[s], p)),
            pl.BlockSpec((tk, hps * HEAD_PAD), lambda p, s, qi, kj: (kj[s], p)),
            pl.BlockSpec((hps * V_ROWS, tk), lambda p, s, qi, kj: (p, kj[s])),
        ],
        out_specs=pl.BlockSpec((hps * C_V, tq), lambda p, s, qi, kj: (p, qi[s])),
        scratch_shapes=[
            pltpu.VMEM((hps, 8, tq), F32),
            pltpu.VMEM((hps, V_ROWS, tq), F32),
            pltpu.VMEM((FLASH_LOOKAHEAD + 1, tk, tq), F32),
        ],
    )
    return pl.pallas_call(
        kern,
        grid_spec=grid_spec,
        out_shape=jax.ShapeDtypeStruct((C_HEADS * C_V, S), BF16),
        compiler_params=_cparams(("parallel", "arbitrary")),
        name="flash_prompt",
    )(jnp.asarray(qi), jnp.asarray(kj), q, k, vt)


def _q_lat_kernel(q_ref, wt_ref, o_ref):
    o_ref[0] = _dot(q_ref[...], wt_ref[0]).astype(BF16)


def _q_lat(q, wukt):
    n = q.shape[0]
    return pl.pallas_call(
        _q_lat_kernel,
        grid=(C_HEADS,),
        in_specs=[
            pl.BlockSpec((n, HEAD_PAD), lambda h: (0, h)),
            pl.BlockSpec((1, HEAD_PAD, C_KV_LORA), lambda h: (h, 0, 0)),
        ],
        out_specs=pl.BlockSpec((1, n, C_KV_LORA), lambda h: (h, 0, 0)),
        out_shape=jax.ShapeDtypeStruct((C_HEADS, n, C_KV_LORA), BF16),
        compiler_params=_cparams(("parallel",)),
        name="q_lat",
    )(q, wukt)


def _attn_sample_kernel(pt_ref, qlat_ref, q128_ref, ckv_ref, krn_ref, kv_hbm, kr_hbm, o_ref,
                        kvbuf, krbuf, sem, newkv_scr, newkr_scr, *, layer, T, n_pages, cp, n_streams):
    b = pl.program_id(0)
    nb = pl.num_programs(0)
    n_chunks = n_pages // cp
    R = T * C_HEADS
    spp = cp // n_streams
    n_slots = ATTN_SLOTS
    ahead = n_slots - 1

    def copies(bb, c):
        slot = c % n_slots
        out = []
        for i in range(cp):
            page = pt_ref[bb, c * cp + i]
            out.append(pltpu.make_async_copy(kv_hbm.at[layer, page], kvbuf.at[slot, i], sem.at[0, slot]))
            out.append(pltpu.make_async_copy(kr_hbm.at[layer, page], krbuf.at[slot, i], sem.at[1, slot]))
        return out

    def start(bb, c):
        for cpy in copies(bb, c):
            cpy.start()

    def wait(bb, c):
        for cpy in copies(bb, c):
            cpy.wait()

    @pl.when(b == 0)
    def _():
        for c in range(ahead):
            start(0, c)

    b_next = jnp.minimum(b + 1, nb - 1)

    qlat = qlat_ref[0]
    qr = q128_ref[0][:, C_NOPE:C_NOPE + C_ROPE]

    def softmax_update(state, s):
        m_prev, l_prev, _ = state
        m_new = jnp.maximum(m_prev, jnp.max(s, axis=-1, keepdims=True))
        alpha = jnp.exp2(m_prev - m_new)
        p = jnp.exp2(s - m_new)
        l_new = alpha * l_prev + jnp.sum(p, axis=-1, keepdims=True)
        return m_new, l_new, alpha, p.astype(BF16)

    def online(state, s, kv):
        m_new, l_new, alpha, p = softmax_update(state, s)
        return m_new, l_new, state[2] * alpha + _dot(p, kv)

    def chunk(c, carry):
        slot = c % n_slots
        wait(b, c)
        kvs, scores = [], []
        for si in range(n_streams):
            kv = kvbuf[slot, si * spp:(si + 1) * spp].reshape(spp * PAGE_SIZE, C_KV_LORA).astype(BF16)
            kr_t = jnp.concatenate([krbuf[slot, si * spp + i] for i in range(spp)], axis=1).astype(BF16)
            kvs.append(kv)
            scores.append(_dot_nt(qlat, kv) + _dot(qr, kr_t))
        if c + ahead < n_chunks:
            start(b, c + ahead)
        else:
            start(b_next, c + ahead - n_chunks)
        stats = [softmax_update(carry[si], scores[si]) for si in range(n_streams)]
        return tuple((m_new, l_new, carry[si][2] * alpha + _dot(p, kvs[si]))
                     for si, (m_new, l_new, alpha, p) in enumerate(stats))

    streams = tuple((jnp.full((R, 1), -jnp.inf, F32), jnp.zeros((R, 1), F32), jnp.zeros((R, C_KV_LORA), F32))
                    for _ in range(n_streams))
    for c in range(n_chunks):
        streams = chunk(c, streams)

    @pl.when(b == nb - 1)
    def _():
        for c in range(ahead):
            wait(b_next, c)

    newkv_scr[...] = jnp.zeros_like(newkv_scr)
    newkr_scr[...] = jnp.zeros_like(newkr_scr)
    newkv_scr[0:T, :] = ckv_ref[0]
    newkr_scr[0:T, :] = krn_ref[0]
    kvn = newkv_scr[...].astype(BF16)
    krn = newkr_scr[...].astype(BF16)
    s = _dot_nt(qlat, kvn) + _dot_nt(qr, krn)
    t_row = lax.broadcasted_iota(jnp.int32, (R, 128), 0) % T
    key = lax.broadcasted_iota(jnp.int32, (R, 128), 1)
    s = jnp.where(key <= t_row, s, -jnp.inf)
    m_all, l_all, acc_all = online(streams[0], s, kvn)
    for m_i, l_i, acc_i in streams[1:]:
        m_new = jnp.maximum(m_all, m_i)
        wa = jnp.exp2(m_all - m_new)
        wi = jnp.exp2(m_i - m_new)
        l_all = wa * l_all + wi * l_i
        acc_all = wa * acc_all + wi * acc_i
        m_all = m_new
    o_ref[0] = (acc_all / l_all).astype(BF16)


def _attn_sample(page_table, qlat, q128, ckv_new, kr_new, cache_kv, cache_kr_t, layer, cp, n_streams):
    B, R, _ = qlat.shape
    T = ckv_new.shape[1]
    n_pages = page_table.shape[1]
    assert (n_pages // cp) % ATTN_SLOTS == 0 and cp % n_streams == 0
    kern = functools.partial(_attn_sample_kernel, layer=layer, T=T, n_pages=n_pages, cp=cp, n_streams=n_streams)
    grid_spec = pltpu.PrefetchScalarGridSpec(
        num_scalar_prefetch=1,
        grid=(B,),
        in_specs=[
            pl.BlockSpec((1, R, C_KV_LORA), lambda b, pt: (b, 0, 0)),
            pl.BlockSpec((1, R, HEAD_PAD), lambda b, pt: (b, 0, 0)),
            pl.BlockSpec((1, T, C_KV_LORA), lambda b, pt: (b, 0, 0)),
            pl.BlockSpec((1, T, C_ROPE), lambda b, pt: (b, 0, 0)),
            pl.BlockSpec(memory_space=pl.ANY),
            pl.BlockSpec(memory_space=pl.ANY),
        ],
        out_specs=pl.BlockSpec((1, R, C_KV_LORA), lambda b, pt: (b, 0, 0)),
        scratch_shapes=[
            pltpu.VMEM((ATTN_SLOTS, cp, PAGE_SIZE, C_KV_LORA), F32),
            pltpu.VMEM((ATTN_SLOTS, cp, C_ROPE, PAGE_SIZE), F32),
            pltpu.SemaphoreType.DMA((2, ATTN_SLOTS)),
            pltpu.VMEM((128, C_KV_LORA), F32),
            pltpu.VMEM((128, C_ROPE), F32),
        ],
    )
    return pl.pallas_call(
        kern,
        grid_spec=grid_spec,
        out_shape=jax.ShapeDtypeStruct((B, R, C_KV_LORA), BF16),
        compiler_params=_cparams(("arbitrary",)),
        name="attn_sample",
    )(page_table, qlat, q128, ckv_new, kr_new, cache_kv, cache_kr_t)


def _uv_proj_kernel(o_ref, wt_ref, y_ref):
    y_ref[0] = _dot_nt(wt_ref[0], o_ref[0]).astype(BF16)


def _uv_proj(olat, wuvt):
    H, n, _ = olat.shape
    return pl.pallas_call(
        _uv_proj_kernel,
        grid=(H,),
        in_specs=[
            pl.BlockSpec((1, n, C_KV_LORA), lambda h: (h, 0, 0)),
            pl.BlockSpec((1, C_V, C_KV_LORA), lambda h: (h, 0, 0)),
        ],
        out_specs=pl.BlockSpec((1, C_V, n), lambda h: (h, 0, 0)),
        out_shape=jax.ShapeDtypeStruct((H, C_V, n), BF16),
        compiler_params=_cparams(("parallel",)),
        name="uv_proj",
    )(olat, wuvt)


def _merge_kernel(x_ref, g_ref, wg_ref, ya_ref, yb_ref, yct_ref, wpa_ref, wpb_ref, wpc_ref, wo_ref, o_ref):
    x = x_ref[...]
    gates = _dot_nt(_rms(x, g_ref[...]).astype(BF16), wg_ref[...])
    m = _sigmoid(gates[:, 0:D_MODEL]) * _dot(ya_ref[...], wpa_ref[...])
    m = m + _sigmoid(gates[:, D_MODEL:2 * D_MODEL]) * _dot(yb_ref[...], wpb_ref[...])
    m = m + _sigmoid(gates[:, 2 * D_MODEL:]) * _dot_tn(yct_ref[...], wpc_ref[...])
    o_ref[...] = x + _dot(m.astype(BF16), wo_ref[...])


def _merge(x, g_mix, w_in_t, layer, ya, yb, yct, wpa, wpb, wpc, wo, tm):
    n_tok = x.shape[0]
    c2 = lambda i: (0, 0)
    return pl.pallas_call(
        _merge_kernel,
        grid=(n_tok // tm,),
        in_specs=[
            pl.BlockSpec((tm, D_MODEL), lambda i: (i, 0)),
            pl.BlockSpec((1, D_MODEL), c2),
            pl.BlockSpec((None, N_BRANCH * D_MODEL, D_MODEL), lambda i: (layer, 0, 0)),
            pl.BlockSpec((tm, A_WIDTH), lambda i: (i, 0)),
            pl.BlockSpec((tm, B_INNER), lambda i: (i, 0)),
            pl.BlockSpec((C_HEADS * C_V, tm), lambda i: (0, i)),
            pl.BlockSpec((A_WIDTH, D_MODEL), c2),
            pl.BlockSpec((B_INNER, D_MODEL), c2),
            pl.BlockSpec((C_HEADS * C_V, D_MODEL), c2),
            pl.BlockSpec((D_MODEL, D_MODEL), c2),
        ],
        out_specs=pl.BlockSpec((tm, D_MODEL), lambda i: (i, 0)),
        out_shape=jax.ShapeDtypeStruct((n_tok, D_MODEL), F32),
        compiler_params=_cparams(("parallel",)),
        name="merge",
    )(x, g_mix, w_in_t, ya, yb, yct, wpa, wpb, wpc, wo)


def _ffn_ple_kernel(x_ref, g_ref, wu_ref, wd_ref, gp_ref, wg_ref, p_ref, wp_ref, gf_ref, o_ref,
                    h_scr, acc_scr, *, final):
    j = pl.program_id(1)

    @pl.when(j == 0)
    def _():
        x = x_ref[...]
        h_scr[...] = _rms(x, g_ref[...]).astype(BF16)
        acc_scr[...] = x

    u = jnp.maximum(_dot(h_scr[...], wu_ref[...]), 0.0)
    acc_scr[...] += _dot((u * u).astype(BF16), wd_ref[...])

    @pl.when(j == pl.num_programs(1) - 1)
    def _():
        x = acc_scr[...]
        pg = _sigmoid(_dot(_rms(x, gp_ref[...]).astype(BF16), wg_ref[...]))
        y = x + pg * _dot(p_ref[...].astype(BF16), wp_ref[...])
        if final:
            y = _rms(y, gf_ref[...])
        o_ref[...] = y


def _ffn_ple(x, g, wu, wd, gp, wg, p, layer, wp, gf, tm, tf, final):
    n_tok = x.shape[0]
    d_ple = p.shape[2]
    c2 = lambda i, j: (0, 0)
    return pl.pallas_call(
        functools.partial(_ffn_ple_kernel, final=final),
        grid=(n_tok // tm, D_FF // tf),
        in_specs=[
            pl.BlockSpec((tm, D_MODEL), lambda i, j: (i, 0)),
            pl.BlockSpec((1, D_MODEL), c2),
            pl.BlockSpec((D_MODEL, tf), lambda i, j: (0, j)),
            pl.BlockSpec((tf, D_MODEL), lambda i, j: (j, 0)),
            pl.BlockSpec((1, D_MODEL), c2),
            pl.BlockSpec((D_MODEL, D_MODEL), c2),
            pl.BlockSpec((None, tm, d_ple), lambda i, j: (layer, i, 0)),
            pl.BlockSpec((d_ple, D_MODEL), c2),
            pl.BlockSpec((1, D_MODEL), c2),
        ],
        out_specs=pl.BlockSpec((tm, D_MODEL), lambda i, j: (i, 0)),
        out_shape=jax.ShapeDtypeStruct((n_tok, D_MODEL), F32),
        scratch_shapes=[pltpu.VMEM((tm, D_MODEL), BF16), pltpu.VMEM((tm, D_MODEL), F32)],
        compiler_params=_cparams(("parallel", "arbitrary")),
        name="ffn_ple",
    )(x, g, wu, wd, gp, wg, p, wp, gf)


def _prep_layer_weights(w_in_tail, w_uq, w_ukv):
    idx = [int(v) for v in np.cumsum((B_HEADS, C_Q_LORA, C_KV_LORA))]
    dt, c_q, c_kv, k_r = jnp.split(w_in_tail, idx, axis=0)
    half = C_ROPE // 2
    zr = lambda n: jnp.zeros((n, D_MODEL), BF16)
    k_rs = jnp.concatenate([k_r[half:], k_r[:half]], axis=0)
    w_tail_t = jnp.concatenate([
        c_q, dt, zr(128 - B_HEADS), c_kv,
        zr(C_NOPE), k_r, zr(HEAD_PAD - C_NOPE - C_ROPE),
        zr(C_NOPE), k_rs, zr(HEAD_PAD - C_NOPE - C_ROPE)], axis=0)

    uq = w_uq.reshape(C_Q_LORA, C_HEADS, C_NOPE + C_ROPE)
    uq_n, uq_r = uq[..., :C_NOPE], uq[..., C_NOPE:]
    uq_rs = jnp.concatenate([uq_r[..., half:], uq_r[..., :half]], axis=-1)
    zq = lambda n: jnp.zeros((C_Q_LORA, C_HEADS, n), w_uq.dtype)
    wq1 = jnp.concatenate([uq_n, uq_r, zq(HEAD_PAD - C_NOPE - C_ROPE)], axis=-1).reshape(C_Q_LORA, -1).astype(BF16)
    wq2 = jnp.concatenate([zq(C_NOPE), uq_rs, zq(HEAD_PAD - C_NOPE - C_ROPE)], axis=-1).reshape(C_Q_LORA, -1).astype(BF16)

    ukv = w_ukv.reshape(C_KV_LORA, C_HEADS, C_NOPE + C_V)
    uk, uv = ukv[..., :C_NOPE], ukv[..., C_NOPE:]
    wk = jnp.concatenate([uk, jnp.zeros((C_KV_LORA, C_HEADS, HEAD_PAD - C_NOPE), w_ukv.dtype)], axis=-1)
    wk = wk.reshape(C_KV_LORA, -1).astype(BF16)
    wv = jnp.concatenate([uv, jnp.zeros((C_KV_LORA, C_HEADS, V_ROWS - C_V), w_ukv.dtype)], axis=-1)
    wv = jnp.transpose(wv, (1, 2, 0)).astype(BF16)
    wukt = jnp.transpose(wk.reshape(C_KV_LORA, C_HEADS, HEAD_PAD), (1, 2, 0))
    wuv_h = jnp.transpose(uv, (1, 2, 0)).astype(BF16)
    return w_tail_t, wq1, wq2, wk, wv, wukt, wuv_h


def _rope_tables(pos):
    half = C_ROPE // 2
    inv = jnp.power(ROPE_BASE, -jnp.arange(half, dtype=F32) * (2.0 / C_ROPE))
    ang = pos.astype(F32)[:, None] * inv[None, :]
    cos, sin = jnp.cos(ang), jnp.sin(ang)
    n = pos.shape[0]
    pad = jnp.zeros((n, HEAD_PAD - C_NOPE - C_ROPE), F32)
    cos_t = jnp.concatenate([jnp.ones((n, C_NOPE), F32), cos, cos, pad], axis=1)
    sin_t = jnp.concatenate([jnp.zeros((n, C_NOPE), F32), -sin, sin, pad], axis=1)
    return cos_t, sin_t


def _pad128(v):
    return jnp.concatenate([v, jnp.zeros((128 - v.shape[0],), v.dtype)])[None, :]


def _token_tile(n, pref):
    t = pref
    while n % t:
        t //= 2
    return t


def kernel(x_prompt, x_sample, cache_kv_latent, cache_k_rope, state_ssm, state_conv, page_table, p_prompt, p_sample, ln_mix, w_in, sgu_ln_w, sgu_ln_b, w_s, b_s, conv_w, conv_b, dt_bias, a_log, d_skip, b_norm, q_norm, w_uq, kv_norm, w_ukv, w_pa, w_pb, w_pc, w_o, ln_ffn, w_up, w_down, ln_ple, w_ple_gate, w_ple, ln_final):
    depth = w_in.shape[0]
    _, S, _ = x_prompt.shape
    B, T, _ = x_sample.shape
    n_pages = page_table.shape[1]
    past_len = n_pages * PAGE_SIZE
    ns = B * T
    assert x_prompt.shape[0] == 1 and S % CHUNK == 0 and ns % CHUNK == 0 and CHUNK % T == 0

    xp = x_prompt.reshape(S, D_MODEL)
    xs = x_sample.reshape(ns, D_MODEL)
    cos_p, sin_p = _rope_tables(jnp.arange(S))
    cos_s, sin_s = _rope_tables(past_len + (jnp.arange(ns) % T))

    hp = np.arange(B_INNER) // B_HEADDIM
    e_mat = jnp.asarray((np.arange(128)[:, None] == hp[None, :]).astype(np.float32)).astype(BF16)
    blockmask = jnp.asarray((np.arange(B_HEADS)[:, None] == hp[None, :]).astype(np.float32))
    gn = np.arange(B_GROUPS * B_STATE) // B_STATE
    hg = np.where(np.arange(128) < B_HEADS, np.arange(128) // (B_HEADS // B_GROUPS), -1)
    s_mat = jnp.asarray((gn[:, None] == hg[None, :]).astype(np.float32)).astype(BF16)

    tm_p = _token_tile(S, 1024)
    tm_s = _token_tile(ns, 512)
    tq = _token_tile(S, FLASH_TILE)
    cp = _token_tile(n_pages // ATTN_SLOTS, ATTN_PAGES_PER_CHUNK)
    cache_kr_t = jnp.swapaxes(cache_k_rope, 2, 3)
    tile_rep = CHUNK // T
    eye_rep = jnp.eye(tile_rep, dtype=F32)

    w_in_t = jnp.swapaxes(w_in, 1, 2).astype(BF16)

    outs_p, outs_s = [], []
    for i in range(depth):
        w_tail_t, wq1, wq2, wk, wv, wukt, wuv_h = _prep_layer_weights(w_in_t[i, IN_TAIL0:], w_uq[i], w_ukv[i])
        g_mix = ln_mix[i][None, :]
        lnw, lnb = sgu_ln_w[i][None, :], sgu_ln_b[i][None, :]
        ws_p = w_s[i][:, :CHUNK, :CHUNK]
        bs_p = b_s[i][:, :CHUNK, None]
        ws_t = jnp.tril(w_s[i][:, :T, :T])
        ws_s = jnp.einsum('ab,gts->gatbs', eye_rep, ws_t).reshape(A_GROUPS, CHUNK, CHUNK)
        bs_s = jnp.tile(b_s[i][:, :T], (1, tile_rep))[:, :, None]
        cw, cb = conv_w[i], conv_b[i][None, :]
        dtb128, alog128 = _pad128(dt_bias[i]), _pad128(a_log[i])
        dsk_e = jnp.repeat(d_skip[i], B_HEADDIM)[None, :]
        bn = b_norm[i][None, :]
        qn, kvn = q_norm[i][None, :], kv_norm[i][None, :]
        wpa, wpb, wpc, wo = (w.astype(BF16) for w in (w_pa[i], w_pb[i], w_pc[i], w_o[i]))
        wu, wd = w_up[i].astype(BF16), w_down[i].astype(BF16)
        wg, wp = w_ple_gate[i].astype(BF16), w_ple[i].astype(BF16)
        g_ffn, g_ple, g_fin = ln_ffn[i][None, :], ln_ple[i][None, :], ln_final[None, :]
        final = i == depth - 1

        proj = _in_proj(xp, g_mix, w_tail_t, w_in_t, i, _token_tile(S, 2048))
        ya, av = _gate_a(proj, lnw, lnb, ws_p, bs_p, _token_tile(S, 512))
        yb, ssm_t = _ssd_prompt(proj, cw, cb, dtb128, alog128, dsk_e, bn, e_mat)
        ssm_p = jnp.transpose(ssm_t.reshape(B_GROUPS, B_STATE, B_HEADS // B_GROUPS, B_HEADDIM),
                              (0, 2, 3, 1)).reshape(1, B_HEADS, B_HEADDIM, B_STATE)
        q, k, v, ckvn, krope = _c_prep(proj, cos_p, sin_p, qn, kvn, wq1, wq2, wk, wv, _token_tile(S, 512))
        yc = _flash(q, k, v, tq, FLASH_HEADS_PER_STEP)
        xp = _merge(xp, g_mix, w_in_t, i, ya, yb, yc, wpa, wpb, wpc, wo, _token_tile(S, 512))
        xp = _ffn_ple(xp, g_ffn, wu, wd, g_ple, wg, p_prompt.reshape(depth, S, -1), i, wp, g_fin, tm_p, 1024, final)
        outs_p.append((ckvn.reshape(1, S, C_KV_LORA), krope.reshape(1, S, C_ROPE),
                       ssm_p,
                       proj[S - (B_CONV - 1):, COL_XBC:COL_XBC + B_CONV_DIM][None],
                       av[S - CHUNK:][None]))

        proj = _in_proj(xs, g_mix, w_tail_t, w_in_t, i, tm_s)
        ya, av = _gate_a(proj, lnw, lnb, ws_s, bs_s, tm_s)
        xbc_s = proj[:, COL_XBC:COL_XBC + B_CONV_DIM].reshape(B, T, B_CONV_DIM)
        z_s = proj[:, COL_Z:COL_Z + B_INNER].reshape(B, T, B_INNER)
        dt_s = proj[:, COL_DT:COL_DT + 128].reshape(B, T, 128)
        yb, ssm_new = _ssd_sample(xbc_s, state_conv, z_s, dt_s, state_ssm, i, cw, cb, dtb128, alog128,
                                  dsk_e, bn, e_mat, blockmask, s_mat)
        q, _, _, ckvn, krope = _c_prep(proj, cos_s, sin_s, qn, kvn, wq1, wq2, wk, wv, tm_s)
        qlat = _q_lat(q, wukt)
        qlat = jnp.transpose(qlat.reshape(C_HEADS, B, T, C_KV_LORA), (1, 0, 2, 3)).reshape(B, C_HEADS * T, C_KV_LORA)
        q128 = jnp.transpose(q.reshape(B, T, C_HEADS, HEAD_PAD), (0, 2, 1, 3)).reshape(B, C_HEADS * T, HEAD_PAD)
        olat = _attn_sample(page_table, qlat, q128, ckvn.reshape(B, T, C_KV_LORA), krope.reshape(B, T, C_ROPE),
                            cache_kv_latent, cache_kr_t, i, cp, min(ATTN_STREAMS, cp))
        olat = jnp.transpose(olat.reshape(B, C_HEADS, T, C_KV_LORA), (1, 0, 2, 3)).reshape(C_HEADS, ns, C_KV_LORA)
        yc = _uv_proj(olat, wuv_h).reshape(C_HEADS * C_V, ns)
        xs = _merge(xs, g_mix, w_in_t, i, ya, yb.reshape(ns, B_INNER), yc, wpa, wpb, wpc, wo, tm_s)
        xs = _ffn_ple(xs, g_ffn, wu, wd, g_ple, wg, p_sample.reshape(depth, ns, -1), i, wp, g_fin, tm_s, 1024, final)
        outs_s.append((ckvn.reshape(B, T, C_KV_LORA), krope.reshape(B, T, C_ROPE), ssm_new,
                       xbc_s[:, T - (B_CONV - 1):], av.reshape(B, T, A_WIDTH)))

    kv_p, kr_p, ssm_p, conv_p, v_p = [jnp.stack(t) for t in zip(*outs_p)]
    kv_s, kr_s, ssm_s, conv_s, v_s = [jnp.stack(t) for t in zip(*outs_s)]
    return (xp.reshape(1, S, D_MODEL), xs.reshape(B, T, D_MODEL), kv_p, kr_p, ssm_p, conv_p, v_p,
            kv_s, kr_s, ssm_s, conv_s, v_s)
```

```python
import functools
import math

import numpy as np
import jax
import jax.numpy as jnp
from jax import lax
from jax.experimental import pallas as pl
from jax.experimental.pallas import tpu as pltpu

F32 = jnp.float32
BF16 = jnp.bfloat16

NORM_EPS = 1e-6
D_MODEL = 1024
N_BRANCH = 3
A_WIDTH = 512
A_GROUPS = 4
CHUNK = 128
B_INNER = 1024
B_HEADDIM = 64
B_HEADS = 16
B_GROUPS = 4
B_STATE = 128
B_CONV = 4
B_CONV_DIM = B_INNER + 2 * B_GROUPS * B_STATE
C_HEADS = 8
C_NOPE = 64
C_ROPE = 32
C_V = 64
C_KV_LORA = 256
C_Q_LORA = 384
ROPE_BASE = 10000.0
C_SCALE = (C_NOPE + C_ROPE) ** -0.5
LOG2E = math.log2(math.e)
FLASH_HEADS_PER_STEP = 8
FLASH_TILE = 1024
FLASH_LOOKAHEAD = 2
SSD_SAMPLE_SEQS = 4
V_ROWS = 80
ATTN_PAGES_PER_CHUNK = 32
ATTN_SLOTS = 4
ATTN_STREAMS = 4
D_FF = 4 * D_MODEL
PAGE_SIZE = 128
HEAD_PAD = 128

COL_CQ = 0
COL_DT = 384
COL_CKV = 512
COL_KR = 768
COL_KRS = 896
COL_Z = 1024
COL_XBC = 2048
COL_AU = 4096
COL_AV = 4608
D_IN_PAD = 5120
IN_TILE = 1024
IN_TAIL0 = N_BRANCH * D_MODEL + 2 * A_WIDTH + B_INNER + B_CONV_DIM

VMEM_LIMIT = 56 * 1024 * 1024


def _cparams(sem):
    return pltpu.CompilerParams(dimension_semantics=sem, vmem_limit_bytes=VMEM_LIMIT)


def _rms(x, g):
    ms = jnp.mean(x * x, axis=-1, keepdims=True)
    return x * lax.rsqrt(ms + NORM_EPS) * g


def _sigmoid(x):
    return 1.0 / (1.0 + jnp.exp(-x))


def _silu(x):
    return x * _sigmoid(x)


def _gelu(x):
    c = math.sqrt(2.0 / math.pi)
    return 0.5 * x * (1.0 + jnp.tanh(c * (x + 0.044715 * (x * x * x))))


def _softplus(x):
    return jnp.maximum(x, 0.0) + jnp.log(1.0 + jnp.exp(-jnp.abs(x)))


def _dot(a, b):
    return jnp.dot(a, b, preferred_element_type=F32)


def _dot_nt(a, b):
    return lax.dot_general(a, b, (((1,), (1,)), ((), ())), preferred_element_type=F32)


def _dot_tn(a, b):
    return lax.dot_general(a, b, (((0,), (0,)), ((), ())), preferred_element_type=F32)


def _split3(x, axis):
    hi = x.astype(BF16).astype(F32)
    r = x - hi
    mid = r.astype(BF16).astype(F32)
    return jnp.concatenate([hi, mid, r - mid], axis=axis).astype(BF16)


def _sum3(y, axis):
    n = y.shape[axis] // 3
    parts = [lax.slice_in_dim(y, k * n, (k + 1) * n, axis=axis) for k in range(3)]
    return parts[0] + parts[1] + parts[2]


def _in_proj_kernel(x_ref, g_ref, wt_ref, wm_ref, o_ref, h_scr):
    j = pl.program_id(1)

    @pl.when(j == 0)
    def _():
        h_scr[...] = _rms(x_ref[...], g_ref[...]).astype(BF16)
        o_ref[...] = _dot_nt(h_scr[...], wt_ref[...])

    @pl.when(j > 0)
    def _():
        o_ref[...] = _dot_nt(h_scr[...], wm_ref[...])


def _in_proj(x, g, w_tail_t, w_in_t, layer, tm):
    n_tok = x.shape[0]
    tn = IN_TILE
    first = (N_BRANCH * D_MODEL) // tn
    n_main = (2 * A_WIDTH + B_INNER + B_CONV_DIM) // tn
    assert (N_BRANCH * D_MODEL) % tn == 0 and 2 * A_WIDTH == tn and B_INNER % tn == 0 and B_CONV_DIM % tn == 0
    assert D_IN_PAD == (1 + n_main) * tn
    main_block = lambda i, j: (layer, first + jnp.where(j == 0, 1, j % n_main), 0)
    return pl.pallas_call(
        _in_proj_kernel,
        grid=(n_tok // tm, 1 + n_main),
        in_specs=[
            pl.BlockSpec((tm, D_MODEL), lambda i, j: (i, 0)),
            pl.BlockSpec((1, D_MODEL), lambda i, j: (0, 0)),
            pl.BlockSpec((tn, D_MODEL), lambda i, j: (0, 0)),
            pl.BlockSpec((None, tn, D_MODEL), main_block),
        ],
        out_specs=pl.BlockSpec((tm, tn), lambda i, j: (i, j)),
        out_shape=jax.ShapeDtypeStruct((n_tok, D_IN_PAD), F32),
        scratch_shapes=[pltpu.VMEM((tm, D_MODEL), BF16)],
        compiler_params=_cparams(("parallel", "arbitrary")),
        name="in_proj",
    )(x, g, w_tail_t, w_in_t)


def _gate_a_kernel(u_ref, v_ref, lnw_ref, lnb_ref, ws_ref, bs_ref, ya_ref, av_ref, *, n_chunks):
    row = lax.broadcasted_iota(jnp.int32, (CHUNK, CHUNK), 0)
    col = lax.broadcasted_iota(jnp.int32, (CHUNK, CHUNK), 1)
    causal = col <= row
    gd = A_WIDTH // A_GROUPS
    for c in range(n_chunks):
        rows = pl.ds(c * CHUNK, CHUNK)
        v = _gelu(v_ref[rows, :])
        mu = jnp.mean(v, axis=-1, keepdims=True)
        vc = v - mu
        var = jnp.mean(vc * vc, axis=-1, keepdims=True)
        av = vc * lax.rsqrt(var + NORM_EPS) * lnw_ref[...] + lnb_ref[...]
        av_ref[rows, :] = av
        u = _gelu(u_ref[rows, :])
        for g in range(A_GROUPS):
            w = jnp.where(causal, ws_ref[g], 0.0).astype(BF16)
            s = _dot(w, av[:, g * gd:(g + 1) * gd].astype(BF16)) + bs_ref[g]
            ya_ref[rows, g * gd:(g + 1) * gd] = (u[:, g * gd:(g + 1) * gd] * s).astype(BF16)


def _gate_a(proj, lnw, lnb, ws, bs, tm):
    n_tok = proj.shape[0]
    kern = functools.partial(_gate_a_kernel, n_chunks=tm // CHUNK)
    return pl.pallas_call(
        kern,
        grid=(n_tok // tm,),
        in_specs=[
            pl.BlockSpec((tm, A_WIDTH), lambda i: (i, COL_AU // A_WIDTH)),
            pl.BlockSpec((tm, A_WIDTH), lambda i: (i, COL_AV // A_WIDTH)),
            pl.BlockSpec((1, A_WIDTH), lambda i: (0, 0)),
            pl.BlockSpec((1, A_WIDTH), lambda i: (0, 0)),
            pl.BlockSpec((A_GROUPS, CHUNK, CHUNK), lambda i: (0, 0, 0)),
            pl.BlockSpec((A_GROUPS, CHUNK, 1), lambda i: (0, 0, 0)),
        ],
        out_specs=[
            pl.BlockSpec((tm, A_WIDTH), lambda i: (i, 0)),
            pl.BlockSpec((tm, A_WIDTH), lambda i: (i, 0)),
        ],
        out_shape=[
            jax.ShapeDtypeStruct((n_tok, A_WIDTH), BF16),
            jax.ShapeDtypeStruct((n_tok, A_WIDTH), F32),
        ],
        compiler_params=_cparams(("parallel",)),
        name="gate_a",
    )(proj, proj, lnw, lnb, ws, bs)


def _ssd_prompt_kernel(xbc_ref, z_ref, dt_ref, cw_ref, cb_ref, dtb_ref, alog_ref, dsk_ref, bn_ref, e_ref,
                       yb_ref, st_ref, xp_scr, h_scr, y_scr, xde_scr):
    L, P, N = CHUNK, B_HEADDIM, B_STATE
    i = pl.program_id(0)

    @pl.when(i == 0)
    def _():
        xp_scr[0:8, :] = jnp.zeros((8, B_CONV_DIM), F32)
        h_scr[...] = jnp.zeros_like(h_scr)

    xp_scr[8:8 + L, :] = xbc_ref[...]
    acc = xp_scr[8:8 + L, :] * cw_ref[B_CONV - 1:B_CONV, :]
    for sh in range(1, B_CONV):
        acc = acc + xp_scr[8 - sh:8 - sh + L, :] * cw_ref[B_CONV - 1 - sh:B_CONV - sh, :]
    y_scr[...] = _silu(acc + cb_ref[...])
    xp_scr[0:8, :] = xp_scr[L:L + 8, :]

    lane = lax.broadcasted_iota(jnp.int32, (1, 128), 1)
    a_row = jnp.where(lane < B_HEADS, -jnp.exp(alog_ref[...]), 0.0)
    dt = _softplus(dt_ref[...] + dtb_ref[...])
    da = dt * a_row
    row = lax.broadcasted_iota(jnp.int32, (L, L), 0)
    col = lax.broadcasted_iota(jnp.int32, (L, L), 1)
    causal = col <= row
    da3 = _split3(da, 1)
    cs = _sum3(_dot(jnp.where(causal, 1.0, 0.0).astype(BF16), da3), 1) * LOG2E
    cs_t = _sum3(_dot_tn(da3, jnp.where(row <= col, 1.0, 0.0).astype(BF16)), 0) * LOG2E
    cs_last = cs[L - 1:L, :]
    dec_e = _sum3(_dot(_split3(jnp.broadcast_to(jnp.exp2(cs_last), (8, 128)), 0), e_ref[...]), 0)[0:1, :]

    rep = B_HEADS // B_GROUPS
    gw = rep * P
    for g in range(B_GROUPS):
        bg = y_scr[:, B_INNER + g * N:B_INNER + (g + 1) * N].astype(BF16)
        cg = y_scr[:, B_INNER + (B_GROUPS + g) * N:B_INNER + (B_GROUPS + g + 1) * N].astype(BF16)
        scores = _dot_nt(cg, bg)
        hg = h_scr[g]
        yo = _dot(cg, hg.astype(BF16))
        for pr in range(rep // 2):
            ha = g * rep + 2 * pr
            cols = slice(ha * P, (ha + 2) * P)
            col_a = jnp.broadcast_to(cs[:, ha:ha + 1], (L, L))
            col_b = jnp.broadcast_to(cs[:, ha + 1:ha + 2], (L, L))
            m_a = scores * jnp.exp2(jnp.where(causal, col_a - cs_t[ha:ha + 1, :], -jnp.inf))
            m_b = scores * jnp.exp2(jnp.where(causal, col_b - cs_t[ha + 1:ha + 2, :], -jnp.inf))
            first = lane < P
            col2 = jnp.where(first, col_a, col_b)
            dt2 = jnp.where(first, dt[:, ha:ha + 1], dt[:, ha + 1:ha + 2])
            last2 = jnp.where(first, cs_last[:, ha:ha + 1], cs_last[:, ha + 1:ha + 2])
            xs2 = y_scr[:, cols]
            xdt = xs2 * dt2
            xdt_b = xdt.astype(BF16)
            zero = jnp.zeros_like(xdt_b)
            x_diag = jnp.concatenate([jnp.where(first, xdt_b, zero), jnp.where(first, zero, xdt_b)], axis=0)
            yd = _dot(jnp.concatenate([m_a.astype(BF16), m_b.astype(BF16)], axis=1), x_diag)
            y_scr[:, cols] = yd + jnp.exp2(col2) * yo[:, 2 * pr * P:(2 * pr + 2) * P] + dsk_ref[:, cols] * xs2
            xde_scr[:, cols] = (xdt * jnp.exp2(last2 - col2)).astype(BF16)
        st = _dot_tn(bg, xde_scr[:, g * gw:(g + 1) * gw])
        h_scr[g] = dec_e[:, g * gw:(g + 1) * gw] * hg + st

    y = y_scr[:, 0:B_INNER] * _silu(z_ref[...])
    yb_ref[...] = _rms(y, bn_ref[...]).astype(BF16)
    st_ref[...] = h_scr[...]


def _ssd_prompt(proj, cw, cb, dtb128, alog128, dsk_e, bnorm, e_mat):
    S = proj.shape[0]
    gw = (B_HEADS // B_GROUPS) * B_HEADDIM
    return pl.pallas_call(
        _ssd_prompt_kernel,
        grid=(S // CHUNK,),
        in_specs=[
            pl.BlockSpec((CHUNK, B_CONV_DIM), lambda i: (i, COL_XBC // B_CONV_DIM)),
            pl.BlockSpec((CHUNK, B_INNER), lambda i: (i, COL_Z // B_INNER)),
            pl.BlockSpec((CHUNK, 128), lambda i: (i, COL_DT // 128)),
            pl.BlockSpec((B_CONV, B_CONV_DIM), lambda i: (0, 0)),
            pl.BlockSpec((1, B_CONV_DIM), lambda i: (0, 0)),
            pl.BlockSpec((1, 128), lambda i: (0, 0)),
            pl.BlockSpec((1, 128), lambda i: (0, 0)),
            pl.BlockSpec((1, B_INNER), lambda i: (0, 0)),
            pl.BlockSpec((1, B_INNER), lambda i: (0, 0)),
            pl.BlockSpec((128, B_INNER), lambda i: (0, 0)),
        ],
        out_specs=[
            pl.BlockSpec((CHUNK, B_INNER), lambda i: (i, 0)),
            pl.BlockSpec((B_GROUPS, B_STATE, gw), lambda i: (0, 0, 0)),
        ],
        out_shape=[
            jax.ShapeDtypeStruct((S, B_INNER), BF16),
            jax.ShapeDtypeStruct((B_GROUPS, B_STATE, gw), F32),
        ],
        scratch_shapes=[
            pltpu.VMEM((CHUNK + 8, B_CONV_DIM), F32),
            pltpu.VMEM((B_GROUPS, B_STATE, gw), F32),
            pltpu.VMEM((CHUNK, B_CONV_DIM), F32),
            pltpu.VMEM((CHUNK, B_INNER), BF16),
        ],
        compiler_params=_cparams(("arbitrary",)),
        name="ssd_prompt",
    )(proj, proj, proj, cw, cb, dtb128, alog128, dsk_e, bnorm, e_mat)


def _ssd_sample_kernel(xbc_ref, cprev_ref, z_ref, dt_ref, h0_ref, cw_ref, cb_ref, dtb_ref, alog_ref,
                       dsk_ref, bn_ref, e_ref, bm_ref, s_ref, *rest, T, seqs, first_layer):
    if first_layer:
        yb_ref, hn_all, xp_scr, r16_scr, prod_scr, ex_scr, a_scr, b_scr = rest
        hn_ref = hn_all.at[0]
        for later in range(1, hn_all.shape[0]):
            hn_all[later] = jnp.zeros(hn_all.shape[1:], F32)
    else:
        _, yb_ref, hn_ref, xp_scr, r16_scr, prod_scr, ex_scr, a_scr, b_scr = rest
    for e in range(seqs):
        _ssd_sample_one(e, xbc_ref, cprev_ref, z_ref, dt_ref, h0_ref, cw_ref, cb_ref, dtb_ref, alog_ref,
                        dsk_ref, bn_ref, e_ref, bm_ref, s_ref, yb_ref, hn_ref,
                        xp_scr.at[e], r16_scr.at[e], prod_scr.at[e], ex_scr.at[e], a_scr.at[e], b_scr.at[e], T)


def _ssd_sample_one(e, xbc_ref, cprev_ref, z_ref, dt_ref, h0_ref, cw_ref, cb_ref, dtb_ref, alog_ref,
                    dsk_ref, bn_ref, e_ref, bm_ref, s_ref,
                    yb_ref, hn_ref, xp_scr, r16_scr, prod_scr, ex_scr, a_scr, b_scr, T):
    H, P, N, G = B_HEADS, B_HEADDIM, B_STATE, B_GROUPS
    rep = H // G
    pairs = [(t, s) for t in range(T) for s in range(t + 1)]
    xp_scr[0:B_CONV - 1, :] = cprev_ref[e]
    xp_scr[B_CONV - 1:B_CONV - 1 + T, :] = xbc_ref[e]
    acc = xp_scr[0:T, :] * cw_ref[0:1, :]
    for k in range(1, B_CONV):
        acc = acc + xp_scr[k:k + T, :] * cw_ref[k:k + 1, :]
    xc = _silu(acc + cb_ref[...])
    xs = xc[:, 0:B_INNER]
    bm = xc[:, B_INNER:B_INNER + G * N]
    cm = xc[:, B_INNER + G * N:]

    lane = lax.broadcasted_iota(jnp.int32, (1, 128), 1)
    a_row = jnp.where(lane < H, -jnp.exp(alog_ref[...]), 0.0)
    dt = _softplus(dt_ref[e] + dtb_ref[...])
    da = dt * a_row
    cs_rows = [da[0:1, :]]
    for t in range(1, T):
        cs_rows.append(cs_rows[-1] + da[t:t + 1, :])
    cs_last = cs_rows[-1]

    prod_scr[...] = jnp.zeros_like(prod_scr)
    for idx, (t, s) in enumerate(pairs):
        prod_scr[idx:idx + 1, :] = cm[t:t + 1, :] * bm[s:s + 1, :]
    gh = _sum3(_dot(_split3(prod_scr[...], 0), s_ref[...]), 0)
    ex_scr[...] = jnp.zeros_like(ex_scr)
    ex_scr[0:T, :] = dt
    for t in range(T):
        ex_scr[T + t:T + t + 1, :] = jnp.exp(cs_rows[t])
    for idx, (t, s) in enumerate(pairs):
        ex_scr[2 * T + idx:2 * T + idx + 1, :] = gh[idx:idx + 1, :] * jnp.exp(cs_rows[t] - cs_rows[s])
    ex = _sum3(_dot(_split3(ex_scr[...], 0), e_ref[...]), 0)
    xdt = xs * ex[0:T, :]

    for g in range(G):
        r16_scr[g * T:(g + 1) * T, :] = cm[:, g * N:(g + 1) * N]
    h2d = h0_ref[e].reshape(H * P, N)
    r = _dot_nt(r16_scr[...].astype(BF16), h2d.astype(BF16))
    gw = rep * P
    y_rows = []
    for t in range(T):
        yo = jnp.concatenate([r[g * T + t:g * T + t + 1, g * gw:(g + 1) * gw] for g in range(G)], axis=1)
        y_rows.append(yo * ex[T + t:T + t + 1, :])
    for idx, (t, s) in enumerate(pairs):
        y_rows[t] = y_rows[t] + ex[2 * T + idx:2 * T + idx + 1, :] * xdt[s:s + 1, :]
    for t in range(T):
        y = y_rows[t] + dsk_ref[...] * xs[t:t + 1, :]
        y = y * _silu(z_ref[e, t:t + 1, :])
        yb_ref[e, t:t + 1, :] = _rms(y, bn_ref[...]).astype(BF16)

    eye = (lax.broadcasted_iota(jnp.int32, (H, 128), 0) == lax.broadcasted_iota(jnp.int32, (H, 128), 1)).astype(F32)

    def to_col(v):
        return jnp.sum(jnp.broadcast_to(v, (H, 128)) * eye, axis=1, keepdims=True)

    a_scr[...] = jnp.zeros_like(a_scr)
    b_scr[...] = jnp.zeros_like(b_scr)
    for t in range(T):
        dcol = to_col(jnp.exp(cs_last - cs_rows[t]))
        a_scr[t * H:(t + 1) * H, :] = jnp.broadcast_to(xdt[t:t + 1, :], (H, H * P)) * bm_ref[...]
        for g in range(G):
            b_scr[t * H + g * rep:t * H + (g + 1) * rep, :] = (
                jnp.broadcast_to(bm[t:t + 1, g * N:(g + 1) * N], (rep, N)) * dcol[g * rep:(g + 1) * rep, :])
    st = _dot_tn(a_scr[...].astype(BF16), b_scr[...].astype(BF16))
    dfull = jnp.broadcast_to(to_col(jnp.exp(cs_last)), (H, N))
    for h in range(H):
        hn_ref[e, h] = dfull[h:h + 1, :] * h0_ref[e, h] + st[h * P:(h + 1) * P, :]


def _ssd_sample(xbc, cprev, z, dtb, h0, layer, cw, cb, dtb128, alog128, dsk_e, bnorm, e_mat, blockmask, s_mat,
                states):
    B, T, _ = xbc.shape
    depth = h0.shape[0]
    n_pair = -(-(T * (T + 1) // 2) // 8) * 8
    assert B_CONV - 1 + T <= 8 and T * B_HEADS <= 128
    nseq = SSD_SAMPLE_SEQS if B % SSD_SAMPLE_SEQS == 0 else 1
    first_layer = states is None
    assert first_layer == (layer == 0)
    kern = functools.partial(_ssd_sample_kernel, T=T, seqs=nseq, first_layer=first_layer)
    c2 = lambda b: (0, 0)
    n_fixed_inputs = 14
    if first_layer:
        state_spec = pl.BlockSpec((depth, nseq, B_HEADS, B_HEADDIM, B_STATE), lambda b: (0, b, 0, 0, 0))
    else:
        state_spec = pl.BlockSpec((None, nseq, B_HEADS, B_HEADDIM, B_STATE), lambda b: (layer, b, 0, 0, 0))
    return pl.pallas_call(
        kern,
        grid=(B // nseq,),
        input_output_aliases={} if first_layer else {n_fixed_inputs: 1},
        in_specs=[
            pl.BlockSpec((nseq, T, B_CONV_DIM), lambda b: (b, 0, 0)),
            pl.BlockSpec((None, nseq, B_CONV - 1, B_CONV_DIM), lambda b: (layer, b, 0, 0)),
            pl.BlockSpec((nseq, T, B_INNER), lambda b: (b, 0, 0)),
            pl.BlockSpec((nseq, T, 128), lambda b: (b, 0, 0)),
            pl.BlockSpec((None, nseq, B_HEADS, B_HEADDIM, B_STATE), lambda b: (layer, b, 0, 0, 0)),
            pl.BlockSpec((B_CONV, B_CONV_DIM), c2),
            pl.BlockSpec((1, B_CONV_DIM), c2),
            pl.BlockSpec((1, 128), c2),
            pl.BlockSpec((1, 128), c2),
            pl.BlockSpec((1, B_INNER), c2),
            pl.BlockSpec((1, B_INNER), c2),
            pl.BlockSpec((128, B_INNER), c2),
            pl.BlockSpec((B_HEADS, B_INNER), c2),
            pl.BlockSpec((B_GROUPS * B_STATE, 128), c2),
        ] + ([] if first_layer else [pl.BlockSpec(memory_space=pl.ANY)]),
        out_specs=[
            pl.BlockSpec((nseq, T, B_INNER), lambda b: (b, 0, 0)),
            state_spec,
        ],
        out_shape=[
            jax.ShapeDtypeStruct((B, T, B_INNER), BF16),
            jax.ShapeDtypeStruct((depth, B, B_HEADS, B_HEADDIM, B_STATE), F32),
        ],
        scratch_shapes=[
            pltpu.VMEM((nseq, 8, B_CONV_DIM), F32),
            pltpu.VMEM((nseq, B_GROUPS * T, B_STATE), F32),
            pltpu.VMEM((nseq, n_pair, B_GROUPS * B_STATE), F32),
            pltpu.VMEM((nseq, 2 * T + n_pair, 128), F32),
            pltpu.VMEM((nseq, 128, B_INNER), F32),
            pltpu.VMEM((nseq, 128, B_STATE), F32),
        ],
        compiler_params=_cparams(("parallel",)),
        name="ssd_sample",
    )(xbc, cprev, z, dtb, h0, cw, cb, dtb128, alog128, dsk_e, bnorm, e_mat, blockmask, s_mat,
      *([] if first_layer else [states]))


def _c_prep_kernel(cq_ref, ckv_ref, kr_ref, krs_ref, cos_ref, sin_ref, qn_ref, kvn_ref,
                   wq1_ref, wq2_ref, wk_ref, wvt_ref,
                   q_ref, k_ref, vt_ref, ckvn_ref, krope_ref):
    cos = cos_ref[...]
    sin = sin_ref[...]
    cos8 = jnp.concatenate([cos] * C_HEADS, axis=1)
    sin8 = jnp.concatenate([sin] * C_HEADS, axis=1)
    cqn = _rms(cq_ref[...], qn_ref[...]).astype(BF16)
    q = _dot(cqn, wq1_ref[...]) * cos8 + _dot(cqn, wq2_ref[...]) * sin8
    q_ref[...] = (q * (C_SCALE * LOG2E)).astype(BF16)
    ckvn = _rms(ckv_ref[...], kvn_ref[...])
    ckvn_ref[...] = ckvn
    k128 = kr_ref[...] * cos + krs_ref[...] * sin
    krope_ref[...] = k128[:, C_NOPE:C_NOPE + C_ROPE]
    cb = ckvn.astype(BF16)
    k_ref[...] = (_dot(cb, wk_ref[...]) + jnp.concatenate([k128] * C_HEADS, axis=1)).astype(BF16)
    row = lax.broadcasted_iota(jnp.int32, (V_ROWS, 1), 0)
    for h in range(C_HEADS):
        vt = _dot_nt(wvt_ref[h], cb)
        vt_ref[h * V_ROWS:(h + 1) * V_ROWS, :] = jnp.where(row == C_V, 1.0, vt).astype(BF16)


def _c_prep(proj, cos, sin, qn, kvn, wq1, wq2, wk, wv, tm):
    n_tok = proj.shape[0]
    c2 = lambda i: (0, 0)
    hq = C_HEADS * HEAD_PAD
    return pl.pallas_call(
        _c_prep_kernel,
        grid=(n_tok // tm,),
        in_specs=[
            pl.BlockSpec((tm, C_Q_LORA), lambda i: (i, COL_CQ // C_Q_LORA)),
            pl.BlockSpec((tm, C_KV_LORA), lambda i: (i, COL_CKV // C_KV_LORA)),
            pl.BlockSpec((tm, 128), lambda i: (i, COL_KR // 128)),
            pl.BlockSpec((tm, 128), lambda i: (i, COL_KRS // 128)),
            pl.BlockSpec((tm, 128), lambda i: (i, 0)),
            pl.BlockSpec((tm, 128), lambda i: (i, 0)),
            pl.BlockSpec((1, C_Q_LORA), c2),
            pl.BlockSpec((1, C_KV_LORA), c2),
            pl.BlockSpec((C_Q_LORA, hq), c2),
            pl.BlockSpec((C_Q_LORA, hq), c2),
            pl.BlockSpec((C_KV_LORA, hq), c2),
            pl.BlockSpec((C_HEADS, V_ROWS, C_KV_LORA), lambda i: (0, 0, 0)),
        ],
        out_specs=[
            pl.BlockSpec((tm, hq), lambda i: (i, 0)),
            pl.BlockSpec((tm, hq), lambda i: (i, 0)),
            pl.BlockSpec((C_HEADS * V_ROWS, tm), lambda i: (0, i)),
            pl.BlockSpec((tm, C_KV_LORA), lambda i: (i, 0)),
            pl.BlockSpec((tm, C_ROPE), lambda i: (i, 0)),
        ],
        out_shape=[
            jax.ShapeDtypeStruct((n_tok, hq), BF16),
            jax.ShapeDtypeStruct((n_tok, hq), BF16),
            jax.ShapeDtypeStruct((C_HEADS * V_ROWS, n_tok), BF16),
            jax.ShapeDtypeStruct((n_tok, C_KV_LORA), F32),
            jax.ShapeDtypeStruct((n_tok, C_ROPE), F32),
        ],
        compiler_params=_cparams(("parallel",)),
        name="c_prep",
    )(proj, proj, proj, proj, cos, sin, qn, kvn, wq1, wq2, wk, wv)


def _flash_kernel(qi_ref, kj_ref, q_ref, k_ref, vt_ref, o_ref, m_scr, acc_scr, s_scr, *, tq, tk, hps):
    s_idx = pl.program_id(1)
    qi = qi_ref[s_idx]
    kj = kj_ref[s_idx]

    @pl.when(kj == 0)
    def _():
        m_scr[...] = jnp.full_like(m_scr, -jnp.inf)
        acc_scr[...] = jnp.zeros_like(acc_scr)

    def step(masked):
        if masked:
            visible = (lax.broadcasted_iota(jnp.int32, (tk, tq), 0)
                       <= lax.broadcasted_iota(jnp.int32, (tk, tq), 1))

        def scores(hh):
            lanes = slice(hh * HEAD_PAD, (hh + 1) * HEAD_PAD)
            s = _dot_nt(k_ref[:, lanes], q_ref[:, lanes])
            if masked:
                s = jnp.where(visible, s, -jnp.inf)
            s_scr[hh % (FLASH_LOOKAHEAD + 1)] = s

        for hh in range(min(FLASH_LOOKAHEAD, hps)):
            scores(hh)
        for hh in range(hps):
            if hh + FLASH_LOOKAHEAD < hps:
                scores(hh + FLASH_LOOKAHEAD)
            s_tile = s_scr.at[hh % (FLASH_LOOKAHEAD + 1)]
            m_prev = m_scr[hh]
            m_new = jnp.maximum(m_prev, jnp.max(s_tile[...], axis=0, keepdims=True))
            m_scr[hh] = m_new
            alpha = jnp.exp2(m_prev[0:1, :] - m_new[0:1, :])
            p = jnp.exp2(s_tile[...] - m_new[0:1, :]).astype(BF16)
            acc_scr[hh] = alpha * acc_scr[hh] + _dot(vt_ref[hh * V_ROWS:(hh + 1) * V_ROWS, :], p)

    @pl.when(kj < qi)
    def _():
        step(False)

    @pl.when(kj == qi)
    def _():
        step(True)
        for hh in range(hps):
            a = acc_scr[hh]
            o_ref[hh * C_V:(hh + 1) * C_V, :] = (a[0:C_V, :] / a[C_V:C_V + 1, :]).astype(BF16)


def _flash(q, k, vt, tq, hps):
    S = q.shape[0]
    tk = tq
    nq = S // tq
    qi = np.concatenate([np.full(i + 1, i, np.int32) for i in range(nq)])
    kj = np.concatenate([np.arange(i + 1, dtype=np.int32) for i in range(nq)])
    kern = functools.partial(_flash_kernel, tq=tq, tk=tk, hps=hps)
    grid_spec = pltpu.PrefetchScalarGridSpec(
        num_scalar_prefetch=2,
        grid=(C_HEADS // hps, int(qi.shape[0])),
        in_specs=[
            pl.BlockSpec((tq, hps * HEAD_PAD), lambda p, s, qi, kj: (qi[s], p)),
            pl.BlockSpec((tk, hps * HEAD_PAD), lambda p, s, qi, kj: (kj[s], p)),
            pl.BlockSpec((hps * V_ROWS, tk), lambda p, s, qi, kj: (p, kj[s])),
        ],
        out_specs=pl.BlockSpec((hps * C_V, tq), lambda p, s, qi, kj: (p, qi[s])),
        scratch_shapes=[
            pltpu.VMEM((hps, 8, tq), F32),
            pltpu.VMEM((hps, V_ROWS, tq), F32),
            pltpu.VMEM((FLASH_LOOKAHEAD + 1, tk, tq), F32),
        ],
    )
    return pl.pallas_call(
        kern,
        grid_spec=grid_spec,
        out_shape=jax.ShapeDtypeStruct((C_HEADS * C_V, S), BF16),
        compiler_params=_cparams(("parallel", "arbitrary")),
        name="flash_prompt",
    )(jnp.asarray(qi), jnp.asarray(kj), q, k, vt)


def _q_lat_kernel(q_ref, wt_ref, o_ref):
    o_ref[0] = _dot(q_ref[...], wt_ref[0]).astype(BF16)


def _q_lat(q, wukt):
    n = q.shape[0]
    return pl.pallas_call(
        _q_lat_kernel,
        grid=(C_HEADS,),
        in_specs=[
            pl.BlockSpec((n, HEAD_PAD), lambda h: (0, h)),
            pl.BlockSpec((1, HEAD_PAD, C_KV_LORA), lambda h: (h, 0, 0)),
        ],
        out_specs=pl.BlockSpec((1, n, C_KV_LORA), lambda h: (h, 0, 0)),
        out_shape=jax.ShapeDtypeStruct((C_HEADS, n, C_KV_LORA), BF16),
        compiler_params=_cparams(("parallel",)),
        name="q_lat",
    )(q, wukt)


def _attn_sample_kernel(pt_ref, qlat_ref, q128_ref, ckv_ref, krn_ref, kv_hbm, kr_hbm, o_ref,
                        kvbuf, krbuf, sem, newkv_scr, newkr_scr, *, layer, T, n_pages, cp, n_streams):
    b = pl.program_id(0)
    nb = pl.num_programs(0)
    n_chunks = n_pages // cp
    R = T * C_HEADS
    spp = cp // n_streams
    n_slots = ATTN_SLOTS
    ahead = n_slots - 1

    def copies(bb, c):
        slot = c % n_slots
        out = []
        for i in range(cp):
            page = pt_ref[bb, c * cp + i]
            out.append(pltpu.make_async_copy(kv_hbm.at[layer, page], kvbuf.at[slot, i], sem.at[0, slot]))
            out.append(pltpu.make_async_copy(kr_hbm.at[layer, page], krbuf.at[slot, i], sem.at[1, slot]))
        return out

    def start(bb, c):
        for cpy in copies(bb, c):
            cpy.start()

    def wait(bb, c):
        for cpy in copies(bb, c):
            cpy.wait()

    @pl.when(b == 0)
    def _():
        for c in range(ahead):
            start(0, c)

    b_next = jnp.minimum(b + 1, nb - 1)

    qlat = qlat_ref[0]
    qr = q128_ref[0][:, C_NOPE:C_NOPE + C_ROPE]

    def softmax_update(state, s):
        m_prev, l_prev, _ = state
        m_new = jnp.maximum(m_prev, jnp.max(s, axis=-1, keepdims=True))
        alpha = jnp.exp2(m_prev - m_new)
        p = jnp.exp2(s - m_new)
        l_new = alpha * l_prev + jnp.sum(p, axis=-1, keepdims=True)
        return m_new, l_new, alpha, p.astype(BF16)

    def online(state, s, kv):
        m_new, l_new, alpha, p = softmax_update(state, s)
        return m_new, l_new, state[2] * alpha + _dot(p, kv)

    def chunk(c, carry):
        slot = c % n_slots
        wait(b, c)
        kvs, scores = [], []
        for si in range(n_streams):
            kv = kvbuf[slot, si * spp:(si + 1) * spp].reshape(spp * PAGE_SIZE, C_KV_LORA).astype(BF16)
            kr_t = jnp.concatenate([krbuf[slot, si * spp + i] for i in range(spp)], axis=1).astype(BF16)
            kvs.append(kv)
            scores.append(_dot_nt(qlat, kv) + _dot(qr, kr_t))
        if c + ahead < n_chunks:
            start(b, c + ahead)
        else:
            start(b_next, c + ahead - n_chunks)
        stats = [softmax_update(carry[si], scores[si]) for si in range(n_streams)]
        return tuple((m_new, l_new, carry[si][2] * alpha + _dot(p, kvs[si]))
                     for si, (m_new, l_new, alpha, p) in enumerate(stats))

    streams = tuple((jnp.full((R, 1), -jnp.inf, F32), jnp.zeros((R, 1), F32), jnp.zeros((R, C_KV_LORA), F32))
                    for _ in range(n_streams))
    for c in range(n_chunks):
        streams = chunk(c, streams)

    @pl.when(b == nb - 1)
    def _():
        for c in range(ahead):
            wait(b_next, c)

    newkv_scr[...] = jnp.zeros_like(newkv_scr)
    newkr_scr[...] = jnp.zeros_like(newkr_scr)
    newkv_scr[0:T, :] = ckv_ref[0]
    newkr_scr[0:T, :] = krn_ref[0]
    kvn = newkv_scr[...].astype(BF16)
    krn = newkr_scr[...].astype(BF16)
    s = _dot_nt(qlat, kvn) + _dot_nt(qr, krn)
    t_row = lax.broadcasted_iota(jnp.int32, (R, 128), 0) % T
    key = lax.broadcasted_iota(jnp.int32, (R, 128), 1)
    s = jnp.where(key <= t_row, s, -jnp.inf)
    m_all, l_all, acc_all = online(streams[0], s, kvn)
    for m_i, l_i, acc_i in streams[1:]:
        m_new = jnp.maximum(m_all, m_i)
        wa = jnp.exp2(m_all - m_new)
        wi = jnp.exp2(m_i - m_new)
        l_all = wa * l_all + wi * l_i
        acc_all = wa * acc_all + wi * acc_i
        m_all = m_new
    o_ref[0] = (acc_all / l_all).astype(BF16)


def _attn_sample(page_table, qlat, q128, ckv_new, kr_new, cache_kv, cache_kr_t, layer, cp, n_streams):
    B, R, _ = qlat.shape
    T = ckv_new.shape[1]
    n_pages = page_table.shape[1]
    assert (n_pages // cp) % ATTN_SLOTS == 0 and cp % n_streams == 0
    kern = functools.partial(_attn_sample_kernel, layer=layer, T=T, n_pages=n_pages, cp=cp, n_streams=n_streams)
    grid_spec = pltpu.PrefetchScalarGridSpec(
        num_scalar_prefetch=1,
        grid=(B,),
        in_specs=[
            pl.BlockSpec((1, R, C_KV_LORA), lambda b, pt: (b, 0, 0)),
            pl.BlockSpec((1, R, HEAD_PAD), lambda b, pt: (b, 0, 0)),
            pl.BlockSpec((1, T, C_KV_LORA), lambda b, pt: (b, 0, 0)),
            pl.BlockSpec((1, T, C_ROPE), lambda b, pt: (b, 0, 0)),
            pl.BlockSpec(memory_space=pl.ANY),
            pl.BlockSpec(memory_space=pl.ANY),
        ],
        out_specs=pl.BlockSpec((1, R, C_KV_LORA), lambda b, pt: (b, 0, 0)),
        scratch_shapes=[
            pltpu.VMEM((ATTN_SLOTS, cp, PAGE_SIZE, C_KV_LORA), F32),
            pltpu.VMEM((ATTN_SLOTS, cp, C_ROPE, PAGE_SIZE), F32),
            pltpu.SemaphoreType.DMA((2, ATTN_SLOTS)),
            pltpu.VMEM((128, C_KV_LORA), F32),
            pltpu.VMEM((128, C_ROPE), F32),
        ],
    )
    return pl.pallas_call(
        kern,
        grid_spec=grid_spec,
        out_shape=jax.ShapeDtypeStruct((B, R, C_KV_LORA), BF16),
        compiler_params=_cparams(("arbitrary",)),
        name="attn_sample",
    )(page_table, qlat, q128, ckv_new, kr_new, cache_kv, cache_kr_t)


def _uv_proj_kernel(o_ref, wt_ref, y_ref):
    y_ref[0] = _dot_nt(wt_ref[0], o_ref[0]).astype(BF16)


def _uv_proj(olat, wuvt):
    H, n, _ = olat.shape
    return pl.pallas_call(
        _uv_proj_kernel,
        grid=(H,),
        in_specs=[
            pl.BlockSpec((1, n, C_KV_LORA), lambda h: (h, 0, 0)),
            pl.BlockSpec((1, C_V, C_KV_LORA), lambda h: (h, 0, 0)),
        ],
        out_specs=pl.BlockSpec((1, C_V, n), lambda h: (h, 0, 0)),
        out_shape=jax.ShapeDtypeStruct((H, C_V, n), BF16),
        compiler_params=_cparams(("parallel",)),
        name="uv_proj",
    )(olat, wuvt)


def _merge_kernel(x_ref, g_ref, wg_ref, ya_ref, yb_ref, yct_ref, wpa_ref, wpb_ref, wpc_ref, wo_ref, o_ref):
    x = x_ref[...]
    gates = _dot_nt(_rms(x, g_ref[...]).astype(BF16), wg_ref[...])
    m = _sigmoid(gates[:, 0:D_MODEL]) * _dot(ya_ref[...], wpa_ref[...])
    m = m + _sigmoid(gates[:, D_MODEL:2 * D_MODEL]) * _dot(yb_ref[...], wpb_ref[...])
    m = m + _sigmoid(gates[:, 2 * D_MODEL:]) * _dot_tn(yct_ref[...], wpc_ref[...])
    o_ref[...] = x + _dot(m.astype(BF16), wo_ref[...])


def _merge(x, g_mix, w_in_t, layer, ya, yb, yct, wpa, wpb, wpc, wo, tm):
    n_tok = x.shape[0]
    c2 = lambda i: (0, 0)
    return pl.pallas_call(
        _merge_kernel,
        grid=(n_tok // tm,),
        in_specs=[
            pl.BlockSpec((tm, D_MODEL), lambda i: (i, 0)),
            pl.BlockSpec((1, D_MODEL), c2),
            pl.BlockSpec((None, N_BRANCH * D_MODEL, D_MODEL), lambda i: (layer, 0, 0)),
            pl.BlockSpec((tm, A_WIDTH), lambda i: (i, 0)),
            pl.BlockSpec((tm, B_INNER), lambda i: (i, 0)),
            pl.BlockSpec((C_HEADS * C_V, tm), lambda i: (0, i)),
            pl.BlockSpec((A_WIDTH, D_MODEL), c2),
            pl.BlockSpec((B_INNER, D_MODEL), c2),
            pl.BlockSpec((C_HEADS * C_V, D_MODEL), c2),
            pl.BlockSpec((D_MODEL, D_MODEL), c2),
        ],
        out_specs=pl.BlockSpec((tm, D_MODEL), lambda i: (i, 0)),
        out_shape=jax.ShapeDtypeStruct((n_tok, D_MODEL), F32),
        compiler_params=_cparams(("parallel",)),
        name="merge",
    )(x, g_mix, w_in_t, ya, yb, yct, wpa, wpb, wpc, wo)


def _ffn_ple_kernel(x_ref, g_ref, wu_ref, wd_ref, gp_ref, wg_ref, p_ref, wp_ref, gf_ref, o_ref,
                    h_scr, acc_scr, *, final):
    j = pl.program_id(1)

    @pl.when(j == 0)
    def _():
        x = x_ref[...]
        h_scr[...] = _rms(x, g_ref[...]).astype(BF16)
        acc_scr[...] = x

    u = jnp.maximum(_dot(h_scr[...], wu_ref[...]), 0.0)
    acc_scr[...] += _dot((u * u).astype(BF16), wd_ref[...])

    @pl.when(j == pl.num_programs(1) - 1)
    def _():
        x = acc_scr[...]
        pg = _sigmoid(_dot(_rms(x, gp_ref[...]).astype(BF16), wg_ref[...]))
        y = x + pg * _dot(p_ref[...].astype(BF16), wp_ref[...])
        if final:
            y = _rms(y, gf_ref[...])
        o_ref[...] = y


def _ffn_ple(x, g, wu, wd, gp, wg, p, layer, wp, gf, tm, tf, final):
    n_tok = x.shape[0]
    d_ple = p.shape[2]
    c2 = lambda i, j: (0, 0)
    return pl.pallas_call(
        functools.partial(_ffn_ple_kernel, final=final),
        grid=(n_tok // tm, D_FF // tf),
        in_specs=[
            pl.BlockSpec((tm, D_MODEL), lambda i, j: (i, 0)),
            pl.BlockSpec((1, D_MODEL), c2),
            pl.BlockSpec((D_MODEL, tf), lambda i, j: (0, j)),
            pl.BlockSpec((tf, D_MODEL), lambda i, j: (j, 0)),
            pl.BlockSpec((1, D_MODEL), c2),
            pl.BlockSpec((D_MODEL, D_MODEL), c2),
            pl.BlockSpec((None, tm, d_ple), lambda i, j: (layer, i, 0)),
            pl.BlockSpec((d_ple, D_MODEL), c2),
            pl.BlockSpec((1, D_MODEL), c2),
        ],
        out_specs=pl.BlockSpec((tm, D_MODEL), lambda i, j: (i, 0)),
        out_shape=jax.ShapeDtypeStruct((n_tok, D_MODEL), F32),
        scratch_shapes=[pltpu.VMEM((tm, D_MODEL), BF16), pltpu.VMEM((tm, D_MODEL), F32)],
        compiler_params=_cparams(("parallel", "arbitrary")),
        name="ffn_ple",
    )(x, g, wu, wd, gp, wg, p, wp, gf)


def _prep_layer_weights(w_in_tail, w_uq, w_ukv):
    idx = [int(v) for v in np.cumsum((B_HEADS, C_Q_LORA, C_KV_LORA))]
    dt, c_q, c_kv, k_r = jnp.split(w_in_tail, idx, axis=0)
    half = C_ROPE // 2
    zr = lambda n: jnp.zeros((n, D_MODEL), BF16)
    k_rs = jnp.concatenate([k_r[half:], k_r[:half]], axis=0)
    w_tail_t = jnp.concatenate([
        c_q, dt, zr(128 - B_HEADS), c_kv,
        zr(C_NOPE), k_r, zr(HEAD_PAD - C_NOPE - C_ROPE),
        zr(C_NOPE), k_rs, zr(HEAD_PAD - C_NOPE - C_ROPE)], axis=0)

    uq = w_uq.reshape(C_Q_LORA, C_HEADS, C_NOPE + C_ROPE)
    uq_n, uq_r = uq[..., :C_NOPE], uq[..., C_NOPE:]
    uq_rs = jnp.concatenate([uq_r[..., half:], uq_r[..., :half]], axis=-1)
    zq = lambda n: jnp.zeros((C_Q_LORA, C_HEADS, n), w_uq.dtype)
    wq1 = jnp.concatenate([uq_n, uq_r, zq(HEAD_PAD - C_NOPE - C_ROPE)], axis=-1).reshape(C_Q_LORA, -1).astype(BF16)
    wq2 = jnp.concatenate([zq(C_NOPE), uq_rs, zq(HEAD_PAD - C_NOPE - C_ROPE)], axis=-1).reshape(C_Q_LORA, -1).astype(BF16)

    ukv = w_ukv.reshape(C_KV_LORA, C_HEADS, C_NOPE + C_V)
    uk, uv = ukv[..., :C_NOPE], ukv[..., C_NOPE:]
    wk = jnp.concatenate([uk, jnp.zeros((C_KV_LORA, C_HEADS, HEAD_PAD - C_NOPE), w_ukv.dtype)], axis=-1)
    wk = wk.reshape(C_KV_LORA, -1).astype(BF16)
    wv = jnp.concatenate([uv, jnp.zeros((C_KV_LORA, C_HEADS, V_ROWS - C_V), w_ukv.dtype)], axis=-1)
    wv = jnp.transpose(wv, (1, 2, 0)).astype(BF16)
    wukt = jnp.transpose(wk.reshape(C_KV_LORA, C_HEADS, HEAD_PAD), (1, 2, 0))
    wuv_h = jnp.transpose(uv, (1, 2, 0)).astype(BF16)
    return w_tail_t, wq1, wq2, wk, wv, wukt, wuv_h


def _rope_angles(pos):
    half = C_ROPE // 2
    inv = jnp.power(ROPE_BASE, -jnp.arange(half, dtype=F32) * (2.0 / C_ROPE))
    ang = pos.astype(F32)[:, None] * inv[None, :]
    return jnp.cos(ang), jnp.sin(ang)


def _rope_tables(pos=None, n_arange=None):
    if pos is not None:
        cos, sin = _rope_angles(pos)
    else:
        assert n_arange % 128 == 0
        ch, sh = _rope_angles(jnp.arange(n_arange // 128) * 128)
        cl, sl = _rope_angles(jnp.arange(128))
        cos = (ch[:, None] * cl[None] - sh[:, None] * sl[None]).reshape(n_arange, -1)
        sin = (sh[:, None] * cl[None] + ch[:, None] * sl[None]).reshape(n_arange, -1)
    n = cos.shape[0]
    pad = jnp.zeros((n, HEAD_PAD - C_NOPE - C_ROPE), F32)
    cos_t = jnp.concatenate([jnp.ones((n, C_NOPE), F32), cos, cos, pad], axis=1)
    sin_t = jnp.concatenate([jnp.zeros((n, C_NOPE), F32), -sin, sin, pad], axis=1)
    return cos_t, sin_t


def _pad128(v):
    return jnp.concatenate([v, jnp.zeros((128 - v.shape[0],), v.dtype)])[None, :]


def _token_tile(n, pref):
    t = pref
    while n % t:
        t //= 2
    return t


def kernel(x_prompt, x_sample, cache_kv_latent, cache_k_rope, state_ssm, state_conv, page_table, p_prompt, p_sample, ln_mix, w_in, sgu_ln_w, sgu_ln_b, w_s, b_s, conv_w, conv_b, dt_bias, a_log, d_skip, b_norm, q_norm, w_uq, kv_norm, w_ukv, w_pa, w_pb, w_pc, w_o, ln_ffn, w_up, w_down, ln_ple, w_ple_gate, w_ple, ln_final):
    depth = w_in.shape[0]
    _, S, _ = x_prompt.shape
    B, T, _ = x_sample.shape
    n_pages = page_table.shape[1]
    past_len = n_pages * PAGE_SIZE
    ns = B * T
    assert x_prompt.shape[0] == 1 and S % CHUNK == 0 and ns % CHUNK == 0 and CHUNK % T == 0

    xp = x_prompt.reshape(S, D_MODEL)
    xs = x_sample.reshape(ns, D_MODEL)
    cos_p, sin_p = _rope_tables(n_arange=S)
    cos_s, sin_s = (jnp.tile(t, (B, 1)) for t in _rope_tables(pos=past_len + jnp.arange(T)))

    hp = np.arange(B_INNER) // B_HEADDIM
    e_mat = jnp.asarray((np.arange(128)[:, None] == hp[None, :]).astype(np.float32)).astype(BF16)
    blockmask = jnp.asarray((np.arange(B_HEADS)[:, None] == hp[None, :]).astype(np.float32))
    gn = np.arange(B_GROUPS * B_STATE) // B_STATE
    hg = np.where(np.arange(128) < B_HEADS, np.arange(128) // (B_HEADS // B_GROUPS), -1)
    s_mat = jnp.asarray((gn[:, None] == hg[None, :]).astype(np.float32)).astype(BF16)

    tm_p = _token_tile(S, 1024)
    tm_s = _token_tile(ns, 512)
    tq = _token_tile(S, FLASH_TILE)
    cp = _token_tile(n_pages // ATTN_SLOTS, ATTN_PAGES_PER_CHUNK)
    cache_kr_t = jnp.swapaxes(cache_k_rope, 2, 3)
    tile_rep = CHUNK // T
    eye_rep = jnp.eye(tile_rep, dtype=F32)

    w_in_t = jnp.swapaxes(w_in, 1, 2).astype(BF16)

    outs_p, outs_s = [], []
    ssm_s = None
    for i in range(depth):
        w_tail_t, wq1, wq2, wk, wv, wukt, wuv_h = _prep_layer_weights(w_in_t[i, IN_TAIL0:], w_uq[i], w_ukv[i])
        g_mix = ln_mix[i][None, :]
        lnw, lnb = sgu_ln_w[i][None, :], sgu_ln_b[i][None, :]
        ws_p = w_s[i][:, :CHUNK, :CHUNK]
        bs_p = b_s[i][:, :CHUNK, None]
        ws_t = jnp.tril(w_s[i][:, :T, :T])
        ws_s = jnp.einsum('ab,gts->gatbs', eye_rep, ws_t).reshape(A_GROUPS, CHUNK, CHUNK)
        bs_s = jnp.tile(b_s[i][:, :T], (1, tile_rep))[:, :, None]
        cw, cb = conv_w[i], conv_b[i][None, :]
        dtb128, alog128 = _pad128(dt_bias[i]), _pad128(a_log[i])
        dsk_e = jnp.repeat(d_skip[i], B_HEADDIM)[None, :]
        bn = b_norm[i][None, :]
        qn, kvn = q_norm[i][None, :], kv_norm[i][None, :]
        wpa, wpb, wpc, wo = (w.astype(BF16) for w in (w_pa[i], w_pb[i], w_pc[i], w_o[i]))
        wu, wd = w_up[i].astype(BF16), w_down[i].astype(BF16)
        wg, wp = w_ple_gate[i].astype(BF16), w_ple[i].astype(BF16)
        g_ffn, g_ple, g_fin = ln_ffn[i][None, :], ln_ple[i][None, :], ln_final[None, :]
        final = i == depth - 1

        proj = _in_proj(xp, g_mix, w_tail_t, w_in_t, i, _token_tile(S, 2048))
        ya, av = _gate_a(proj, lnw, lnb, ws_p, bs_p, _token_tile(S, 512))
        yb, ssm_t = _ssd_prompt(proj, cw, cb, dtb128, alog128, dsk_e, bn, e_mat)
        ssm_p = jnp.transpose(ssm_t.reshape(B_GROUPS, B_STATE, B_HEADS // B_GROUPS, B_HEADDIM),
                              (0, 2, 3, 1)).reshape(1, B_HEADS, B_HEADDIM, B_STATE)
        q, k, v, ckvn, krope = _c_prep(proj, cos_p, sin_p, qn, kvn, wq1, wq2, wk, wv, _token_tile(S, 512))
        yc = _flash(q, k, v, tq, FLASH_HEADS_PER_STEP)
        xp = _merge(xp, g_mix, w_in_t, i, ya, yb, yc, wpa, wpb, wpc, wo, _token_tile(S, 512))
        xp = _ffn_ple(xp, g_ffn, wu, wd, g_ple, wg, p_prompt.reshape(depth, S, -1), i, wp, g_fin, tm_p, 1024, final)
        outs_p.append((ckvn.reshape(1, S, C_KV_LORA), krope.reshape(1, S, C_ROPE),
                       ssm_p,
                       proj[S - (B_CONV - 1):, COL_XBC:COL_XBC + B_CONV_DIM][None],
                       av[S - CHUNK:][None]))

        proj = _in_proj(xs, g_mix, w_tail_t, w_in_t, i, tm_s)
        ya, av = _gate_a(proj, lnw, lnb, ws_s, bs_s, tm_s)
        xbc_s = proj[:, COL_XBC:COL_XBC + B_CONV_DIM].reshape(B, T, B_CONV_DIM)
        z_s = proj[:, COL_Z:COL_Z + B_INNER].reshape(B, T, B_INNER)
        dt_s = proj[:, COL_DT:COL_DT + 128].reshape(B, T, 128)
        yb, ssm_s = _ssd_sample(xbc_s, state_conv, z_s, dt_s, state_ssm, i, cw, cb, dtb128, alog128,
                                dsk_e, bn, e_mat, blockmask, s_mat, ssm_s)
        q, _, _, ckvn, krope = _c_prep(proj, cos_s, sin_s, qn, kvn, wq1, wq2, wk, wv, tm_s)
        qlat = _q_lat(q, wukt)
        qlat = jnp.transpose(qlat.reshape(C_HEADS, B, T, C_KV_LORA), (1, 0, 2, 3)).reshape(B, C_HEADS * T, C_KV_LORA)
        q128 = jnp.transpose(q.reshape(B, T, C_HEADS, HEAD_PAD), (0, 2, 1, 3)).reshape(B, C_HEADS * T, HEAD_PAD)
        olat = _attn_sample(page_table, qlat, q128, ckvn.reshape(B, T, C_KV_LORA), krope.reshape(B, T, C_ROPE),
                            cache_kv_latent, cache_kr_t, i, cp, min(ATTN_STREAMS, cp))
        olat = jnp.transpose(olat.reshape(B, C_HEADS, T, C_KV_LORA), (1, 0, 2, 3)).reshape(C_HEADS, ns, C_KV_LORA)
        yc = _uv_proj(olat, wuv_h).reshape(C_HEADS * C_V, ns)
        xs = _merge(xs, g_mix, w_in_t, i, ya, yb.reshape(ns, B_INNER), yc, wpa, wpb, wpc, wo, tm_s)
        xs = _ffn_ple(xs, g_ffn, wu, wd, g_ple, wg, p_sample.reshape(depth, ns, -1), i, wp, g_fin, tm_s, 1024, final)
        outs_s.append((ckvn.reshape(B, T, C_KV_LORA), krope.reshape(B, T, C_ROPE),
                       xbc_s[:, T - (B_CONV - 1):], av.reshape(B, T, A_WIDTH)))

    kv_p, kr_p, ssm_p, conv_p, v_p = [jnp.stack(t) for t in zip(*outs_p)]
    kv_s, kr_s, conv_s, v_s = [jnp.stack(t) for t in zip(*outs_s)]
    return (xp.reshape(1, S, D_MODEL), xs.reshape(B, T, D_MODEL), kv_p, kr_p, ssm_p, conv_p, v_p,
            kv_s, kr_s, ssm_s, conv_s, v_s)
```
